```python
import math
import jax, jax.numpy as jnp
from jax import lax
import numpy as np

D_MODEL = 2048
BATCH = 2
SEQ = 16384
DEPTH = 4

CHUNK = 64
N_MIXERS = 4
D_MIX = D_MODEL
GROUP_W = D_MIX // N_MIXERS
MLA_HEADS = 4
QK_NOPE = 128
QK_ROPE = 64
V_HEAD = 128
KV_RANK = 512
ROPE_THETA = 10000.0
Q_BLOCK = 128
CONV_W = 3
CONV_HEADS = 4
CONV_CH = GROUP_W
POOL_WINDOWS = (2, 4, 8, 16)
POOL_GROUPS = 4
POOL_CH = GROUP_W // POOL_GROUPS
SGU_LEN = 128
SGU_GROUPS = 4
SGU_CH = GROUP_W // SGU_GROUPS
D_FF = 5632
N_EXPERTS = 8
TOP_K = 2
EXPERT_BLOCK = 256
N_DENSE = (DEPTH + 1) // 2
N_MOE = DEPTH // 2
ALPHA = (2.0 * DEPTH) ** 0.25
BETA = (8.0 * DEPTH) ** -0.25
LN_EPS = 1e-5
RMS_EPS = 1e-6

Q_DIM = MLA_HEADS * (QK_NOPE + QK_ROPE)
SPLIT_SIZES = (Q_DIM, KV_RANK, QK_ROPE, CONV_CH, CONV_CH, CONV_CH, GROUP_W, GROUP_W, GROUP_W)
IN_COLS = sum(SPLIT_SIZES)

kernel_name = 'hybrid_mla_conv_pool_sgu_moe_deepnorm'

F32 = jnp.float32


def layer_norm(x, g, b):
    xf = x.astype(F32)
    mu = jnp.mean(xf, axis=-1, keepdims=True)
    var = jnp.mean(jnp.square(xf - mu), axis=-1, keepdims=True)
    return ((xf - mu) * lax.rsqrt(var + LN_EPS) * g + b).astype(x.dtype)


def rms_normalise(x):
    xf = x.astype(F32)
    return (xf * lax.rsqrt(jnp.mean(xf * xf, axis=-1, keepdims=True) + RMS_EPS)).astype(x.dtype)


def rope_tables(positions):
    inv_freq = ROPE_THETA ** (-jnp.arange(0, QK_ROPE, 2, dtype=F32) / QK_ROPE)
    ang = positions.astype(F32)[..., None] * inv_freq
    return jnp.cos(ang), jnp.sin(ang)


def apply_rope(x, cos, sin):
    xf = x.astype(F32)
    x1, x2 = jnp.split(xf, 2, axis=-1)
    return jnp.concatenate([x1 * cos - x2 * sin, x1 * sin + x2 * cos], axis=-1).astype(x.dtype)


def mla_attention(q, c_kv, k_rope, kv_norm_g, w_ukv, cos, sin):
    B, S, _ = q.shape
    q = q.reshape(B, S, MLA_HEADS, QK_NOPE + QK_ROPE)
    q_nope = q[..., :QK_NOPE]
    q_rope = apply_rope(q[..., QK_NOPE:], cos[:, :, None, :], sin[:, :, None, :])
    k_rope = apply_rope(k_rope, cos, sin)
    kv = jnp.einsum('bsr,rn->bsn', rms_normalise(c_kv) * kv_norm_g, w_ukv)
    kv = kv.reshape(B, S, MLA_HEADS, QK_NOPE + V_HEAD)
    k_nope, v = kv[..., :QK_NOPE], kv[..., QK_NOPE:]
    scale = 1.0 / math.sqrt(QK_NOPE + QK_ROPE)
    n_blk = S // Q_BLOCK
    k_chunk = jnp.arange(S) // CHUNK

    def query_block(args):
        qn, qr, blk = args
        s = (jnp.einsum('bqhd,bkhd->bhqk', qn, k_nope)
             + jnp.einsum('bqhr,bkr->bhqk', qr, k_rope))
        q_chunk = (blk * Q_BLOCK + jnp.arange(Q_BLOCK)) // CHUNK
        allowed = k_chunk[None, :] <= q_chunk[:, None]
        s = jnp.where(allowed, s.astype(F32) * scale, -jnp.inf)
        p = jax.nn.softmax(s, axis=-1).astype(v.dtype)
        return jnp.einsum('bhqk,bkhd->bqhd', p, v)

    def to_blocks(t):
        return jnp.moveaxis(t.reshape(B, n_blk, Q_BLOCK, *t.shape[2:]), 1, 0)

    o = lax.map(query_block, (to_blocks(q_nope), to_blocks(q_rope), jnp.arange(n_blk)))
    return jnp.moveaxis(o, 0, 1).reshape(B, S, MLA_HEADS * V_HEAD)


def short_conv(b_gate, c_gate, h, conv_w):
    S = h.shape[1]
    g = c_gate * h
    gp = jnp.pad(g, ((0, 0), (CONV_W - 1, 0), (0, 0)))
    conv = conv_w[0] * gp[:, 0:S]
    for j in range(1, CONV_W):
        conv = conv + conv_w[j] * gp[:, j:j + S]
    return b_gate * conv


def pool_mixer(h, pool_w):
    B, S, _ = h.shape
    hf = h.astype(F32)
    cs = jnp.cumsum(hf, axis=1)
    t = jnp.arange(S)
    outs = []
    for g, w in enumerate(POOL_WINDOWS):
        sl = slice(g * POOL_CH, (g + 1) * POOL_CH)
        csg = cs[..., sl]
        lagged = jnp.pad(csg, ((0, 0), (w, 0), (0, 0)))[:, :S]
        count = jnp.minimum(t + 1, w).astype(F32)[None, :, None]
        outs.append((csg - lagged) / count - hf[..., sl])
    p = jnp.stack(outs, axis=2).astype(h.dtype)
    return jnp.einsum('bsgc,gcd->bsgd', p, pool_w).reshape(B, S, GROUP_W)


def spatial_gating(u, v, ln_g, ln_b, sgu_w, sgu_b):
    B, S, _ = u.shape
    u = jax.nn.gelu(u, approximate=False)
    v = layer_norm(jax.nn.gelu(v, approximate=False), ln_g, ln_b)
    n = S // SGU_LEN
    causal = jnp.tril(jnp.ones((SGU_LEN, SGU_LEN), dtype=bool))
    w = jnp.where(causal[None], sgu_w, jnp.zeros_like(sgu_w))
    vb = v.reshape(B, n, SGU_LEN, SGU_GROUPS, SGU_CH)
    mixed = jnp.einsum('gts,bnsgc->bntgc', w, vb) + jnp.transpose(sgu_b)[None, None, :, :, None]
    return u * mixed.reshape(B, S, GROUP_W)


def swiglu(x, wg, wu, wd):
    return jnp.matmul(jax.nn.silu(jnp.matmul(x, wg)) * jnp.matmul(x, wu), wd)


def moe_ffn(x, router_w, wg, wu, wd):
    B, S, D = x.shape
    T = B * S
    A = T * TOP_K
    xf = x.reshape(T, D)
    logits = jnp.matmul(xf, router_w).astype(F32)
    top_logit, top_e = lax.top_k(logits, TOP_K)
    gates = jax.nn.softmax(top_logit, axis=-1)
    flat_e = top_e.reshape(A)
    order = jnp.argsort(flat_e)
    sorted_e = flat_e[order]
    tok = (order // TOP_K).astype(jnp.int32)
    gate_sorted = gates.reshape(A)[order]
    counts = jnp.bincount(flat_e, length=N_EXPERTS)
    padded = (counts + EXPERT_BLOCK - 1) // EXPERT_BLOCK * EXPERT_BLOCK
    starts = jnp.cumsum(counts) - counts
    pad_ends = jnp.cumsum(padded)
    pad_starts = pad_ends - padded
    dest = pad_starts[sorted_e] + jnp.arange(A) - starts[sorted_e]
    n_blocks = -(-(A + N_EXPERTS * (EXPERT_BLOCK - 1)) // EXPERT_BLOCK)
    n_pad = n_blocks * EXPERT_BLOCK
    slot_tok = jnp.full((n_pad,), T, dtype=jnp.int32).at[dest].set(tok)
    x_rows = jnp.concatenate([xf, jnp.zeros((1, D), xf.dtype)], axis=0)[slot_tok]
    x_rows = x_rows.reshape(n_blocks, EXPERT_BLOCK, D)
    block_e = jnp.minimum(
        jnp.searchsorted(pad_ends, jnp.arange(n_blocks) * EXPERT_BLOCK, side='right'), N_EXPERTS - 1)

    def expert_block(args):
        xb, e = args
        return swiglu(xb, wg[e], wu[e], wd[e])

    y_rows = lax.map(expert_block, (x_rows, block_e)).reshape(n_pad, D)
    contrib = y_rows[dest] * gate_sorted[:, None].astype(x.dtype)
    return jax.ops.segment_sum(contrib, tok, num_segments=T).reshape(B, S, D)


def setup_inputs(seed: int = 0) -> dict:
    key = jax.random.key(seed)
    ks = jax.random.split(key, 26)
    L = DEPTH

    def nrm(k, shape, scale):
        return jax.random.normal(k, shape, F32) * scale

    x = nrm(ks[0], (BATCH, SEQ, D_MODEL), 1.0)
    positions = (jax.random.randint(ks[1], (BATCH, 1), 0, 4096, dtype=jnp.int32)
                 + jnp.arange(SEQ, dtype=jnp.int32)[None, :])
    return {
        'x': x,
        'positions': positions,
        'w_in': nrm(ks[2], (L, D_MODEL, IN_COLS), D_MODEL ** -0.5),
        'kv_norm_g': 1.0 + nrm(ks[3], (L, KV_RANK), 0.02),
        'w_ukv': nrm(ks[4], (L, KV_RANK, MLA_HEADS * (QK_NOPE + V_HEAD)), KV_RANK ** -0.5),
        'conv_w': nrm(ks[5], (L, CONV_W, CONV_CH), CONV_W ** -0.5),
        'pool_w': nrm(ks[6], (L, POOL_GROUPS, POOL_CH, POOL_CH), POOL_CH ** -0.5),
        'sgu_ln_g': 1.0 + nrm(ks[7], (L, GROUP_W), 0.02),
        'sgu_ln_b': nrm(ks[8], (L, GROUP_W), 0.02),
        'sgu_w': nrm(ks[9], (L, SGU_GROUPS, SGU_LEN, SGU_LEN), SGU_LEN ** -0.5),
        'sgu_b': 1.0 + nrm(ks[10], (L, SGU_GROUPS, SGU_LEN), 0.02),
        'mix_gain': 1.0 + nrm(ks[11], (L, D_MIX), 0.02),
        'w_o': nrm(ks[12], (L, D_MIX, D_MODEL), BETA * D_MIX ** -0.5),
        'ln1_g': 1.0 + nrm(ks[13], (L, D_MODEL), 0.02),
        'ln1_b': nrm(ks[14], (L, D_MODEL), 0.02),
        'ffn_wg': nrm(ks[15], (N_DENSE, D_MODEL, D_FF), D_MODEL ** -0.5),
        'ffn_wu': nrm(ks[16], (N_DENSE, D_MODEL, D_FF), D_MODEL ** -0.5),
        'ffn_wd': nrm(ks[17], (N_DENSE, D_FF, D_MODEL), BETA * D_FF ** -0.5),
        'router_w': nrm(ks[18], (N_MOE, D_MODEL, N_EXPERTS), D_MODEL ** -0.5),
        'exp_wg': nrm(ks[19], (N_MOE, N_EXPERTS, D_MODEL, D_FF), D_MODEL ** -0.5),
        'exp_wu': nrm(ks[20], (N_MOE, N_EXPERTS, D_MODEL, D_FF), D_MODEL ** -0.5),
        'exp_wd': nrm(ks[21], (N_MOE, N_EXPERTS, D_FF, D_MODEL), BETA * D_FF ** -0.5),
        'ln2_g': 1.0 + nrm(ks[22], (L, D_MODEL), 0.02),
        'ln2_b': nrm(ks[23], (L, D_MODEL), 0.02),
    }


def reference(x, positions, w_in, kv_norm_g, w_ukv, conv_w, pool_w, sgu_ln_g, sgu_ln_b, sgu_w, sgu_b,
              mix_gain, w_o, ln1_g, ln1_b, ffn_wg, ffn_wu, ffn_wd, router_w, exp_wg, exp_wu, exp_wd,
              ln2_g, ln2_b):
    B, S, _ = x.shape
    cos, sin = rope_tables(positions)
    offsets = [sum(SPLIT_SIZES[:i + 1]) for i in range(len(SPLIT_SIZES) - 1)]
    for l in range(DEPTH):
        proj = jnp.einsum('bsd,dn->bsn', x, w_in[l])
        q, c_kv, k_rope, cb, cc, ch, ph, gu, gv = jnp.split(proj, offsets, axis=-1)
        y_mla = mla_attention(q, c_kv, k_rope, kv_norm_g[l], w_ukv[l], cos, sin)
        y_conv = short_conv(cb, cc, ch, conv_w[l])
        y_pool = pool_mixer(ph, pool_w[l])
        y_sgu = spatial_gating(gu, gv, sgu_ln_g[l], sgu_ln_b[l], sgu_w[l], sgu_b[l])
        groups = jnp.stack([y_mla, y_conv, y_pool, y_sgu], axis=2)
        merged = (rms_normalise(groups) * mix_gain[l].reshape(N_MIXERS, GROUP_W)).reshape(B, S, D_MIX)
        mix = jnp.matmul(merged, w_o[l])
        x = layer_norm(ALPHA * x + mix, ln1_g[l], ln1_b[l])
        if l % 2 == 0:
            ffn = swiglu(x, ffn_wg[l // 2], ffn_wu[l // 2], ffn_wd[l // 2])
        else:
            ffn = moe_ffn(x, router_w[l // 2], exp_wg[l // 2], exp_wu[l // 2], exp_wd[l // 2])
        x = layer_norm(ALPHA * x + ffn, ln2_g[l], ln2_b[l])
    return x
```

```python
import functools
import math

import jax
import jax.numpy as jnp
from jax import lax
from jax.experimental import pallas as pl
from jax.experimental.pallas import tpu as pltpu

F32 = jnp.float32
BF16 = jnp.bfloat16

D_MODEL = 2048
CHUNK = 64
GROUP_W = 512
N_GROUPS = 4
MLA_HEADS = 4
QK_NOPE = 128
QK_ROPE = 64
V_HEAD = 128
KV_RANK = 512
ROPE_THETA = 10000.0
CONV_W = 3
POOL_WINDOWS = (2, 4, 8, 16)
POOL_CH = 128
SGU_LEN = 128
SGU_GROUPS = 4
SGU_CH = 128
D_FF = 5632
N_EXPERTS = 8
TOP_K = 2
DEPTH = 4
ALPHA = (2.0 * DEPTH) ** 0.25
LN_EPS = 1e-5
RMS_EPS = 1e-6

LANES = 128
HEAD_W = 2 * LANES
VMEM_LIMIT = 56 * 1024 * 1024

Q_COLS = MLA_HEADS * HEAD_W
COL_Q, COL_CKV, COL_CB, COL_CC, COL_CH, COL_PH, COL_GU, COL_GV = 0, 2, 3, 4, 5, 6, 7, 8
IN_COLS_PAD = Q_COLS + 7 * GROUP_W

HALO = 32

TM_PROJ, TN_PROJ = 1024, 512
TM_PREP = 512
TQ_ATT = 512
TM_MIX = 512
TM_OUT = 256
TM_FFN, TF_FFN = 512, 512


def _cparams(sem):
    return pltpu.CompilerParams(dimension_semantics=sem, vmem_limit_bytes=VMEM_LIMIT)


def _layer_norm_rows(z, g, b):
    mu = jnp.mean(z, axis=-1, keepdims=True)
    zc = z - mu
    var = jnp.mean(zc * zc, axis=-1, keepdims=True)
    return zc * lax.rsqrt(var + LN_EPS) * g + b


def _rms_rows(y):
    return y * lax.rsqrt(jnp.mean(y * y, axis=-1, keepdims=True) + RMS_EPS)


def _matmul_kernel(x_ref, w_ref, o_ref):
    o_ref[...] = jnp.dot(x_ref[...], w_ref[...], preferred_element_type=F32)


def _in_proj(xb, w):
    T, K = xb.shape
    N = w.shape[1]
    tm = min(TM_PROJ, T)
    return pl.pallas_call(
        _matmul_kernel,
        grid=(T // tm, N // TN_PROJ),
        in_specs=[pl.BlockSpec((tm, K), lambda i, j: (i, 0)),
                  pl.BlockSpec((K, TN_PROJ), lambda i, j: (0, j))],
        out_specs=pl.BlockSpec((tm, TN_PROJ), lambda i, j: (i, j)),
        out_shape=jax.ShapeDtypeStruct((T, N), F32),
        compiler_params=_cparams(("parallel", "arbitrary")),
        name="in_proj",
    )(xb, w)


def _mla_prep_kernel(q_ref, ckv_ref, cos_ref, sin_ref, g_ref, wukv_ref, qo_ref, ko_ref, vo_ref, *, qscale):
    cos = cos_ref[...]
    sin = sin_ref[...]
    lane = lax.broadcasted_iota(jnp.int32, cos.shape, 1)
    first_half = (lane & (QK_ROPE // 2)) == 0
    low = lane < QK_ROPE

    def rope(r):
        partner = jnp.where(first_half, pltpu.roll(r, LANES - QK_ROPE // 2, axis=1), pltpu.roll(r, QK_ROPE // 2, axis=1))
        return r * cos + partner * sin

    k_rope = None
    for h in range(MLA_HEADS):
        c0 = h * HEAD_W
        qo_ref[:, c0:c0 + LANES] = (q_ref[:, c0:c0 + LANES] * qscale).astype(BF16)
        rr = rope(q_ref[:, c0 + LANES:c0 + HEAD_W])
        qo_ref[:, c0 + LANES:c0 + HEAD_W] = jnp.where(low, rr * qscale, 0.0).astype(BF16)
        if h == 0:
            k_rope = jnp.where(low, pltpu.roll(rr, QK_ROPE, axis=1), 0.0).astype(BF16)

    c = ckv_ref[...]
    cn = (_rms_rows(c) * g_ref[...]).astype(BF16)
    kv = jnp.dot(cn, wukv_ref[...], preferred_element_type=F32)
    for h in range(MLA_HEADS):
        c0 = h * HEAD_W
        ko_ref[:, c0:c0 + LANES] = kv[:, c0:c0 + LANES].astype(BF16)
        ko_ref[:, c0 + LANES:c0 + HEAD_W] = k_rope
        vo_ref[:, h * V_HEAD:(h + 1) * V_HEAD] = kv[:, c0 + LANES:c0 + HEAD_W].astype(BF16)


def _mla_prep(proj, cos4, sin4, kv_g, w_ukv_b):
    T = proj.shape[0]
    tm = min(TM_PREP, T)
    qscale = math.log2(math.e) / math.sqrt(QK_NOPE + QK_ROPE)
    return pl.pallas_call(
        functools.partial(_mla_prep_kernel, qscale=qscale),
        grid=(T // tm,),
        in_specs=[pl.BlockSpec((tm, Q_COLS), lambda i: (i, 0)),
                  pl.BlockSpec((tm, GROUP_W), lambda i: (i, COL_CKV)),
                  pl.BlockSpec((tm, LANES), lambda i: (i, 0)),
                  pl.BlockSpec((tm, LANES), lambda i: (i, 0)),
                  pl.BlockSpec((1, KV_RANK), lambda i: (0, 0)),
                  pl.BlockSpec(w_ukv_b.shape, lambda i: (0, 0))],
        out_specs=[pl.BlockSpec((tm, Q_COLS), lambda i: (i, 0)),
                   pl.BlockSpec((tm, Q_COLS), lambda i: (i, 0)),
                   pl.BlockSpec((tm, MLA_HEADS * V_HEAD), lambda i: (i, 0))],
        out_shape=[jax.ShapeDtypeStruct((T, Q_COLS), BF16),
                   jax.ShapeDtypeStruct((T, Q_COLS), BF16),
                   jax.ShapeDtypeStruct((T, MLA_HEADS * V_HEAD), BF16)],
        compiler_params=_cparams(("parallel",)),
        name="mla_prep",
    )(proj, proj, cos4, sin4, kv_g, w_ukv_b)


def _attention_kernel(q_ref, k_ref, v_ref, o_ref, m_ref, l_ref, acc_ref, *, tq):
    i = pl.program_id(2)
    q = q_ref[...]
    m_ref[...] = jnp.full(m_ref.shape, -jnp.inf, F32)
    l_ref[...] = jnp.zeros(l_ref.shape, F32)
    acc_ref[...] = jnp.zeros(acc_ref.shape, F32)

    def block(j, masked):
        r0 = pl.multiple_of(j * tq, tq)
        k = k_ref[pl.ds(r0, tq), :]
        s = lax.dot_general(q, k, (((1,), (1,)), ((), ())), preferred_element_type=F32)
        if masked:
            qc = lax.broadcasted_iota(jnp.int32, s.shape, 0) // CHUNK
            kc = lax.broadcasted_iota(jnp.int32, s.shape, 1) // CHUNK
            s = jnp.where(kc <= qc, s, -jnp.inf)
        m_old = m_ref[...]
        m_new = jnp.maximum(m_old, jnp.max(s, axis=-1, keepdims=True))
        p = jnp.exp2(s - m_new)
        a = jnp.exp2(m_old - m_new)
        l_ref[...] = a * l_ref[...] + jnp.sum(p, axis=-1, keepdims=True)
        acc_ref[...] = a * acc_ref[...] + jnp.dot(p.astype(BF16), v_ref[pl.ds(r0, tq), :], preferred_element_type=F32)
        m_ref[...] = m_new

    def body(j, carry):
        block(j, False)
        return carry

    lax.fori_loop(0, i, body, 0)
    block(i, True)
    o_ref[...] = acc_ref[...] / l_ref[...]


def _attention(q_cat, k_cat, v, B, S):
    T = B * S
    tq = min(TQ_ATT, S)
    nq = S // tq
    return pl.pallas_call(
        functools.partial(_attention_kernel, tq=tq),
        grid=(B, MLA_HEADS, nq),
        in_specs=[pl.BlockSpec((tq, HEAD_W), lambda b, h, i: (b * nq + i, h)),
                  pl.BlockSpec((S, HEAD_W), lambda b, h, i: (b, h)),
                  pl.BlockSpec((S, V_HEAD), lambda b, h, i: (b, h))],
        out_specs=pl.BlockSpec((tq, V_HEAD), lambda b, h, i: (b * nq + i, h)),
        out_shape=jax.ShapeDtypeStruct((T, MLA_HEADS * V_HEAD), F32),
        scratch_shapes=[pltpu.VMEM((tq, 1), F32), pltpu.VMEM((tq, 1), F32), pltpu.VMEM((tq, V_HEAD), F32)],
        compiler_params=_cparams(("parallel", "parallel", "arbitrary")),
        name="attention",
    )(q_cat, k_cat, v)


def _gelu(x):
    return 0.5 * x * (1.0 + lax.erf(x * (1.0 / math.sqrt(2.0))))


def _mixers_kernel(cb_ref, cc_ref, ch_ref, ph_ref, gu_ref, gv_ref, hcc_ref, hch_ref, hph_ref,
                   convw_ref, poolw_ref, lng_ref, lnb_ref, sguw_ref, sgub_ref, gain_ref,
                   yc_ref, yp_ref, ys_ref,
                   g_scr, e_scr, a_scr, b_scr, y_scr, *, tm, tiles_per_seq):
    i = pl.program_id(0)
    t0 = (i % tiles_per_seq) * tm
    keep = jnp.where(t0 == 0, 0.0, 1.0).astype(F32)

    g_scr[0:HALO, :] = hcc_ref[...] * hch_ref[...] * keep
    g_scr[HALO:, :] = cc_ref[...] * ch_ref[...]
    conv = convw_ref[CONV_W - 1:CONV_W, :] * g_scr[HALO:HALO + tm, :]
    for j in range(CONV_W - 1):
        off = HALO - (CONV_W - 1) + j
        conv = conv + convw_ref[j:j + 1, :] * g_scr[off:off + tm, :]
    yc = cb_ref[...] * conv
    yc_ref[...] = (_rms_rows(yc) * gain_ref[:, GROUP_W:2 * GROUP_W]).astype(BF16)

    e_scr[0:HALO, :] = hph_ref[...] * keep
    e_scr[HALO:, :] = ph_ref[...]
    n = tm + HALO
    pos = (t0 + 1 + lax.broadcasted_iota(jnp.int32, (tm, 1), 0)).astype(F32)
    for g, w in enumerate(POOL_WINDOWS):
        cs = slice(g * POOL_CH, (g + 1) * POOL_CH)
        levels = g + 1
        src = None
        bufs = (a_scr, b_scr)
        total = None
        for lev in range(1, levels + 1):
            shift = 1 << (lev - 1)
            lo = 8 * lev if lev < levels else HALO
            if src is None:
                cur = e_scr[lo:n, cs] + e_scr[lo - shift:n - shift, cs]
            else:
                cur = src[lo:n, :] + src[lo - shift:n - shift, :]
            if lev < levels:
                dst = bufs[(lev - 1) % 2]
                dst[lo:n, :] = cur
                src = dst
            else:
                total = cur
        inv = 1.0 / jnp.minimum(pos, float(w))
        pooled = total * inv - e_scr[HALO:, cs]
        y_scr[:, cs] = jnp.dot(pooled.astype(BF16), poolw_ref[g], preferred_element_type=F32)
    yp = y_scr[...]
    yp_ref[...] = (_rms_rows(yp) * gain_ref[:, 2 * GROUP_W:3 * GROUP_W]).astype(BF16)

    u = _gelu(gu_ref[...])
    vn = _layer_norm_rows(_gelu(gv_ref[...]), lng_ref[...], lnb_ref[...]).astype(BF16)
    row = lax.broadcasted_iota(jnp.int32, (SGU_LEN, SGU_LEN), 0)
    col = lax.broadcasted_iota(jnp.int32, (SGU_LEN, SGU_LEN), 1)
    for g in range(SGU_GROUPS):
        cs = slice(g * SGU_CH, (g + 1) * SGU_CH)
        wg = jnp.where(col <= row, sguw_ref[g], 0.0).astype(BF16)
        bias = sgub_ref[:, g:g + 1]
        for c in range(tm // SGU_LEN):
            rs = slice(c * SGU_LEN, (c + 1) * SGU_LEN)
            mixed = jnp.dot(wg, vn[rs, cs], preferred_element_type=F32) + bias
            y_scr[rs, cs] = u[rs, cs] * mixed
    ysg = y_scr[...]
    ys_ref[...] = (_rms_rows(ysg) * gain_ref[:, 3 * GROUP_W:4 * GROUP_W]).astype(BF16)


def _mixers(proj, conv_w, pool_w, ln_g, ln_b, sgu_w, sgu_bt, gain, S):
    T = proj.shape[0]
    tm = min(TM_MIX, S)
    tiles_per_seq = S // tm
    hb = tm // HALO

    def col(c):
        return pl.BlockSpec((tm, GROUP_W), lambda i: (i, c))

    def halo(c):
        return pl.BlockSpec((HALO, GROUP_W), lambda i: (jnp.maximum(i * hb - 1, 0), c))

    def full(a):
        return pl.BlockSpec(a.shape, lambda i: (0,) * a.ndim)

    out = jax.ShapeDtypeStruct((T, GROUP_W), BF16)
    return pl.pallas_call(
        functools.partial(_mixers_kernel, tm=tm, tiles_per_seq=tiles_per_seq),
        grid=(T // tm,),
        in_specs=[col(COL_CB), col(COL_CC), col(COL_CH), col(COL_PH), col(COL_GU), col(COL_GV),
                  halo(COL_CC), halo(COL_CH), halo(COL_PH),
                  full(conv_w), full(pool_w), full(ln_g), full(ln_b), full(sgu_w), full(sgu_bt), full(gain)],
        out_specs=[pl.BlockSpec((tm, GROUP_W), lambda i: (i, 0))] * 3,
        out_shape=[out, out, out],
        scratch_shapes=[pltpu.VMEM((tm + HALO, GROUP_W), F32), pltpu.VMEM((tm + HALO, GROUP_W), F32),
                        pltpu.VMEM((tm + HALO, POOL_CH), F32), pltpu.VMEM((tm + HALO, POOL_CH), F32),
                        pltpu.VMEM((tm, GROUP_W), F32)],
        compiler_params=_cparams(("parallel",)),
        name="mixers",
    )(proj, proj, proj, proj, proj, proj, proj, proj, proj,
      conv_w, pool_w, ln_g, ln_b, sgu_w, sgu_bt, gain)


def _out_proj_kernel(*refs, with_router):
    if with_router:
        (o_ref, yc_ref, yp_ref, ys_ref, gain_ref, wo_ref, x_ref, g_ref, b_ref, rw_ref,
         x1_ref, x1b_ref, lg_ref) = refs
    else:
        (o_ref, yc_ref, yp_ref, ys_ref, gain_ref, wo_ref, x_ref, g_ref, b_ref,
         x1_ref, x1b_ref) = refs
    ym = (_rms_rows(o_ref[...]) * gain_ref[:, 0:GROUP_W]).astype(BF16)
    mix = jnp.dot(ym, wo_ref[0:GROUP_W, :], preferred_element_type=F32)
    for gi, y_ref in enumerate((yc_ref, yp_ref, ys_ref), start=1):
        mix = mix + jnp.dot(y_ref[...], wo_ref[gi * GROUP_W:(gi + 1) * GROUP_W, :], preferred_element_type=F32)
    x1 = _layer_norm_rows(ALPHA * x_ref[...] + mix, g_ref[...], b_ref[...])
    x1_ref[...] = x1
    x1b_ref[...] = x1.astype(BF16)
    if with_router:
        lg_ref[...] = jnp.dot(x1, rw_ref[...], preferred_element_type=F32, precision=lax.Precision.HIGHEST)


def _out_proj(o_mla, yc, yp, ys, gain, wo_b, x, ln_g, ln_b, router_w_pad=None):
    T, D = x.shape
    tm = min(TM_OUT, T)
    with_router = router_w_pad is not None

    def rows(w):
        return pl.BlockSpec((tm, w), lambda i: (i, 0))

    def full(a):
        return pl.BlockSpec(a.shape, lambda i: (0,) * a.ndim)

    in_specs = [rows(GROUP_W)] * 4 + [full(gain), full(wo_b), rows(D), full(ln_g), full(ln_b)]
    args = [o_mla, yc, yp, ys, gain, wo_b, x, ln_g, ln_b]
    out_specs = [rows(D), rows(D)]
    out_shape = [jax.ShapeDtypeStruct((T, D), F32), jax.ShapeDtypeStruct((T, D), BF16)]
    if with_router:
        in_specs.append(full(router_w_pad))
        args.append(router_w_pad)
        out_specs.append(rows(LANES))
        out_shape.append(jax.ShapeDtypeStruct((T, LANES), F32))
    return pl.pallas_call(
        functools.partial(_out_proj_kernel, with_router=with_router),
        grid=(T // tm,),
        in_specs=in_specs,
        out_specs=out_specs,
        out_shape=out_shape,
        compiler_params=_cparams(("parallel",)),
        name="out_proj",
    )(*args)


def _ffn_kernel(be_ref, nv_ref, *refs, fuse_ln):
    if fuse_ln:
        xb_ref, wg_ref, wu_ref, wd_ref, xres_ref, g_ref, b_ref, o_ref, ob_ref, acc_ref = refs
    else:
        xb_ref, wg_ref, wu_ref, wd_ref, o_ref, acc_ref = refs
    i = pl.program_id(0)
    f = pl.program_id(1)
    nf = pl.num_programs(1)
    valid = i < nv_ref[0]

    @pl.when(f == 0)
    def _():
        acc_ref[...] = jnp.zeros(acc_ref.shape, F32)

    @pl.when(valid)
    def _():
        xb = xb_ref[...]
        hg = jnp.dot(xb, wg_ref[...], preferred_element_type=F32)
        hu = jnp.dot(xb, wu_ref[...], preferred_element_type=F32)
        h = (hg * jax.nn.sigmoid(hg) * hu).astype(BF16)
        acc_ref[...] += jnp.dot(h, wd_ref[...], preferred_element_type=F32)

    @pl.when(f == nf - 1)
    def _():
        if fuse_ln:
            x2 = _layer_norm_rows(ALPHA * xres_ref[...] + acc_ref[...], g_ref[...], b_ref[...])
            o_ref[...] = x2
            ob_ref[...] = x2.astype(BF16)
        else:
            o_ref[...] = acc_ref[...]


def _ffn(x_rows_b, wg, wu, wd, block_e, n_valid, ln_args=None):
    R, D = x_rows_b.shape
    F = wg.shape[2]
    tm = min(TM_FFN, R)
    tf = TF_FFN
    nf = F // tf
    fuse_ln = ln_args is not None

    def fidx(i, f, nv):
        return jnp.where(i < nv[0], f, nf - 1)

    in_specs = [pl.BlockSpec((tm, D), lambda i, f, be, nv: (i, 0)),
                pl.BlockSpec((None, D, tf), lambda i, f, be, nv: (be[i], 0, fidx(i, f, nv))),
                pl.BlockSpec((None, D, tf), lambda i, f, be, nv: (be[i], 0, fidx(i, f, nv))),
                pl.BlockSpec((None, tf, D), lambda i, f, be, nv: (be[i], fidx(i, f, nv), 0))]
    args = [x_rows_b, wg, wu, wd]
    out_specs = [pl.BlockSpec((tm, D), lambda i, f, be, nv: (i, 0))]
    out_shape = [jax.ShapeDtypeStruct((R, D), F32)]
    if fuse_ln:
        xres, g, b = ln_args
        in_specs += [pl.BlockSpec((tm, D), lambda i, f, be, nv: (i, 0)),
                     pl.BlockSpec((1, D), lambda i, f, be, nv: (0, 0)),
                     pl.BlockSpec((1, D), lambda i, f, be, nv: (0, 0))]
        args += [xres, g, b]
        out_specs.append(pl.BlockSpec((tm, D), lambda i, f, be, nv: (i, 0)))
        out_shape.append(jax.ShapeDtypeStruct((R, D), BF16))
    return pl.pallas_call(
        functools.partial(_ffn_kernel, fuse_ln=fuse_ln),
        grid_spec=pltpu.PrefetchScalarGridSpec(
            num_scalar_prefetch=2,
            grid=(R // tm, nf),
            in_specs=in_specs,
            out_specs=out_specs,
            scratch_shapes=[pltpu.VMEM((tm, D), F32)]),
        out_shape=out_shape,
        compiler_params=_cparams(("parallel", "arbitrary")),
        name="ffn_dense" if fuse_ln else "ffn_moe",
    )(block_e, n_valid, *args)


def _combine_kernel(x_ref, y0_ref, y1_ref, g0_ref, g1_ref, g_ref, b_ref, o_ref, ob_ref):
    ffn = y0_ref[...] * g0_ref[...] + y1_ref[...] * g1_ref[...]
    x2 = _layer_norm_rows(ALPHA * x_ref[...] + ffn, g_ref[...], b_ref[...])
    o_ref[...] = x2
    ob_ref[...] = x2.astype(BF16)


def _combine(x, y0, y1, g0, g1, ln_g, ln_b):
    T, D = x.shape
    tm = min(TM_OUT, T)
    rows = pl.BlockSpec((tm, D), lambda i: (i, 0))
    gate = pl.BlockSpec((tm, 1), lambda i: (i, 0))
    vec = pl.BlockSpec((1, D), lambda i: (0, 0))
    return pl.pallas_call(
        _combine_kernel,
        grid=(T // tm,),
        in_specs=[rows, rows, rows, gate, gate, vec, vec],
        out_specs=[rows, rows],
        out_shape=[jax.ShapeDtypeStruct((T, D), F32), jax.ShapeDtypeStruct((T, D), BF16)],
        compiler_params=_cparams(("parallel",)),
        name="moe_combine",
    )(x, y0, y1, g0, g1, ln_g, ln_b)


def _pad_w_in(w_in):
    L, D, _ = w_in.shape
    q_dim = MLA_HEADS * (QK_NOPE + QK_ROPE)
    wq = w_in[:, :, :q_dim].reshape(L, D, MLA_HEADS, QK_NOPE + QK_ROPE)
    kr0 = q_dim + KV_RANK
    w_kr = w_in[:, :, kr0:kr0 + QK_ROPE]
    pad = jnp.zeros((L, D, MLA_HEADS, HEAD_W - QK_NOPE - QK_ROPE), w_in.dtype).at[:, :, 0, :].set(w_kr)
    q_part = jnp.concatenate([wq, pad], axis=-1).reshape(L, D, Q_COLS)
    rest = jnp.concatenate([w_in[:, :, q_dim:kr0], w_in[:, :, kr0 + QK_ROPE:]], axis=-1)
    return jnp.concatenate([q_part, rest], axis=-1).astype(BF16)


def _rope_tables(positions):
    inv_freq = ROPE_THETA ** (-jnp.arange(0, QK_ROPE, 2, dtype=F32) / QK_ROPE)
    ang = positions.astype(F32).reshape(-1)[:, None] * inv_freq
    cos, sin = jnp.cos(ang), jnp.sin(ang)
    cos4 = jnp.concatenate([cos, cos, cos, cos], axis=-1)
    sin4 = jnp.concatenate([-sin, sin, -sin, sin], axis=-1)
    return cos4, sin4


def _route(logits, tm):
    T = logits.shape[0]
    A = T * TOP_K
    top_logit, top_e = lax.top_k(logits, TOP_K)
    gates = jax.nn.softmax(top_logit, axis=-1)
    flat_e = top_e.reshape(A)
    onehot = (flat_e[:, None] == jnp.arange(N_EXPERTS, dtype=flat_e.dtype)[None, :]).astype(jnp.int32)
    csum = jnp.cumsum(onehot, axis=0)
    counts = csum[-1]
    rank = jnp.take_along_axis(csum, flat_e[:, None], axis=1)[:, 0] - 1
    padded = (counts + tm - 1) // tm * tm
    pad_ends = jnp.cumsum(padded)
    pad_starts = pad_ends - padded
    dest = (pad_starts[flat_e] + rank).astype(jnp.int32)
    n_blocks = -(-(A + N_EXPERTS * (tm - 1)) // tm)
    n_pad = n_blocks * tm
    tok = (jnp.arange(A, dtype=jnp.int32) // TOP_K)
    slot_tok = jnp.zeros((n_pad,), jnp.int32).at[dest].set(tok)
    block_start = jnp.arange(n_blocks, dtype=jnp.int32) * tm
    block_e = jnp.minimum(jnp.searchsorted(pad_ends, block_start, side='right'), N_EXPERTS - 1).astype(jnp.int32)
    n_valid = (pad_ends[-1] // tm).astype(jnp.int32).reshape(1)
    return dest.reshape(T, TOP_K), gates, slot_tok, block_e, n_valid


def kernel(x, positions, w_in, kv_norm_g, w_ukv, conv_w, pool_w, sgu_ln_g, sgu_ln_b, sgu_w, sgu_b, mix_gain, w_o, ln1_g, ln1_b, ffn_wg, ffn_wu, ffn_wd, router_w, exp_wg, exp_wu, exp_wd, ln2_g, ln2_b):
    B, S, D = x.shape
    T = B * S
    L = w_in.shape[0]
    cos4, sin4 = _rope_tables(positions)
    w_in_b = _pad_w_in(w_in)
    w_ukv_b = w_ukv.astype(BF16)
    w_o_b = w_o.astype(BF16)
    pool_w_b = pool_w.astype(BF16)
    sgu_bt = jnp.swapaxes(sgu_b, 1, 2)
    router_pad = jnp.pad(router_w, ((0, 0), (0, 0), (0, LANES - N_EXPERTS)))

    xf = x.reshape(T, D)
    xb = xf.astype(BF16)
    tm_ffn = min(TM_FFN, T)
    dense_be = jnp.zeros((T // tm_ffn,), jnp.int32)
    dense_nv = jnp.full((1,), T // tm_ffn, jnp.int32)
    for l in range(L):
        proj = _in_proj(xb, w_in_b[l])
        q_cat, k_cat, v = _mla_prep(proj, cos4, sin4, kv_norm_g[l][None, :], w_ukv_b[l])
        o_mla = _attention(q_cat, k_cat, v, B, S)
        yc, yp, ys = _mixers(proj, conv_w[l], pool_w_b[l], sgu_ln_g[l][None, :], sgu_ln_b[l][None, :],
                             sgu_w[l], sgu_bt[l], mix_gain[l][None, :], S)
        ln1 = (ln1_g[l][None, :], ln1_b[l][None, :])
        ln2 = (ln2_g[l][None, :], ln2_b[l][None, :])
        if l % 2 == 0:
            x1, x1b = _out_proj(o_mla, yc, yp, ys, mix_gain[l][None, :], w_o_b[l], xf, *ln1)
            e = l // 2
            xf, xb = _ffn(x1b, ffn_wg[e:e + 1].astype(BF16), ffn_wu[e:e + 1].astype(BF16),
                          ffn_wd[e:e + 1].astype(BF16), dense_be, dense_nv, ln_args=(x1, *ln2))
        else:
            e = l // 2
            x1, x1b, logits = _out_proj(o_mla, yc, yp, ys, mix_gain[l][None, :], w_o_b[l], xf, *ln1,
                                        router_w_pad=router_pad[e])
            dest, gates, slot_tok, block_e, n_valid = _route(logits[:, :N_EXPERTS], tm_ffn)
            x_rows = jnp.take(x1b, slot_tok, axis=0)
            (y_rows,) = _ffn(x_rows, exp_wg[e].astype(BF16), exp_wu[e].astype(BF16), exp_wd[e].astype(BF16),
                             block_e, n_valid)
            y0 = jnp.take(y_rows, dest[:, 0], axis=0)
            y1 = jnp.take(y_rows, dest[:, 1], axis=0)
            xf, xb = _combine(x1, y0, y1, gates[:, 0:1], gates[:, 1:2], *ln2)
    return xf.reshape(B, S, D)
```

```python
import functools
import math

import jax
import jax.numpy as jnp
from jax import lax
from jax.experimental import pallas as pl
from jax.experimental.pallas import tpu as pltpu

F32 = jnp.float32
BF16 = jnp.bfloat16

D_MODEL = 2048
CHUNK = 64
GROUP_W = 512
N_GROUPS = 4
MLA_HEADS = 4
QK_NOPE = 128
QK_ROPE = 64
V_HEAD = 128
KV_RANK = 512
ROPE_THETA = 10000.0
CONV_W = 3
POOL_WINDOWS = (2, 4, 8, 16)
POOL_CH = 128
SGU_LEN = 128
SGU_GROUPS = 4
SGU_CH = 128
D_FF = 5632
N_EXPERTS = 8
TOP_K = 2
DEPTH = 4
ALPHA = (2.0 * DEPTH) ** 0.25
LN_EPS = 1e-5
RMS_EPS = 1e-6

LANES = 128
HEAD_W = 2 * LANES
VMEM_LIMIT = 56 * 1024 * 1024

Q_COLS = MLA_HEADS * HEAD_W
COL_Q, COL_CKV, COL_CB, COL_CC, COL_CH, COL_PH, COL_GU, COL_GV = 0, 2, 3, 4, 5, 6, 7, 8
IN_COLS_PAD = Q_COLS + 7 * GROUP_W

HALO = 32

TM_PROJ, TN_PROJ = 1024, 512
TB_ATT = 512
TM_PREP = TB_ATT
MASK_BIG = 2.0 ** 100
TM_MIX = 512
TM_OUT = 256
TM_FFN, TF_FFN = 512, 512


def _cparams(sem):
    return pltpu.CompilerParams(dimension_semantics=sem, vmem_limit_bytes=VMEM_LIMIT)


def _layer_norm_rows(z, g, b):
    mu = jnp.mean(z, axis=-1, keepdims=True)
    zc = z - mu
    var = jnp.mean(zc * zc, axis=-1, keepdims=True)
    return zc * lax.rsqrt(var + LN_EPS) * g + b


def _rms_rows(y):
    return y * lax.rsqrt(jnp.mean(y * y, axis=-1, keepdims=True) + RMS_EPS)


def _matmul_kernel(x_ref, w_ref, o_ref):
    o_ref[...] = jnp.dot(x_ref[...], w_ref[...], preferred_element_type=F32)


def _in_proj(xb, w_all, l):
    T, K = xb.shape
    N = w_all.shape[2]
    tm = min(TM_PROJ, T)
    return pl.pallas_call(
        _matmul_kernel,
        grid=(T // tm, N // TN_PROJ),
        in_specs=[pl.BlockSpec((tm, K), lambda i, j: (i, 0)),
                  pl.BlockSpec((None, K, TN_PROJ), lambda i, j: (l, 0, j))],
        out_specs=pl.BlockSpec((tm, TN_PROJ), lambda i, j: (i, j)),
        out_shape=jax.ShapeDtypeStruct((T, N), F32),
        compiler_params=_cparams(("parallel", "arbitrary")),
        name="in_proj",
    )(xb, w_all)


def _mla_prep_kernel(q_ref, ckv_ref, cos_ref, sin_ref, g_ref, wukv_ref, qo_ref, ko_ref, vt_ref, *, qscale):
    cos = cos_ref[...]
    sin = sin_ref[...]
    lane = lax.broadcasted_iota(jnp.int32, cos.shape, 1)
    row_chunk = lax.broadcasted_iota(jnp.int32, cos.shape, 0) // CHUNK
    first_half = (lane & (QK_ROPE // 2)) == 0
    low = lane < QK_ROPE

    def rope(r):
        partner = jnp.where(first_half, pltpu.roll(r, LANES - QK_ROPE // 2, axis=1), pltpu.roll(r, QK_ROPE // 2, axis=1))
        return r * cos + partner * sin

    k_rope = None
    for h in range(MLA_HEADS):
        c0 = h * HEAD_W
        qo_ref[:, c0:c0 + LANES] = (q_ref[:, c0:c0 + LANES] * qscale).astype(BF16)
        rr = rope(q_ref[:, c0 + LANES:c0 + HEAD_W])
        qo_ref[:, c0 + LANES:c0 + HEAD_W] = jnp.where(low, rr * qscale, 0.0).astype(BF16)
        if h == 0:
            stair = jnp.where(row_chunk > lane - QK_ROPE, -MASK_BIG, 0.0)
            k_rope = jnp.where(low, pltpu.roll(rr, QK_ROPE, axis=1), stair).astype(BF16)

    c = ckv_ref[...]
    cn = (_rms_rows(c) * g_ref[...]).astype(BF16)
    kv = jnp.dot(cn, wukv_ref[...], preferred_element_type=F32)
    for h in range(MLA_HEADS):
        c0 = h * HEAD_W
        ko_ref[:, c0:c0 + LANES] = kv[:, c0:c0 + LANES].astype(BF16)
        ko_ref[:, c0 + LANES:c0 + HEAD_W] = k_rope
        vt_ref[h] = kv[:, c0 + LANES:c0 + HEAD_W].T.astype(BF16)


def _mla_prep(proj, cos4, sin4, kv_g, w_ukv_all, l):
    T = proj.shape[0]
    tm = min(TM_PREP, T)
    qscale = math.log2(math.e) / math.sqrt(QK_NOPE + QK_ROPE)
    return pl.pallas_call(
        functools.partial(_mla_prep_kernel, qscale=qscale),
        grid=(T // tm,),
        in_specs=[pl.BlockSpec((tm, Q_COLS), lambda i: (i, 0)),
                  pl.BlockSpec((tm, GROUP_W), lambda i: (i, COL_CKV)),
                  pl.BlockSpec((tm, LANES), lambda i: (i, 0)),
                  pl.BlockSpec((tm, LANES), lambda i: (i, 0)),
                  pl.BlockSpec((1, KV_RANK), lambda i: (0, 0)),
                  pl.BlockSpec((None,) + w_ukv_all.shape[1:], lambda i: (l, 0, 0))],
        out_specs=[pl.BlockSpec((tm, Q_COLS), lambda i: (i, 0)),
                   pl.BlockSpec((tm, Q_COLS), lambda i: (i, 0)),
                   pl.BlockSpec((None, MLA_HEADS, V_HEAD, tm), lambda i: (i, 0, 0, 0))],
        out_shape=[jax.ShapeDtypeStruct((T, Q_COLS), BF16),
                   jax.ShapeDtypeStruct((T, Q_COLS), BF16),
                   jax.ShapeDtypeStruct((T // tm, MLA_HEADS, V_HEAD, tm), BF16)],
        compiler_params=_cparams(("parallel",)),
        name="mla_prep",
    )(proj, proj, cos4, sin4, kv_g, w_ukv_all)


def _attention_kernel(q_ref, k_ref, vt_ref, o_ref, q2_ref, s0_ref, s1_ref, m_ref, l_ref, acc_ref, *, tb):
    i = pl.program_id(2)
    q = q_ref[...]
    lane = lax.broadcasted_iota(jnp.int32, q.shape, 1)
    row_chunk = lax.broadcasted_iota(jnp.int32, q.shape, 0) // CHUNK
    q2_ref[0] = q
    q2_ref[1] = jnp.where(lane == QK_NOPE + QK_ROPE + row_chunk, jnp.ones_like(q), q)
    m_ref[...] = jnp.full(m_ref.shape, -jnp.inf, F32)
    l_ref[...] = jnp.zeros(l_ref.shape, F32)
    acc_ref[...] = jnp.zeros(acc_ref.shape, F32)

    def scores(j, s_ref):
        r0 = pl.multiple_of(j * tb, tb)
        qsel = q2_ref[(j == i).astype(jnp.int32)]
        s_ref[...] = lax.dot_general(k_ref[pl.ds(r0, tb), :], qsel, (((1,), (1,)), ((), ())),
                                     preferred_element_type=F32)

    def softmax_pv(j, s_ref):
        s = s_ref[...]
        m_old = m_ref[...]
        m_new = jnp.maximum(m_old, jnp.max(s, axis=0, keepdims=True))
        p = jnp.exp2(s - m_new)
        a = jnp.exp2(m_old - m_new)
        l_ref[...] = a * l_ref[...] + jnp.sum(p, axis=0, keepdims=True)
        acc_ref[...] = a * acc_ref[...] + jnp.dot(vt_ref[j], p.astype(BF16), preferred_element_type=F32)
        m_ref[...] = m_new

    scores(0, s0_ref)

    def pair(jj, carry):
        j = 2 * jj
        scores(j + 1, s1_ref)
        softmax_pv(j, s0_ref)
        scores(j + 2, s0_ref)
        softmax_pv(j + 1, s1_ref)
        return carry

    lax.fori_loop(0, i // 2, pair, 0)

    @pl.when(i % 2 == 1)
    def _():
        scores(i, s1_ref)
        softmax_pv(i - 1, s0_ref)
        softmax_pv(i, s1_ref)

    @pl.when(i % 2 == 0)
    def _():
        softmax_pv(i, s0_ref)

    o_ref[...] = (acc_ref[...] / l_ref[...]).T


def _attention(q_cat, k_cat, v_t, B, S):
    T = B * S
    tb = v_t.shape[-1]
    nq = S // tb
    return pl.pallas_call(
        functools.partial(_attention_kernel, tb=tb),
        grid=(B, MLA_HEADS, nq),
        in_specs=[pl.BlockSpec((tb, HEAD_W), lambda b, h, i: (b * nq + i, h)),
                  pl.BlockSpec((S, HEAD_W), lambda b, h, i: (b, h)),
                  pl.BlockSpec((nq, None, V_HEAD, tb), lambda b, h, i: (b, h, 0, 0))],
        out_specs=pl.BlockSpec((tb, V_HEAD), lambda b, h, i: (b * nq + i, h)),
        out_shape=jax.ShapeDtypeStruct((T, MLA_HEADS * V_HEAD), F32),
        scratch_shapes=[pltpu.VMEM((2, tb, HEAD_W), BF16), pltpu.VMEM((tb, tb), F32), pltpu.VMEM((tb, tb), F32),
                        pltpu.VMEM((1, tb), F32), pltpu.VMEM((1, tb), F32), pltpu.VMEM((V_HEAD, tb), F32)],
        compiler_params=_cparams(("parallel", "parallel", "arbitrary")),
        name="attention",
    )(q_cat, k_cat, v_t)


def _gelu(x):
    return 0.5 * x * (1.0 + lax.erf(x * (1.0 / math.sqrt(2.0))))


def _mixers_kernel(cb_ref, cc_ref, ch_ref, ph_ref, gu_ref, gv_ref, hcc_ref, hch_ref, hph_ref,
                   convw_ref, poolw_ref, lng_ref, lnb_ref, sguw_ref, sgub_ref, gain_ref,
                   yc_ref, yp_ref, ys_ref,
                   g_scr, e_scr, a_scr, b_scr, y_scr, *, tm, tiles_per_seq):
    i = pl.program_id(0)
    t0 = (i % tiles_per_seq) * tm
    keep = jnp.where(t0 == 0, 0.0, 1.0).astype(F32)

    g_scr[0:HALO, :] = hcc_ref[...] * hch_ref[...] * keep
    g_scr[HALO:, :] = cc_ref[...] * ch_ref[...]
    conv = convw_ref[CONV_W - 1:CONV_W, :] * g_scr[HALO:HALO + tm, :]
    for j in range(CONV_W - 1):
        off = HALO - (CONV_W - 1) + j
        conv = conv + convw_ref[j:j + 1, :] * g_scr[off:off + tm, :]
    yc = cb_ref[...] * conv
    yc_ref[...] = (_rms_rows(yc) * gain_ref[:, GROUP_W:2 * GROUP_W]).astype(BF16)

    e_scr[0:HALO, :] = hph_ref[...] * keep
    e_scr[HALO:, :] = ph_ref[...]
    n = tm + HALO
    pos = (t0 + 1 + lax.broadcasted_iota(jnp.int32, (tm, 1), 0)).astype(F32)
    for g, w in enumerate(POOL_WINDOWS):
        cs = slice(g * POOL_CH, (g + 1) * POOL_CH)
        levels = g + 1
        src = None
        bufs = (a_scr, b_scr)
        total = None
        for lev in range(1, levels + 1):
            shift = 1 << (lev - 1)
            lo = 8 * lev if lev < levels else HALO
            if src is None:
                cur = e_scr[lo:n, cs] + e_scr[lo - shift:n - shift, cs]
            else:
                cur = src[lo:n, :] + src[lo - shift:n - shift, :]
            if lev < levels:
                dst = bufs[(lev - 1) % 2]
                dst[lo:n, :] = cur
                src = dst
            else:
                total = cur
        inv = 1.0 / jnp.minimum(pos, float(w))
        pooled = total * inv - e_scr[HALO:, cs]
        y_scr[:, cs] = jnp.dot(pooled.astype(BF16), poolw_ref[g], preferred_element_type=F32)
    yp = y_scr[...]
    yp_ref[...] = (_rms_rows(yp) * gain_ref[:, 2 * GROUP_W:3 * GROUP_W]).astype(BF16)

    u = _gelu(gu_ref[...])
    vn = _layer_norm_rows(_gelu(gv_ref[...]), lng_ref[...], lnb_ref[...]).astype(BF16)
    row = lax.broadcasted_iota(jnp.int32, (SGU_LEN, SGU_LEN), 0)
    col = lax.broadcasted_iota(jnp.int32, (SGU_LEN, SGU_LEN), 1)
    for g in range(SGU_GROUPS):
        cs = slice(g * SGU_CH, (g + 1) * SGU_CH)
        wg = jnp.where(col <= row, sguw_ref[g], 0.0).astype(BF16)
        bias = sgub_ref[:, g:g + 1]
        for c in range(tm // SGU_LEN):
            rs = slice(c * SGU_LEN, (c + 1) * SGU_LEN)
            mixed = jnp.dot(wg, vn[rs, cs], preferred_element_type=F32) + bias
            y_scr[rs, cs] = u[rs, cs] * mixed
    ysg = y_scr[...]
    ys_ref[...] = (_rms_rows(ysg) * gain_ref[:, 3 * GROUP_W:4 * GROUP_W]).astype(BF16)


def _mixers(proj, conv_w, pool_w, ln_g, ln_b, sgu_w, sgu_bt, gain, S):
    T = proj.shape[0]
    tm = min(TM_MIX, S)
    tiles_per_seq = S // tm
    hb = tm // HALO

    def col(c):
        return pl.BlockSpec((tm, GROUP_W), lambda i: (i, c))

    def halo(c):
        return pl.BlockSpec((HALO, GROUP_W), lambda i: (jnp.maximum(i * hb - 1, 0), c))

    def full(a):
        return pl.BlockSpec(a.shape, lambda i: (0,) * a.ndim)

    out = jax.ShapeDtypeStruct((T, GROUP_W), BF16)
    return pl.pallas_call(
        functools.partial(_mixers_kernel, tm=tm, tiles_per_seq=tiles_per_seq),
        grid=(T // tm,),
        in_specs=[col(COL_CB), col(COL_CC), col(COL_CH), col(COL_PH), col(COL_GU), col(COL_GV),
                  halo(COL_CC), halo(COL_CH), halo(COL_PH),
                  full(conv_w), full(pool_w), full(ln_g), full(ln_b), full(sgu_w), full(sgu_bt), full(gain)],
        out_specs=[pl.BlockSpec((tm, GROUP_W), lambda i: (i, 0))] * 3,
        out_shape=[out, out, out],
        scratch_shapes=[pltpu.VMEM((tm + HALO, GROUP_W), F32), pltpu.VMEM((tm + HALO, GROUP_W), F32),
                        pltpu.VMEM((tm + HALO, POOL_CH), F32), pltpu.VMEM((tm + HALO, POOL_CH), F32),
                        pltpu.VMEM((tm, GROUP_W), F32)],
        compiler_params=_cparams(("parallel",)),
        name="mixers",
    )(proj, proj, proj, proj, proj, proj, proj, proj, proj,
      conv_w, pool_w, ln_g, ln_b, sgu_w, sgu_bt, gain)


def _out_proj_kernel(*refs, with_router):
    if with_router:
        (o_ref, yc_ref, yp_ref, ys_ref, gain_ref, wo_ref, x_ref, g_ref, b_ref, rw_ref,
         x1_ref, x1b_ref, lg_ref) = refs
    else:
        (o_ref, yc_ref, yp_ref, ys_ref, gain_ref, wo_ref, x_ref, g_ref, b_ref,
         x1_ref, x1b_ref) = refs
    ym = (_rms_rows(o_ref[...]) * gain_ref[:, 0:GROUP_W]).astype(BF16)
    mix = jnp.dot(ym, wo_ref[0:GROUP_W, :], preferred_element_type=F32)
    for gi, y_ref in enumerate((yc_ref, yp_ref, ys_ref), start=1):
        mix = mix + jnp.dot(y_ref[...], wo_ref[gi * GROUP_W:(gi + 1) * GROUP_W, :], preferred_element_type=F32)
    x1 = _layer_norm_rows(ALPHA * x_ref[...] + mix, g_ref[...], b_ref[...])
    x1_ref[...] = x1
    x1_hi = x1.astype(BF16)
    x1b_ref[...] = x1_hi
    if with_router:
        x1_lo = (x1 - x1_hi.astype(F32)).astype(BF16)
        lg_ref[...] = (jnp.dot(x1_hi, rw_ref[0], preferred_element_type=F32)
                       + (jnp.dot(x1_hi, rw_ref[1], preferred_element_type=F32)
                          + jnp.dot(x1_lo, rw_ref[0], preferred_element_type=F32)))


def _out_proj(o_mla, yc, yp, ys, gain, wo_all, l, x, ln_g, ln_b, router_hl=None):
    T, D = x.shape
    tm = min(TM_OUT, T)
    with_router = router_hl is not None

    def rows(w):
        return pl.BlockSpec((tm, w), lambda i: (i, 0))

    def full(a):
        return pl.BlockSpec(a.shape, lambda i: (0,) * a.ndim)

    wo_spec = pl.BlockSpec((None,) + wo_all.shape[1:], lambda i: (l, 0, 0))
    in_specs = [rows(GROUP_W)] * 4 + [full(gain), wo_spec, rows(D), full(ln_g), full(ln_b)]
    args = [o_mla, yc, yp, ys, gain, wo_all, x, ln_g, ln_b]
    out_specs = [rows(D), rows(D)]
    out_shape = [jax.ShapeDtypeStruct((T, D), F32), jax.ShapeDtypeStruct((T, D), BF16)]
    if with_router:
        in_specs.append(full(router_hl))
        args.append(router_hl)
        out_specs.append(rows(LANES))
        out_shape.append(jax.ShapeDtypeStruct((T, LANES), F32))
    return pl.pallas_call(
        functools.partial(_out_proj_kernel, with_router=with_router),
        grid=(T // tm,),
        in_specs=in_specs,
        out_specs=out_specs,
        out_shape=out_shape,
        compiler_params=_cparams(("parallel",)),
        name="out_proj",
    )(*args)


def _ffn_kernel(be_ref, nv_ref, *refs, fuse_ln):
    if fuse_ln:
        xb_ref, wg_ref, wu_ref, wd_ref, xres_ref, g_ref, b_ref, o_ref, ob_ref, acc_ref = refs
    else:
        xb_ref, wg_ref, wu_ref, wd_ref, o_ref, acc_ref = refs
    i = pl.program_id(0)
    f = pl.program_id(1)
    nf = pl.num_programs(1)
    valid = i < nv_ref[0]

    @pl.when(f == 0)
    def _():
        acc_ref[...] = jnp.zeros(acc_ref.shape, F32)

    @pl.when(valid)
    def _():
        xb = xb_ref[...]
        hg = jnp.dot(xb, wg_ref[...], preferred_element_type=F32)
        hu = jnp.dot(xb, wu_ref[...], preferred_element_type=F32)
        h = (hg * jax.nn.sigmoid(hg) * hu).astype(BF16)
        acc_ref[...] += jnp.dot(h, wd_ref[...], preferred_element_type=F32)

    @pl.when(f == nf - 1)
    def _():
        if fuse_ln:
            x2 = _layer_norm_rows(ALPHA * xres_ref[...] + acc_ref[...], g_ref[...], b_ref[...])
            o_ref[...] = x2
            ob_ref[...] = x2.astype(BF16)
        else:
            o_ref[...] = acc_ref[...]


def _ffn(x_rows_b, wg, wu, wd, block_e, n_valid, ln_args=None):
    R, D = x_rows_b.shape
    F = wg.shape[2]
    tm = min(TM_FFN, R)
    tf = TF_FFN
    nf = F // tf
    fuse_ln = ln_args is not None

    def fidx(i, f, nv):
        return jnp.where(i < nv[0], f, nf - 1)

    in_specs = [pl.BlockSpec((tm, D), lambda i, f, be, nv: (i, 0)),
                pl.BlockSpec((None, D, tf), lambda i, f, be, nv: (be[i], 0, fidx(i, f, nv))),
                pl.BlockSpec((None, D, tf), lambda i, f, be, nv: (be[i], 0, fidx(i, f, nv))),
                pl.BlockSpec((None, tf, D), lambda i, f, be, nv: (be[i], fidx(i, f, nv), 0))]
    args = [x_rows_b, wg, wu, wd]
    out_specs = [pl.BlockSpec((tm, D), lambda i, f, be, nv: (i, 0))]
    out_shape = [jax.ShapeDtypeStruct((R, D), F32)]
    if fuse_ln:
        xres, g, b = ln_args
        in_specs += [pl.BlockSpec((tm, D), lambda i, f, be, nv: (i, 0)),
                     pl.BlockSpec((1, D), lambda i, f, be, nv: (0, 0)),
                     pl.BlockSpec((1, D), lambda i, f, be, nv: (0, 0))]
        args += [xres, g, b]
        out_specs.append(pl.BlockSpec((tm, D), lambda i, f, be, nv: (i, 0)))
        out_shape.append(jax.ShapeDtypeStruct((R, D), BF16))
    return pl.pallas_call(
        functools.partial(_ffn_kernel, fuse_ln=fuse_ln),
        grid_spec=pltpu.PrefetchScalarGridSpec(
            num_scalar_prefetch=2,
            grid=(R // tm, nf),
            in_specs=in_specs,
            out_specs=out_specs,
            scratch_shapes=[pltpu.VMEM((tm, D), F32)]),
        out_shape=out_shape,
        compiler_params=_cparams(("parallel", "arbitrary")),
        name="ffn_dense" if fuse_ln else "ffn_moe",
    )(block_e, n_valid, *args)


def _combine_kernel(x_ref, y0_ref, y1_ref, g0_ref, g1_ref, g_ref, b_ref, o_ref, ob_ref):
    ffn = y0_ref[...] * g0_ref[...] + y1_ref[...] * g1_ref[...]
    x2 = _layer_norm_rows(ALPHA * x_ref[...] + ffn, g_ref[...], b_ref[...])
    o_ref[...] = x2
    ob_ref[...] = x2.astype(BF16)


def _combine(x, y0, y1, g0, g1, ln_g, ln_b):
    T, D = x.shape
    tm = min(TM_OUT, T)
    rows = pl.BlockSpec((tm, D), lambda i: (i, 0))
    gate = pl.BlockSpec((tm, 1), lambda i: (i, 0))
    vec = pl.BlockSpec((1, D), lambda i: (0, 0))
    return pl.pallas_call(
        _combine_kernel,
        grid=(T // tm,),
        in_specs=[rows, rows, rows, gate, gate, vec, vec],
        out_specs=[rows, rows],
        out_shape=[jax.ShapeDtypeStruct((T, D), F32), jax.ShapeDtypeStruct((T, D), BF16)],
        compiler_params=_cparams(("parallel",)),
        name="moe_combine",
    )(x, y0, y1, g0, g1, ln_g, ln_b)


def _pad_w_in(w_in):
    L, D, _ = w_in.shape
    q_dim = MLA_HEADS * (QK_NOPE + QK_ROPE)
    wq = w_in[:, :, :q_dim].reshape(L, D, MLA_HEADS, QK_NOPE + QK_ROPE)
    kr0 = q_dim + KV_RANK
    w_kr = w_in[:, :, kr0:kr0 + QK_ROPE]
    pad = jnp.zeros((L, D, MLA_HEADS, HEAD_W - QK_NOPE - QK_ROPE), w_in.dtype).at[:, :, 0, :].set(w_kr)
    q_part = jnp.concatenate([wq, pad], axis=-1).reshape(L, D, Q_COLS)
    rest = jnp.concatenate([w_in[:, :, q_dim:kr0], w_in[:, :, kr0 + QK_ROPE:]], axis=-1)
    return jnp.concatenate([q_part, rest], axis=-1).astype(BF16)


def _rope_tables(positions):
    inv_freq = ROPE_THETA ** (-jnp.arange(0, QK_ROPE, 2, dtype=F32) / QK_ROPE)
    ang = positions.astype(F32).reshape(-1)[:, None] * inv_freq
    cos, sin = jnp.cos(ang), jnp.sin(ang)
    cos4 = jnp.concatenate([cos, cos, cos, cos], axis=-1)
    sin4 = jnp.concatenate([-sin, sin, -sin, sin], axis=-1)
    return cos4, sin4


def _route(logits, tm):
    T = logits.shape[0]
    A = T * TOP_K
    top_logit, top_e = lax.top_k(logits, TOP_K)
    gates = jax.nn.softmax(top_logit, axis=-1)
    flat_e = top_e.reshape(A)
    onehot = (flat_e[:, None] == jnp.arange(N_EXPERTS, dtype=flat_e.dtype)[None, :]).astype(jnp.int32)
    csum = jnp.cumsum(onehot, axis=0)
    counts = csum[-1]
    rank = jnp.take_along_axis(csum, flat_e[:, None], axis=1)[:, 0] - 1
    padded = (counts + tm - 1) // tm * tm
    pad_ends = jnp.cumsum(padded)
    pad_starts = pad_ends - padded
    dest = (pad_starts[flat_e] + rank).astype(jnp.int32)
    n_blocks = -(-(A + N_EXPERTS * (tm - 1)) // tm)
    n_pad = n_blocks * tm
    tok = (jnp.arange(A, dtype=jnp.int32) // TOP_K)
    slot_tok = jnp.zeros((n_pad,), jnp.int32).at[dest].set(tok)
    block_start = jnp.arange(n_blocks, dtype=jnp.int32) * tm
    block_e = jnp.minimum(jnp.searchsorted(pad_ends, block_start, side='right'), N_EXPERTS - 1).astype(jnp.int32)
    n_valid = (pad_ends[-1] // tm).astype(jnp.int32).reshape(1)
    return dest.reshape(T, TOP_K), gates, slot_tok, block_e, n_valid


def kernel(x, positions, w_in, kv_norm_g, w_ukv, conv_w, pool_w, sgu_ln_g, sgu_ln_b, sgu_w, sgu_b, mix_gain, w_o, ln1_g, ln1_b, ffn_wg, ffn_wu, ffn_wd, router_w, exp_wg, exp_wu, exp_wd, ln2_g, ln2_b):
    B, S, D = x.shape
    T = B * S
    L = w_in.shape[0]
    cos4, sin4 = _rope_tables(positions)
    w_in_b = _pad_w_in(w_in)
    w_ukv_b = w_ukv.astype(BF16)
    w_o_b = w_o.astype(BF16)
    pool_w_b = pool_w.astype(BF16)
    sgu_bt = jnp.swapaxes(sgu_b, 1, 2)
    router_pad = jnp.pad(router_w, ((0, 0), (0, 0), (0, LANES - N_EXPERTS)))
    router_hi = router_pad.astype(BF16)
    router_hl = jnp.stack([router_hi, (router_pad - router_hi.astype(F32)).astype(BF16)], axis=1)
    ffn_w = (ffn_wg.astype(BF16), ffn_wu.astype(BF16), ffn_wd.astype(BF16))
    exp_w = tuple(w.astype(BF16).reshape((-1,) + w.shape[2:]) for w in (exp_wg, exp_wu, exp_wd))

    xf = x.reshape(T, D)
    xb = xf.astype(BF16)
    tm_ffn = min(TM_FFN, T)
    dense_nv = jnp.full((1,), T // tm_ffn, jnp.int32)
    for l in range(L):
        proj = _in_proj(xb, w_in_b, l)
        q_cat, k_cat, v_t = _mla_prep(proj, cos4, sin4, kv_norm_g[l][None, :], w_ukv_b, l)
        o_mla = _attention(q_cat, k_cat, v_t, B, S)
        yc, yp, ys = _mixers(proj, conv_w[l], pool_w_b[l], sgu_ln_g[l][None, :], sgu_ln_b[l][None, :],
                             sgu_w[l], sgu_bt[l], mix_gain[l][None, :], S)
        ln1 = (ln1_g[l][None, :], ln1_b[l][None, :])
        ln2 = (ln2_g[l][None, :], ln2_b[l][None, :])
        e = l // 2
        if l % 2 == 0:
            x1, x1b = _out_proj(o_mla, yc, yp, ys, mix_gain[l][None, :], w_o_b, l, xf, *ln1)
            dense_be = jnp.full((T // tm_ffn,), e, jnp.int32)
            xf, xb = _ffn(x1b, *ffn_w, dense_be, dense_nv, ln_args=(x1, *ln2))
        else:
            x1, x1b, logits = _out_proj(o_mla, yc, yp, ys, mix_gain[l][None, :], w_o_b, l, xf, *ln1,
                                        router_hl=router_hl[e])
            dest, gates, slot_tok, block_e, n_valid = _route(logits[:, :N_EXPERTS], tm_ffn)
            x_rows = x1b.at[slot_tok].get(mode="promise_in_bounds")
            (y_rows,) = _ffn(x_rows, *exp_w, block_e + e * N_EXPERTS, n_valid)
            y0 = y_rows.at[dest[:, 0]].get(mode="promise_in_bounds")
            y1 = y_rows.at[dest[:, 1]].get(mode="promise_in_bounds")
            xf, xb = _combine(x1, y0, y1, gates[:, 0:1], gates[:, 1:2], *ln2)
    return xf.reshape(B, S, D)
```

```python
import functools
import math

import jax
import jax.numpy as jnp
from jax import lax
from jax.experimental import pallas as pl
from jax.experimental.pallas import tpu as pltpu

F32 = jnp.float32
BF16 = jnp.bfloat16

D_MODEL = 2048
CHUNK = 64
GROUP_W = 512
N_GROUPS = 4
MLA_HEADS = 4
QK_NOPE = 128
QK_ROPE = 64
V_HEAD = 128
KV_RANK = 512
ROPE_THETA = 10000.0
CONV_W = 3
POOL_WINDOWS = (2, 4, 8, 16)
POOL_CH = 128
SGU_LEN = 128
SGU_GROUPS = 4
SGU_CH = 128
D_FF = 5632
N_EXPERTS = 8
TOP_K = 2
DEPTH = 4
ALPHA = (2.0 * DEPTH) ** 0.25
LN_EPS = 1e-5
RMS_EPS = 1e-6

LANES = 128
HEAD_W = 2 * LANES
VMEM_LIMIT = 56 * 1024 * 1024

Q_COLS = MLA_HEADS * HEAD_W
COL_Q, COL_CKV, COL_CB, COL_CC, COL_CH, COL_PH, COL_GU, COL_GV = 0, 2, 3, 4, 5, 6, 7, 8
IN_COLS_PAD = Q_COLS + 7 * GROUP_W

HALO = 32

TM_PROJ, TN_PROJ = 1024, 512
TB_ATT = 1024
ATT_HEADS = 2
ATT_BLOCKS_PER_TRIP = 2
TM_PREP = TB_ATT
MASK_BIG = 2.0 ** 100
TM_MIX = 512
TM_OUT = 256
TM_FFN, TF_FFN = 512, 512


def _cparams(sem):
    return pltpu.CompilerParams(dimension_semantics=sem, vmem_limit_bytes=VMEM_LIMIT)


def _layer_norm_rows(z, g, b):
    mu = jnp.mean(z, axis=-1, keepdims=True)
    zc = z - mu
    var = jnp.mean(zc * zc, axis=-1, keepdims=True)
    return zc * lax.rsqrt(var + LN_EPS) * g + b


def _rms_rows(y):
    return y * lax.rsqrt(jnp.mean(y * y, axis=-1, keepdims=True) + RMS_EPS)


def _matmul_kernel(x_ref, w_ref, o_ref):
    o_ref[...] = jnp.dot(x_ref[...], w_ref[...], preferred_element_type=F32).astype(o_ref.dtype)


def _in_proj(xb, w_all, l):
    T, K = xb.shape
    N = w_all.shape[2]
    tm = min(TM_PROJ, T)
    return pl.pallas_call(
        _matmul_kernel,
        grid=(T // tm, N // TN_PROJ),
        in_specs=[pl.BlockSpec((tm, K), lambda i, j: (i, 0)),
                  pl.BlockSpec((None, K, TN_PROJ), lambda i, j: (l, 0, j))],
        out_specs=pl.BlockSpec((tm, TN_PROJ), lambda i, j: (i, j)),
        out_shape=jax.ShapeDtypeStruct((T, N), BF16),
        compiler_params=_cparams(("parallel", "arbitrary")),
        name="in_proj",
    )(xb, w_all)


def _mla_prep_kernel(q_ref, ckv_ref, cos_ref, sin_ref, g_ref, wukv_ref, qo_ref, ko_ref, vt_ref, *, qscale):
    cos = cos_ref[...]
    sin = sin_ref[...]
    lane = lax.broadcasted_iota(jnp.int32, cos.shape, 1)
    row_chunk = lax.broadcasted_iota(jnp.int32, cos.shape, 0) // CHUNK
    first_half = (lane & (QK_ROPE // 2)) == 0
    low = lane < QK_ROPE

    def rope(r):
        partner = jnp.where(first_half, pltpu.roll(r, LANES - QK_ROPE // 2, axis=1), pltpu.roll(r, QK_ROPE // 2, axis=1))
        return r * cos + partner * sin

    k_rope = None
    for h in range(MLA_HEADS):
        c0 = h * HEAD_W
        qo_ref[:, c0:c0 + LANES] = (q_ref[:, c0:c0 + LANES].astype(F32) * qscale).astype(BF16)
        rr = rope(q_ref[:, c0 + LANES:c0 + HEAD_W].astype(F32))
        qo_ref[:, c0 + LANES:c0 + HEAD_W] = jnp.where(low, rr * qscale, 0.0).astype(BF16)
        if h == 0:
            stair = jnp.where(row_chunk > lane - QK_ROPE, -MASK_BIG, 0.0)
            k_rope = jnp.where(low, pltpu.roll(rr, QK_ROPE, axis=1), stair).astype(BF16)

    c = ckv_ref[...].astype(F32)
    cn = (_rms_rows(c) * g_ref[...]).astype(BF16)
    kv = jnp.dot(cn, wukv_ref[...], preferred_element_type=F32)
    for h in range(MLA_HEADS):
        c0 = h * HEAD_W
        ko_ref[:, c0:c0 + LANES] = kv[:, c0:c0 + LANES].astype(BF16)
        ko_ref[:, c0 + LANES:c0 + HEAD_W] = k_rope
        vt_ref[h] = kv[:, c0 + LANES:c0 + HEAD_W].T.astype(BF16)


def _mla_prep(proj, cos4, sin4, kv_g, w_ukv_all, l):
    T = proj.shape[0]
    tm = min(TM_PREP, T)
    qscale = math.log2(math.e) / math.sqrt(QK_NOPE + QK_ROPE)
    return pl.pallas_call(
        functools.partial(_mla_prep_kernel, qscale=qscale),
        grid=(T // tm,),
        in_specs=[pl.BlockSpec((tm, Q_COLS), lambda i: (i, 0)),
                  pl.BlockSpec((tm, GROUP_W), lambda i: (i, COL_CKV)),
                  pl.BlockSpec((tm, LANES), lambda i: (i, 0)),
                  pl.BlockSpec((tm, LANES), lambda i: (i, 0)),
                  pl.BlockSpec((1, KV_RANK), lambda i: (0, 0)),
                  pl.BlockSpec((None,) + w_ukv_all.shape[1:], lambda i: (l, 0, 0))],
        out_specs=[pl.BlockSpec((tm, Q_COLS), lambda i: (i, 0)),
                   pl.BlockSpec((tm, Q_COLS), lambda i: (i, 0)),
                   pl.BlockSpec((None, MLA_HEADS, V_HEAD, tm), lambda i: (i, 0, 0, 0))],
        out_shape=[jax.ShapeDtypeStruct((T, Q_COLS), BF16),
                   jax.ShapeDtypeStruct((T, Q_COLS), BF16),
                   jax.ShapeDtypeStruct((T // tm, MLA_HEADS, V_HEAD, tm), BF16)],
        compiler_params=_cparams(("parallel",)),
        name="mla_prep",
    )(proj, proj, cos4, sin4, kv_g, w_ukv_all)


def _attention_kernel(q_ref, k_ref, vt_ref, o_ref, q2_ref, s0_ref, s1_ref, m_ref, l_ref, acc_ref, *, tb):
    i = pl.program_id(2)
    hp = ATT_HEADS
    q = q_ref[...]
    lane = lax.broadcasted_iota(jnp.int32, q.shape, 1) % HEAD_W
    row_chunk = lax.broadcasted_iota(jnp.int32, q.shape, 0) // CHUNK
    q2_ref[0] = q
    q2_ref[1] = jnp.where(lane == QK_NOPE + QK_ROPE + row_chunk, jnp.ones_like(q), q)
    m_ref[...] = jnp.full(m_ref.shape, -jnp.inf, F32)
    l_ref[...] = jnp.zeros(l_ref.shape, F32)
    acc_ref[...] = jnp.zeros(acc_ref.shape, F32)

    def scores(j, s_ref):
        r0 = pl.multiple_of(j * tb, tb)
        sel = (j == i).astype(jnp.int32)
        for h in range(hp):
            cs = slice(h * HEAD_W, (h + 1) * HEAD_W)
            s_ref[h] = lax.dot_general(k_ref[pl.ds(r0, tb), cs], q2_ref[sel, :, cs], (((1,), (1,)), ((), ())),
                                       preferred_element_type=F32)

    def softmax_pv(j, s_ref):
        for h in range(hp):
            s = s_ref[h]
            m_old = m_ref[h]
            m_new = jnp.maximum(m_old, jnp.max(s, axis=0, keepdims=True))
            p = jnp.exp2(s - m_new)
            a = jnp.exp2(m_old - m_new)
            l_ref[h] = a * l_ref[h] + jnp.sum(p, axis=0, keepdims=True)
            acc_ref[h] = a * acc_ref[h] + jnp.dot(vt_ref[j, h], p.astype(BF16), preferred_element_type=F32)
            m_ref[h] = m_new

    scores(0, s0_ref)
    bufs = (s0_ref, s1_ref)
    U = ATT_BLOCKS_PER_TRIP

    def trip(g, carry):
        j = U * g
        for u in range(U):
            scores(j + u + 1, bufs[(u + 1) % 2])
            softmax_pv(j + u, bufs[u % 2])
        return carry

    trips = i // U
    lax.fori_loop(0, trips, trip, 0)
    base = U * trips
    for u in range(U):
        j = base + u

        @pl.when(j < i)
        def _():
            scores(j + 1, bufs[(u + 1) % 2])
            softmax_pv(j, bufs[u % 2])

        @pl.when(j == i)
        def _():
            softmax_pv(j, bufs[u % 2])

    for h in range(hp):
        o_ref[:, h * V_HEAD:(h + 1) * V_HEAD] = (acc_ref[h] / l_ref[h]).T


def _attention(q_cat, k_cat, v_t, B, S):
    T = B * S
    tb = v_t.shape[-1]
    nq = S // tb
    hp = ATT_HEADS
    resident = dict(pipeline_mode=pl.Buffered(1))
    return pl.pallas_call(
        functools.partial(_attention_kernel, tb=tb),
        grid=(B, MLA_HEADS // hp, nq),
        in_specs=[pl.BlockSpec((tb, hp * HEAD_W), lambda b, h, i: (b * nq + i, h)),
                  pl.BlockSpec((S, hp * HEAD_W), lambda b, h, i: (b, h), **resident),
                  pl.BlockSpec((nq, hp, V_HEAD, tb), lambda b, h, i: (b, h, 0, 0), **resident)],
        out_specs=pl.BlockSpec((tb, hp * V_HEAD), lambda b, h, i: (b * nq + i, h)),
        out_shape=jax.ShapeDtypeStruct((T, MLA_HEADS * V_HEAD), F32),
        scratch_shapes=[pltpu.VMEM((2, tb, hp * HEAD_W), BF16),
                        pltpu.VMEM((hp, tb, tb), F32), pltpu.VMEM((hp, tb, tb), F32),
                        pltpu.VMEM((hp, 1, tb), F32), pltpu.VMEM((hp, 1, tb), F32),
                        pltpu.VMEM((hp, V_HEAD, tb), F32)],
        compiler_params=_cparams(("parallel", "parallel", "arbitrary")),
        name="attention",
    )(q_cat, k_cat, v_t)


def _gelu(x):
    return 0.5 * x * (1.0 + lax.erf(x * (1.0 / math.sqrt(2.0))))


def _mixers_kernel(cb_ref, cc_ref, ch_ref, ph_ref, gu_ref, gv_ref, hcc_ref, hch_ref, hph_ref,
                   convw_ref, poolw_ref, lng_ref, lnb_ref, sguw_ref, sgub_ref, gain_ref,
                   yc_ref, yp_ref, ys_ref,
                   g_scr, e_scr, a_scr, b_scr, y_scr, *, tm, tiles_per_seq):
    i = pl.program_id(0)
    t0 = (i % tiles_per_seq) * tm
    keep = jnp.where(t0 == 0, 0.0, 1.0).astype(F32)

    g_scr[0:HALO, :] = hcc_ref[...].astype(F32) * hch_ref[...].astype(F32) * keep
    g_scr[HALO:, :] = cc_ref[...].astype(F32) * ch_ref[...].astype(F32)
    conv = convw_ref[CONV_W - 1:CONV_W, :] * g_scr[HALO:HALO + tm, :]
    for j in range(CONV_W - 1):
        off = HALO - (CONV_W - 1) + j
        conv = conv + convw_ref[j:j + 1, :] * g_scr[off:off + tm, :]
    yc = cb_ref[...].astype(F32) * conv
    yc_ref[...] = (_rms_rows(yc) * gain_ref[:, GROUP_W:2 * GROUP_W]).astype(BF16)

    e_scr[0:HALO, :] = hph_ref[...].astype(F32) * keep
    e_scr[HALO:, :] = ph_ref[...].astype(F32)
    n = tm + HALO
    pos = (t0 + 1 + lax.broadcasted_iota(jnp.int32, (tm, 1), 0)).astype(F32)
    for g, w in enumerate(POOL_WINDOWS):
        cs = slice(g * POOL_CH, (g + 1) * POOL_CH)
        levels = g + 1
        src = None
        bufs = (a_scr, b_scr)
        total = None
        for lev in range(1, levels + 1):
            shift = 1 << (lev - 1)
            lo = 8 * lev if lev < levels else HALO
            if src is None:
                cur = e_scr[lo:n, cs] + e_scr[lo - shift:n - shift, cs]
            else:
                cur = src[lo:n, :] + src[lo - shift:n - shift, :]
            if lev < levels:
                dst = bufs[(lev - 1) % 2]
                dst[lo:n, :] = cur
                src = dst
            else:
                total = cur
        inv = 1.0 / jnp.minimum(pos, float(w))
        pooled = total * inv - e_scr[HALO:, cs]
        y_scr[:, cs] = jnp.dot(pooled.astype(BF16), poolw_ref[g], preferred_element_type=F32)
    yp = y_scr[...]
    yp_ref[...] = (_rms_rows(yp) * gain_ref[:, 2 * GROUP_W:3 * GROUP_W]).astype(BF16)

    u = _gelu(gu_ref[...].astype(F32))
    vn = _layer_norm_rows(_gelu(gv_ref[...].astype(F32)), lng_ref[...], lnb_ref[...]).astype(BF16)
    row = lax.broadcasted_iota(jnp.int32, (SGU_LEN, SGU_LEN), 0)
    col = lax.broadcasted_iota(jnp.int32, (SGU_LEN, SGU_LEN), 1)
    for g in range(SGU_GROUPS):
        cs = slice(g * SGU_CH, (g + 1) * SGU_CH)
        wg = jnp.where(col <= row, sguw_ref[g], 0.0).astype(BF16)
        bias = sgub_ref[:, g:g + 1]
        for c in range(tm // SGU_LEN):
            rs = slice(c * SGU_LEN, (c + 1) * SGU_LEN)
            mixed = jnp.dot(wg, vn[rs, cs], preferred_element_type=F32) + bias
            y_scr[rs, cs] = u[rs, cs] * mixed
    ysg = y_scr[...]
    ys_ref[...] = (_rms_rows(ysg) * gain_ref[:, 3 * GROUP_W:4 * GROUP_W]).astype(BF16)


def _mixers(proj, conv_w, pool_w, ln_g, ln_b, sgu_w, sgu_bt, gain, S):
    T = proj.shape[0]
    tm = min(TM_MIX, S)
    tiles_per_seq = S // tm
    hb = tm // HALO

    def col(c):
        return pl.BlockSpec((tm, GROUP_W), lambda i: (i, c))

    def halo(c):
        return pl.BlockSpec((HALO, GROUP_W), lambda i: (jnp.maximum(i * hb - 1, 0), c))

    def full(a):
        return pl.BlockSpec(a.shape, lambda i: (0,) * a.ndim)

    out = jax.ShapeDtypeStruct((T, GROUP_W), BF16)
    return pl.pallas_call(
        functools.partial(_mixers_kernel, tm=tm, tiles_per_seq=tiles_per_seq),
        grid=(T // tm,),
        in_specs=[col(COL_CB), col(COL_CC), col(COL_CH), col(COL_PH), col(COL_GU), col(COL_GV),
                  halo(COL_CC), halo(COL_CH), halo(COL_PH),
                  full(conv_w), full(pool_w), full(ln_g), full(ln_b), full(sgu_w), full(sgu_bt), full(gain)],
        out_specs=[pl.BlockSpec((tm, GROUP_W), lambda i: (i, 0))] * 3,
        out_shape=[out, out, out],
        scratch_shapes=[pltpu.VMEM((tm + HALO, GROUP_W), F32), pltpu.VMEM((tm + HALO, GROUP_W), F32),
                        pltpu.VMEM((tm + HALO, POOL_CH), F32), pltpu.VMEM((tm + HALO, POOL_CH), F32),
                        pltpu.VMEM((tm, GROUP_W), F32)],
        compiler_params=_cparams(("parallel",)),
        name="mixers",
    )(proj, proj, proj, proj, proj, proj, proj, proj, proj,
      conv_w, pool_w, ln_g, ln_b, sgu_w, sgu_bt, gain)


def _out_proj_kernel(*refs, with_router):
    if with_router:
        (o_ref, yc_ref, yp_ref, ys_ref, gain_ref, wo_ref, x_ref, g_ref, b_ref, rw_ref,
         x1_ref, x1b_ref, lg_ref) = refs
    else:
        (o_ref, yc_ref, yp_ref, ys_ref, gain_ref, wo_ref, x_ref, g_ref, b_ref,
         x1_ref, x1b_ref) = refs
    ym = (_rms_rows(o_ref[...]) * gain_ref[:, 0:GROUP_W]).astype(BF16)
    mix = jnp.dot(ym, wo_ref[0:GROUP_W, :], preferred_element_type=F32)
    for gi, y_ref in enumerate((yc_ref, yp_ref, ys_ref), start=1):
        mix = mix + jnp.dot(y_ref[...], wo_ref[gi * GROUP_W:(gi + 1) * GROUP_W, :], preferred_element_type=F32)
    x1 = _layer_norm_rows(ALPHA * x_ref[...] + mix, g_ref[...], b_ref[...])
    x1_ref[...] = x1
    x1_hi = x1.astype(BF16)
    x1b_ref[...] = x1_hi
    if with_router:
        x1_lo = (x1 - x1_hi.astype(F32)).astype(BF16)
        lg_ref[...] = (jnp.dot(x1_hi, rw_ref[0], preferred_element_type=F32)
                       + (jnp.dot(x1_hi, rw_ref[1], preferred_element_type=F32)
                          + jnp.dot(x1_lo, rw_ref[0], preferred_element_type=F32)))


def _out_proj(o_mla, yc, yp, ys, gain, wo_all, l, x, ln_g, ln_b, router_hl=None):
    T, D = x.shape
    tm = min(TM_OUT, T)
    with_router = router_hl is not None

    def rows(w):
        return pl.BlockSpec((tm, w), lambda i: (i, 0))

    def full(a):
        return pl.BlockSpec(a.shape, lambda i: (0,) * a.ndim)

    wo_spec = pl.BlockSpec((None,) + wo_all.shape[1:], lambda i: (l, 0, 0))
    in_specs = [rows(GROUP_W)] * 4 + [full(gain), wo_spec, rows(D), full(ln_g), full(ln_b)]
    args = [o_mla, yc, yp, ys, gain, wo_all, x, ln_g, ln_b]
    out_specs = [rows(D), rows(D)]
    out_shape = [jax.ShapeDtypeStruct((T, D), F32), jax.ShapeDtypeStruct((T, D), BF16)]
    if with_router:
        in_specs.append(full(router_hl))
        args.append(router_hl)
        out_specs.append(rows(LANES))
        out_shape.append(jax.ShapeDtypeStruct((T, LANES), F32))
    return pl.pallas_call(
        functools.partial(_out_proj_kernel, with_router=with_router),
        grid=(T // tm,),
        in_specs=in_specs,
        out_specs=out_specs,
        out_shape=out_shape,
        compiler_params=_cparams(("parallel",)),
        name="out_proj",
    )(*args)


def _ffn_kernel(be_ref, nv_ref, *refs, fuse_ln):
    if fuse_ln:
        xb_ref, wg_ref, wu_ref, wd_ref, xres_ref, g_ref, b_ref, o_ref, ob_ref, acc_ref = refs
    else:
        xb_ref, wg_ref, wu_ref, wd_ref, o_ref, acc_ref = refs
    i = pl.program_id(0)
    f = pl.program_id(1)
    nf = pl.num_programs(1)
    valid = i < nv_ref[0]

    @pl.when(f == 0)
    def _():
        acc_ref[...] = jnp.zeros(acc_ref.shape, F32)

    @pl.when(valid)
    def _():
        xb = xb_ref[...]
        hg = jnp.dot(xb, wg_ref[...], preferred_element_type=F32)
        hu = jnp.dot(xb, wu_ref[...], preferred_element_type=F32)
        h = (hg * jax.nn.sigmoid(hg) * hu).astype(BF16)
        acc_ref[...] += jnp.dot(h, wd_ref[...], preferred_element_type=F32)

    @pl.when(f == nf - 1)
    def _():
        if fuse_ln:
            x2 = _layer_norm_rows(ALPHA * xres_ref[...] + acc_ref[...], g_ref[...], b_ref[...])
            o_ref[...] = x2
            ob_ref[...] = x2.astype(BF16)
        else:
            o_ref[...] = acc_ref[...].astype(o_ref.dtype)


def _ffn(x_rows_b, wg, wu, wd, block_e, n_valid, ln_args=None):
    R, D = x_rows_b.shape
    F = wg.shape[2]
    tm = min(TM_FFN, R)
    tf = TF_FFN
    nf = F // tf
    fuse_ln = ln_args is not None

    def fidx(i, f, nv):
        return jnp.where(i < nv[0], f, nf - 1)

    in_specs = [pl.BlockSpec((tm, D), lambda i, f, be, nv: (i, 0)),
                pl.BlockSpec((None, D, tf), lambda i, f, be, nv: (be[i], 0, fidx(i, f, nv))),
                pl.BlockSpec((None, D, tf), lambda i, f, be, nv: (be[i], 0, fidx(i, f, nv))),
                pl.BlockSpec((None, tf, D), lambda i, f, be, nv: (be[i], fidx(i, f, nv), 0))]
    args = [x_rows_b, wg, wu, wd]
    out_specs = [pl.BlockSpec((tm, D), lambda i, f, be, nv: (i, 0))]
    out_shape = [jax.ShapeDtypeStruct((R, D), F32 if fuse_ln else BF16)]
    if fuse_ln:
        xres, g, b = ln_args
        in_specs += [pl.BlockSpec((tm, D), lambda i, f, be, nv: (i, 0)),
                     pl.BlockSpec((1, D), lambda i, f, be, nv: (0, 0)),
                     pl.BlockSpec((1, D), lambda i, f, be, nv: (0, 0))]
        args += [xres, g, b]
        out_specs.append(pl.BlockSpec((tm, D), lambda i, f, be, nv: (i, 0)))
        out_shape.append(jax.ShapeDtypeStruct((R, D), BF16))
    return pl.pallas_call(
        functools.partial(_ffn_kernel, fuse_ln=fuse_ln),
        grid_spec=pltpu.PrefetchScalarGridSpec(
            num_scalar_prefetch=2,
            grid=(R // tm, nf),
            in_specs=in_specs,
            out_specs=out_specs,
            scratch_shapes=[pltpu.VMEM((tm, D), F32)]),
        out_shape=out_shape,
        compiler_params=_cparams(("parallel", "arbitrary")),
        name="ffn_dense" if fuse_ln else "ffn_moe",
    )(block_e, n_valid, *args)


def _combine_kernel(x_ref, y0_ref, y1_ref, g0_ref, g1_ref, g_ref, b_ref, o_ref, ob_ref):
    ffn = y0_ref[...].astype(F32) * g0_ref[...] + y1_ref[...].astype(F32) * g1_ref[...]
    x2 = _layer_norm_rows(ALPHA * x_ref[...] + ffn, g_ref[...], b_ref[...])
    o_ref[...] = x2
    ob_ref[...] = x2.astype(BF16)


def _combine(x, y0, y1, g0, g1, ln_g, ln_b):
    T, D = x.shape
    tm = min(TM_OUT, T)
    rows = pl.BlockSpec((tm, D), lambda i: (i, 0))
    gate = pl.BlockSpec((tm, 1), lambda i: (i, 0))
    vec = pl.BlockSpec((1, D), lambda i: (0, 0))
    return pl.pallas_call(
        _combine_kernel,
        grid=(T // tm,),
        in_specs=[rows, rows, rows, gate, gate, vec, vec],
        out_specs=[rows, rows],
        out_shape=[jax.ShapeDtypeStruct((T, D), F32), jax.ShapeDtypeStruct((T, D), BF16)],
        compiler_params=_cparams(("parallel",)),
        name="moe_combine",
    )(x, y0, y1, g0, g1, ln_g, ln_b)


def _pad_w_in(w_in):
    L, D, _ = w_in.shape
    q_dim = MLA_HEADS * (QK_NOPE + QK_ROPE)
    wq = w_in[:, :, :q_dim].reshape(L, D, MLA_HEADS, QK_NOPE + QK_ROPE)
    kr0 = q_dim + KV_RANK
    w_kr = w_in[:, :, kr0:kr0 + QK_ROPE]
    pad = jnp.zeros((L, D, MLA_HEADS, HEAD_W - QK_NOPE - QK_ROPE), w_in.dtype).at[:, :, 0, :].set(w_kr)
    q_part = jnp.concatenate([wq, pad], axis=-1).reshape(L, D, Q_COLS)
    rest = jnp.concatenate([w_in[:, :, q_dim:kr0], w_in[:, :, kr0 + QK_ROPE:]], axis=-1)
    return jnp.concatenate([q_part, rest], axis=-1).astype(BF16)


def _rope_tables(positions):
    inv_freq = ROPE_THETA ** (-jnp.arange(0, QK_ROPE, 2, dtype=F32) / QK_ROPE)
    ang = positions.astype(F32).reshape(-1)[:, None] * inv_freq
    cos, sin = jnp.cos(ang), jnp.sin(ang)
    cos4 = jnp.concatenate([cos, cos, cos, cos], axis=-1)
    sin4 = jnp.concatenate([-sin, sin, -sin, sin], axis=-1)
    return cos4, sin4


def _route(logits, tm):
    T = logits.shape[0]
    A = T * TOP_K
    top_logit, top_e = lax.top_k(logits, TOP_K)
    gates = jax.nn.softmax(top_logit, axis=-1)
    flat_e = top_e.reshape(A)
    onehot = (flat_e[:, None] == jnp.arange(N_EXPERTS, dtype=flat_e.dtype)[None, :]).astype(jnp.int32)
    csum = jnp.cumsum(onehot, axis=0)
    counts = csum[-1]
    rank = jnp.take_along_axis(csum, flat_e[:, None], axis=1)[:, 0] - 1
    padded = (counts + tm - 1) // tm * tm
    pad_ends = jnp.cumsum(padded)
    pad_starts = pad_ends - padded
    dest = (pad_starts[flat_e] + rank).astype(jnp.int32)
    n_blocks = -(-(A + N_EXPERTS * (tm - 1)) // tm)
    n_pad = n_blocks * tm
    tok = (jnp.arange(A, dtype=jnp.int32) // TOP_K)
    slot_tok = jnp.zeros((n_pad,), jnp.int32).at[dest].set(tok)
    block_start = jnp.arange(n_blocks, dtype=jnp.int32) * tm
    block_e = jnp.minimum(jnp.searchsorted(pad_ends, block_start, side='right'), N_EXPERTS - 1).astype(jnp.int32)
    n_valid = (pad_ends[-1] // tm).astype(jnp.int32).reshape(1)
    return dest.reshape(T, TOP_K), gates, slot_tok, block_e, n_valid


def kernel(x, positions, w_in, kv_norm_g, w_ukv, conv_w, pool_w, sgu_ln_g, sgu_ln_b, sgu_w, sgu_b, mix_gain, w_o, ln1_g, ln1_b, ffn_wg, ffn_wu, ffn_wd, router_w, exp_wg, exp_wu, exp_wd, ln2_g, ln2_b):
    B, S, D = x.shape
    T = B * S
    L = w_in.shape[0]
    cos4, sin4 = _rope_tables(positions)
    w_in_b = _pad_w_in(w_in)
    w_ukv_b = w_ukv.astype(BF16)
    w_o_b = w_o.astype(BF16)
    pool_w_b = pool_w.astype(BF16)
    sgu_bt = jnp.swapaxes(sgu_b, 1, 2)
    router_pad = jnp.pad(router_w, ((0, 0), (0, 0), (0, LANES - N_EXPERTS)))
    router_hi = router_pad.astype(BF16)
    router_hl = jnp.stack([router_hi, (router_pad - router_hi.astype(F32)).astype(BF16)], axis=1)
    ffn_w = (ffn_wg.astype(BF16), ffn_wu.astype(BF16), ffn_wd.astype(BF16))
    exp_w = tuple(w.astype(BF16).reshape((-1,) + w.shape[2:]) for w in (exp_wg, exp_wu, exp_wd))

    xf = x.reshape(T, D)
    xb = xf.astype(BF16)
    tm_ffn = min(TM_FFN, T)
    dense_nv = jnp.full((1,), T // tm_ffn, jnp.int32)
    for l in range(L):
        proj = _in_proj(xb, w_in_b, l)
        q_cat, k_cat, v_t = _mla_prep(proj, cos4, sin4, kv_norm_g[l][None, :], w_ukv_b, l)
        o_mla = _attention(q_cat, k_cat, v_t, B, S)
        yc, yp, ys = _mixers(proj, conv_w[l], pool_w_b[l], sgu_ln_g[l][None, :], sgu_ln_b[l][None, :],
                             sgu_w[l], sgu_bt[l], mix_gain[l][None, :], S)
        ln1 = (ln1_g[l][None, :], ln1_b[l][None, :])
        ln2 = (ln2_g[l][None, :], ln2_b[l][None, :])
        e = l // 2
        if l % 2 == 0:
            x1, x1b = _out_proj(o_mla, yc, yp, ys, mix_gain[l][None, :], w_o_b, l, xf, *ln1)
            dense_be = jnp.full((T // tm_ffn,), e, jnp.int32)
            xf, xb = _ffn(x1b, *ffn_w, dense_be, dense_nv, ln_args=(x1, *ln2))
        else:
            x1, x1b, logits = _out_proj(o_mla, yc, yp, ys, mix_gain[l][None, :], w_o_b, l, xf, *ln1,
                                        router_hl=router_hl[e])
            dest, gates, slot_tok, block_e, n_valid = _route(logits[:, :N_EXPERTS], tm_ffn)
            x_rows = x1b.at[slot_tok].get(mode="promise_in_bounds")
            (y_rows,) = _ffn(x_rows, *exp_w, block_e + e * N_EXPERTS, n_valid)
            y0 = y_rows.at[dest[:, 0]].get(mode="promise_in_bounds")
            y1 = y_rows.at[dest[:, 1]].get(mode="promise_in_bounds")
            xf, xb = _combine(x1, y0, y1, gates[:, 0:1], gates[:, 1:2], *ln2)
    return xf.reshape(B, S, D)
```

```python
import functools
import math

import jax
import jax.numpy as jnp
from jax import lax
from jax.experimental import pallas as pl
from jax.experimental.pallas import tpu as pltpu

F32 = jnp.float32
BF16 = jnp.bfloat16

D_MODEL = 2048
CHUNK = 64
GROUP_W = 512
N_GROUPS = 4
MLA_HEADS = 4
QK_NOPE = 128
QK_ROPE = 64
V_HEAD = 128
KV_RANK = 512
ROPE_THETA = 10000.0
CONV_W = 3
POOL_WINDOWS = (2, 4, 8, 16)
POOL_CH = 128
SGU_LEN = 128
SGU_GROUPS = 4
SGU_CH = 128
D_FF = 5632
N_EXPERTS = 8
TOP_K = 2
DEPTH = 4
ALPHA = (2.0 * DEPTH) ** 0.25
LN_EPS = 1e-5
RMS_EPS = 1e-6

LANES = 128
HEAD_W = 2 * LANES
VMEM_LIMIT = 56 * 1024 * 1024

Q_COLS = MLA_HEADS * HEAD_W
COL_Q, COL_CKV, COL_CB, COL_CC, COL_CH, COL_PH, COL_GU, COL_GV = 0, 2, 3, 4, 5, 6, 7, 8
IN_COLS_PAD = Q_COLS + 7 * GROUP_W

HALO = 32

TM_PROJ, TN_PROJ = 1024, 2304
TB_ATT = 1024
ATT_HEADS = 2
ATT_BLOCKS_PER_TRIP = 2
TM_PREP = TB_ATT
MASK_BIG = 2.0 ** 100
TM_MIX = 512
TM_OUT = 256
TF_FFN = 512
TM_MOE = 1024
TM_UP = 1024
TM_DOWN, TK_DOWN = 512, 1408


def _cparams(sem):
    return pltpu.CompilerParams(dimension_semantics=sem, vmem_limit_bytes=VMEM_LIMIT)


def _layer_norm_rows(z, g, b):
    mu = jnp.mean(z, axis=-1, keepdims=True)
    zc = z - mu
    var = jnp.mean(zc * zc, axis=-1, keepdims=True)
    return zc * lax.rsqrt(var + LN_EPS) * g + b


def _rms_rows(y):
    return y * lax.rsqrt(jnp.mean(y * y, axis=-1, keepdims=True) + RMS_EPS)


def _matmul_kernel(x_ref, w_ref, o_ref):
    o_ref[...] = jnp.dot(x_ref[...], w_ref[...], preferred_element_type=F32).astype(o_ref.dtype)


def _in_proj(xb, w_all, l):
    T, K = xb.shape
    N = w_all.shape[2]
    tm = min(TM_PROJ, T)
    return pl.pallas_call(
        _matmul_kernel,
        grid=(T // tm, N // TN_PROJ),
        in_specs=[pl.BlockSpec((tm, K), lambda i, j: (i, 0)),
                  pl.BlockSpec((None, K, TN_PROJ), lambda i, j: (l, 0, j))],
        out_specs=pl.BlockSpec((tm, TN_PROJ), lambda i, j: (i, j)),
        out_shape=jax.ShapeDtypeStruct((T, N), BF16),
        compiler_params=_cparams(("parallel", "arbitrary")),
        name="in_proj",
    )(xb, w_all)


def _mla_prep_kernel(q_ref, ckv_ref, cos_ref, sin_ref, g_ref, wukv_ref, qo_ref, ko_ref, vt_ref, *, qscale):
    cos = cos_ref[...]
    sin = sin_ref[...]
    lane = lax.broadcasted_iota(jnp.int32, cos.shape, 1)
    row_chunk = lax.broadcasted_iota(jnp.int32, cos.shape, 0) // CHUNK
    first_half = (lane & (QK_ROPE // 2)) == 0
    low = lane < QK_ROPE

    def rope(r):
        partner = jnp.where(first_half, pltpu.roll(r, LANES - QK_ROPE // 2, axis=1), pltpu.roll(r, QK_ROPE // 2, axis=1))
        return r * cos + partner * sin

    k_rope = None
    for h in range(MLA_HEADS):
        c0 = h * HEAD_W
        qo_ref[:, c0:c0 + LANES] = (q_ref[:, c0:c0 + LANES].astype(F32) * qscale).astype(BF16)
        rr = rope(q_ref[:, c0 + LANES:c0 + HEAD_W].astype(F32))
        qo_ref[:, c0 + LANES:c0 + HEAD_W] = jnp.where(low, rr * qscale, 0.0).astype(BF16)
        if h == 0:
            stair = jnp.where(row_chunk > lane - QK_ROPE, -MASK_BIG, 0.0)
            k_rope = jnp.where(low, pltpu.roll(rr, QK_ROPE, axis=1), stair).astype(BF16)

    c = ckv_ref[...].astype(F32)
    cn = (_rms_rows(c) * g_ref[...]).astype(BF16)
    kv = jnp.dot(cn, wukv_ref[...], preferred_element_type=F32)
    for h in range(MLA_HEADS):
        c0 = h * HEAD_W
        ko_ref[:, c0:c0 + LANES] = kv[:, c0:c0 + LANES].astype(BF16)
        ko_ref[:, c0 + LANES:c0 + HEAD_W] = k_rope
        vt_ref[h] = kv[:, c0 + LANES:c0 + HEAD_W].T.astype(BF16)


def _mla_prep(proj, cos4, sin4, kv_g, w_ukv_all, l):
    T = proj.shape[0]
    tm = min(TM_PREP, T)
    qscale = math.log2(math.e) / math.sqrt(QK_NOPE + QK_ROPE)
    return pl.pallas_call(
        functools.partial(_mla_prep_kernel, qscale=qscale),
        grid=(T // tm,),
        in_specs=[pl.BlockSpec((tm, Q_COLS), lambda i: (i, 0)),
                  pl.BlockSpec((tm, GROUP_W), lambda i: (i, COL_CKV)),
                  pl.BlockSpec((tm, LANES), lambda i: (i, 0)),
                  pl.BlockSpec((tm, LANES), lambda i: (i, 0)),
                  pl.BlockSpec((1, KV_RANK), lambda i: (0, 0)),
                  pl.BlockSpec((None,) + w_ukv_all.shape[1:], lambda i: (l, 0, 0))],
        out_specs=[pl.BlockSpec((tm, Q_COLS), lambda i: (i, 0)),
                   pl.BlockSpec((tm, Q_COLS), lambda i: (i, 0)),
                   pl.BlockSpec((None, MLA_HEADS, V_HEAD, tm), lambda i: (i, 0, 0, 0))],
        out_shape=[jax.ShapeDtypeStruct((T, Q_COLS), BF16),
                   jax.ShapeDtypeStruct((T, Q_COLS), BF16),
                   jax.ShapeDtypeStruct((T // tm, MLA_HEADS, V_HEAD, tm), BF16)],
        compiler_params=_cparams(("parallel",)),
        name="mla_prep",
    )(proj, proj, cos4, sin4, kv_g, w_ukv_all)


def _attention_kernel(q_ref, k_ref, vt_ref, o_ref, q2_ref, s0_ref, s1_ref, m_ref, l_ref, acc_ref, *, tb):
    i = pl.program_id(2)
    hp = ATT_HEADS
    q = q_ref[...]
    lane = lax.broadcasted_iota(jnp.int32, q.shape, 1) % HEAD_W
    row_chunk = lax.broadcasted_iota(jnp.int32, q.shape, 0) // CHUNK
    q2_ref[0] = q
    q2_ref[1] = jnp.where(lane == QK_NOPE + QK_ROPE + row_chunk, jnp.ones_like(q), q)
    m_ref[...] = jnp.full(m_ref.shape, -jnp.inf, F32)
    l_ref[...] = jnp.zeros(l_ref.shape, F32)
    acc_ref[...] = jnp.zeros(acc_ref.shape, F32)

    def scores(j, s_ref):
        r0 = pl.multiple_of(j * tb, tb)
        sel = (j == i).astype(jnp.int32)
        for h in range(hp):
            cs = slice(h * HEAD_W, (h + 1) * HEAD_W)
            s_ref[h] = lax.dot_general(k_ref[pl.ds(r0, tb), cs], q2_ref[sel, :, cs], (((1,), (1,)), ((), ())),
                                       preferred_element_type=F32)

    def softmax_pv(j, s_ref):
        for h in range(hp):
            s = s_ref[h]
            m_old = m_ref[h]
            m_new = jnp.maximum(m_old, jnp.max(s, axis=0, keepdims=True))
            p = jnp.exp2(s - m_new)
            a = jnp.exp2(m_old - m_new)
            l_ref[h] = a * l_ref[h] + jnp.sum(p, axis=0, keepdims=True)
            acc_ref[h] = a * acc_ref[h] + jnp.dot(vt_ref[j, h], p.astype(BF16), preferred_element_type=F32)
            m_ref[h] = m_new

    scores(0, s0_ref)
    bufs = (s0_ref, s1_ref)
    U = ATT_BLOCKS_PER_TRIP

    def trip(g, carry):
        j = U * g
        for u in range(U):
            scores(j + u + 1, bufs[(u + 1) % 2])
            softmax_pv(j + u, bufs[u % 2])
        return carry

    trips = i // U
    lax.fori_loop(0, trips, trip, 0)
    base = U * trips
    for u in range(U):
        j = base + u

        @pl.when(j < i)
        def _():
            scores(j + 1, bufs[(u + 1) % 2])
            softmax_pv(j, bufs[u % 2])

        @pl.when(j == i)
        def _():
            softmax_pv(j, bufs[u % 2])

    for h in range(hp):
        o_ref[:, h * V_HEAD:(h + 1) * V_HEAD] = (acc_ref[h] / l_ref[h]).T


def _attention(q_cat, k_cat, v_t, B, S):
    T = B * S
    tb = v_t.shape[-1]
    nq = S // tb
    hp = ATT_HEADS
    resident = dict(pipeline_mode=pl.Buffered(1))
    return pl.pallas_call(
        functools.partial(_attention_kernel, tb=tb),
        grid=(B, MLA_HEADS // hp, nq),
        in_specs=[pl.BlockSpec((tb, hp * HEAD_W), lambda b, h, i: (b * nq + i, h)),
                  pl.BlockSpec((S, hp * HEAD_W), lambda b, h, i: (b, h), **resident),
                  pl.BlockSpec((nq, hp, V_HEAD, tb), lambda b, h, i: (b, h, 0, 0), **resident)],
        out_specs=pl.BlockSpec((tb, hp * V_HEAD), lambda b, h, i: (b * nq + i, h)),
        out_shape=jax.ShapeDtypeStruct((T, MLA_HEADS * V_HEAD), F32),
        scratch_shapes=[pltpu.VMEM((2, tb, hp * HEAD_W), BF16),
                        pltpu.VMEM((hp, tb, tb), F32), pltpu.VMEM((hp, tb, tb), F32),
                        pltpu.VMEM((hp, 1, tb), F32), pltpu.VMEM((hp, 1, tb), F32),
                        pltpu.VMEM((hp, V_HEAD, tb), F32)],
        compiler_params=_cparams(("parallel", "parallel", "arbitrary")),
        name="attention",
    )(q_cat, k_cat, v_t)


def _gelu(x):
    return 0.5 * x * (1.0 + lax.erf(x * (1.0 / math.sqrt(2.0))))


def _mixers_kernel(cb_ref, cc_ref, ch_ref, ph_ref, gu_ref, gv_ref, hcc_ref, hch_ref, hph_ref,
                   convw_ref, poolw_ref, lng_ref, lnb_ref, sguw_ref, sgub_ref, gain_ref,
                   yc_ref, yp_ref, ys_ref,
                   g_scr, e_scr, a_scr, b_scr, y_scr, *, tm, tiles_per_seq):
    i = pl.program_id(0)
    t0 = (i % tiles_per_seq) * tm
    keep = jnp.where(t0 == 0, 0.0, 1.0).astype(F32)

    g_scr[0:HALO, :] = hcc_ref[...].astype(F32) * hch_ref[...].astype(F32) * keep
    g_scr[HALO:, :] = cc_ref[...].astype(F32) * ch_ref[...].astype(F32)
    conv = convw_ref[CONV_W - 1:CONV_W, :] * g_scr[HALO:HALO + tm, :]
    for j in range(CONV_W - 1):
        off = HALO - (CONV_W - 1) + j
        conv = conv + convw_ref[j:j + 1, :] * g_scr[off:off + tm, :]
    yc = cb_ref[...].astype(F32) * conv
    yc_ref[...] = (_rms_rows(yc) * gain_ref[:, GROUP_W:2 * GROUP_W]).astype(BF16)

    e_scr[0:HALO, :] = hph_ref[...].astype(F32) * keep
    e_scr[HALO:, :] = ph_ref[...].astype(F32)
    n = tm + HALO
    pos = (t0 + 1 + lax.broadcasted_iota(jnp.int32, (tm, 1), 0)).astype(F32)
    for g, w in enumerate(POOL_WINDOWS):
        cs = slice(g * POOL_CH, (g + 1) * POOL_CH)
        levels = g + 1
        src = None
        bufs = (a_scr, b_scr)
        total = None
        for lev in range(1, levels + 1):
            shift = 1 << (lev - 1)
            lo = 8 * lev if lev < levels else HALO
            if src is None:
                cur = e_scr[lo:n, cs] + e_scr[lo - shift:n - shift, cs]
            else:
                cur = src[lo:n, :] + src[lo - shift:n - shift, :]
            if lev < levels:
                dst = bufs[(lev - 1) % 2]
                dst[lo:n, :] = cur
                src = dst
            else:
                total = cur
        inv = 1.0 / jnp.minimum(pos, float(w))
        pooled = total * inv - e_scr[HALO:, cs]
        y_scr[:, cs] = jnp.dot(pooled.astype(BF16), poolw_ref[g], preferred_element_type=F32)
    yp = y_scr[...]
    yp_ref[...] = (_rms_rows(yp) * gain_ref[:, 2 * GROUP_W:3 * GROUP_W]).astype(BF16)

    u = _gelu(gu_ref[...].astype(F32))
    vn = _layer_norm_rows(_gelu(gv_ref[...].astype(F32)), lng_ref[...], lnb_ref[...]).astype(BF16)
    row = lax.broadcasted_iota(jnp.int32, (SGU_LEN, SGU_LEN), 0)
    col = lax.broadcasted_iota(jnp.int32, (SGU_LEN, SGU_LEN), 1)
    for g in range(SGU_GROUPS):
        cs = slice(g * SGU_CH, (g + 1) * SGU_CH)
        wg = jnp.where(col <= row, sguw_ref[g], 0.0).astype(BF16)
        bias = sgub_ref[:, g:g + 1]
        for c in range(tm // SGU_LEN):
            rs = slice(c * SGU_LEN, (c + 1) * SGU_LEN)
            mixed = jnp.dot(wg, vn[rs, cs], preferred_element_type=F32) + bias
            y_scr[rs, cs] = u[rs, cs] * mixed
    ysg = y_scr[...]
    ys_ref[...] = (_rms_rows(ysg) * gain_ref[:, 3 * GROUP_W:4 * GROUP_W]).astype(BF16)


def _mixers(proj, conv_w, pool_w, ln_g, ln_b, sgu_w, sgu_bt, gain, S):
    T = proj.shape[0]
    tm = min(TM_MIX, S)
    tiles_per_seq = S // tm
    hb = tm // HALO

    def col(c):
        return pl.BlockSpec((tm, GROUP_W), lambda i: (i, c))

    def halo(c):
        return pl.BlockSpec((HALO, GROUP_W), lambda i: (jnp.maximum(i * hb - 1, 0), c))

    def full(a):
        return pl.BlockSpec(a.shape, lambda i: (0,) * a.ndim)

    out = jax.ShapeDtypeStruct((T, GROUP_W), BF16)
    return pl.pallas_call(
        functools.partial(_mixers_kernel, tm=tm, tiles_per_seq=tiles_per_seq),
        grid=(T // tm,),
        in_specs=[col(COL_CB), col(COL_CC), col(COL_CH), col(COL_PH), col(COL_GU), col(COL_GV),
                  halo(COL_CC), halo(COL_CH), halo(COL_PH),
                  full(conv_w), full(pool_w), full(ln_g), full(ln_b), full(sgu_w), full(sgu_bt), full(gain)],
        out_specs=[pl.BlockSpec((tm, GROUP_W), lambda i: (i, 0))] * 3,
        out_shape=[out, out, out],
        scratch_shapes=[pltpu.VMEM((tm + HALO, GROUP_W), F32), pltpu.VMEM((tm + HALO, GROUP_W), F32),
                        pltpu.VMEM((tm + HALO, POOL_CH), F32), pltpu.VMEM((tm + HALO, POOL_CH), F32),
                        pltpu.VMEM((tm, GROUP_W), F32)],
        compiler_params=_cparams(("parallel",)),
        name="mixers",
    )(proj, proj, proj, proj, proj, proj, proj, proj, proj,
      conv_w, pool_w, ln_g, ln_b, sgu_w, sgu_bt, gain)


def _out_proj_kernel(*refs, with_router):
    if with_router:
        (o_ref, yc_ref, yp_ref, ys_ref, gain_ref, wo_ref, x_ref, g_ref, b_ref, rw_ref,
         x1_ref, x1b_ref, lg_ref) = refs
    else:
        (o_ref, yc_ref, yp_ref, ys_ref, gain_ref, wo_ref, x_ref, g_ref, b_ref,
         x1_ref, x1b_ref) = refs
    ym = (_rms_rows(o_ref[...]) * gain_ref[:, 0:GROUP_W]).astype(BF16)
    mix = jnp.dot(ym, wo_ref[0:GROUP_W, :], preferred_element_type=F32)
    for gi, y_ref in enumerate((yc_ref, yp_ref, ys_ref), start=1):
        mix = mix + jnp.dot(y_ref[...], wo_ref[gi * GROUP_W:(gi + 1) * GROUP_W, :], preferred_element_type=F32)
    x1 = _layer_norm_rows(ALPHA * x_ref[...] + mix, g_ref[...], b_ref[...])
    x1_ref[...] = x1
    x1_hi = x1.astype(BF16)
    x1b_ref[...] = x1_hi
    if with_router:
        x1_lo = (x1 - x1_hi.astype(F32)).astype(BF16)
        lg_ref[...] = (jnp.dot(x1_hi, rw_ref[0], preferred_element_type=F32)
                       + (jnp.dot(x1_hi, rw_ref[1], preferred_element_type=F32)
                          + jnp.dot(x1_lo, rw_ref[0], preferred_element_type=F32)))


def _out_proj(o_mla, yc, yp, ys, gain, wo_all, l, x, ln_g, ln_b, router_hl=None):
    T, D = x.shape
    tm = min(TM_OUT, T)
    with_router = router_hl is not None

    def rows(w):
        return pl.BlockSpec((tm, w), lambda i: (i, 0))

    def full(a):
        return pl.BlockSpec(a.shape, lambda i: (0,) * a.ndim)

    wo_spec = pl.BlockSpec((None,) + wo_all.shape[1:], lambda i: (l, 0, 0))
    in_specs = [rows(GROUP_W)] * 4 + [full(gain), wo_spec, rows(D), full(ln_g), full(ln_b)]
    args = [o_mla, yc, yp, ys, gain, wo_all, x, ln_g, ln_b]
    out_specs = [rows(D), rows(D)]
    out_shape = [jax.ShapeDtypeStruct((T, D), F32), jax.ShapeDtypeStruct((T, D), BF16)]
    if with_router:
        in_specs.append(full(router_hl))
        args.append(router_hl)
        out_specs.append(rows(LANES))
        out_shape.append(jax.ShapeDtypeStruct((T, LANES), F32))
    return pl.pallas_call(
        functools.partial(_out_proj_kernel, with_router=with_router),
        grid=(T // tm,),
        in_specs=in_specs,
        out_specs=out_specs,
        out_shape=out_shape,
        compiler_params=_cparams(("parallel",)),
        name="out_proj",
    )(*args)


def _swiglu(xb, wg, wu):
    hg = jnp.dot(xb, wg, preferred_element_type=F32)
    hu = jnp.dot(xb, wu, preferred_element_type=F32)
    return (hg * jax.nn.sigmoid(hg) * hu).astype(BF16)


def _ffn_moe_kernel(be_ref, nv_ref, xb_ref, wg_ref, wu_ref, wd_ref, o_ref, acc_ref):
    i = pl.program_id(0)
    f = pl.program_id(1)

    @pl.when(f == 0)
    def _():
        acc_ref[...] = jnp.zeros(acc_ref.shape, F32)

    @pl.when(i < nv_ref[0])
    def _():
        h = _swiglu(xb_ref[...], wg_ref[...], wu_ref[...])
        acc_ref[...] += jnp.dot(h, wd_ref[...], preferred_element_type=F32)

    @pl.when(f == pl.num_programs(1) - 1)
    def _():
        o_ref[...] = acc_ref[...].astype(o_ref.dtype)


def _ffn_moe(x_rows_b, wg, wu, wd, block_e, n_valid):
    R, D = x_rows_b.shape
    F = wg.shape[2]
    tm = min(TM_MOE, R)
    tf = TF_FFN
    nf = F // tf

    def fidx(i, f, nv):
        return jnp.where(i < nv[0], f, nf - 1)

    return pl.pallas_call(
        _ffn_moe_kernel,
        grid_spec=pltpu.PrefetchScalarGridSpec(
            num_scalar_prefetch=2,
            grid=(R // tm, nf),
            in_specs=[pl.BlockSpec((tm, D), lambda i, f, be, nv: (i, 0)),
                      pl.BlockSpec((None, D, tf), lambda i, f, be, nv: (be[i], 0, fidx(i, f, nv))),
                      pl.BlockSpec((None, D, tf), lambda i, f, be, nv: (be[i], 0, fidx(i, f, nv))),
                      pl.BlockSpec((None, tf, D), lambda i, f, be, nv: (be[i], fidx(i, f, nv), 0))],
            out_specs=pl.BlockSpec((tm, D), lambda i, f, be, nv: (i, 0)),
            scratch_shapes=[pltpu.VMEM((tm, D), F32)]),
        out_shape=jax.ShapeDtypeStruct((R, D), BF16),
        compiler_params=_cparams(("parallel", "arbitrary")),
        name="ffn_moe",
    )(block_e, n_valid, x_rows_b, wg, wu, wd)


def _ffn_up_kernel(x_ref, wg_ref, wu_ref, h_ref):
    h_ref[...] = _swiglu(x_ref[...], wg_ref[...], wu_ref[...])


def _ffn_down_kernel(h_ref, wd_ref, xres_ref, g_ref, b_ref, o_ref, ob_ref, acc_ref):
    k = pl.program_id(1)

    @pl.when(k == 0)
    def _():
        acc_ref[...] = jnp.zeros(acc_ref.shape, F32)

    acc_ref[...] += jnp.dot(h_ref[...], wd_ref[...], preferred_element_type=F32)

    @pl.when(k == pl.num_programs(1) - 1)
    def _():
        x2 = _layer_norm_rows(ALPHA * xres_ref[...] + acc_ref[...], g_ref[...], b_ref[...])
        o_ref[...] = x2
        ob_ref[...] = x2.astype(BF16)


def _ffn_dense(xb, wg, wu, wd, e, xres, ln_g, ln_b):
    T, D = xb.shape
    F = wg.shape[2]
    tm = min(TM_UP, T)
    h = pl.pallas_call(
        _ffn_up_kernel,
        grid=(T // tm, F // TF_FFN),
        in_specs=[pl.BlockSpec((tm, D), lambda i, f: (i, 0)),
                  pl.BlockSpec((None, D, TF_FFN), lambda i, f: (e, 0, f)),
                  pl.BlockSpec((None, D, TF_FFN), lambda i, f: (e, 0, f))],
        out_specs=pl.BlockSpec((tm, TF_FFN), lambda i, f: (i, f)),
        out_shape=jax.ShapeDtypeStruct((T, F), BF16),
        compiler_params=_cparams(("parallel", "arbitrary")),
        name="ffn_up",
    )(xb, wg, wu)
    tmd = min(TM_DOWN, T)
    rows = pl.BlockSpec((tmd, D), lambda i, k: (i, 0))
    vec = pl.BlockSpec((1, D), lambda i, k: (0, 0))
    return pl.pallas_call(
        _ffn_down_kernel,
        grid=(T // tmd, F // TK_DOWN),
        in_specs=[pl.BlockSpec((tmd, TK_DOWN), lambda i, k: (i, k)),
                  pl.BlockSpec((None, TK_DOWN, D), lambda i, k: (e, k, 0)),
                  rows, vec, vec],
        out_specs=[rows, rows],
        out_shape=[jax.ShapeDtypeStruct((T, D), F32), jax.ShapeDtypeStruct((T, D), BF16)],
        scratch_shapes=[pltpu.VMEM((tmd, D), F32)],
        compiler_params=_cparams(("parallel", "arbitrary")),
        name="ffn_down",
    )(h, wd, xres, ln_g, ln_b)


def _combine_kernel(x_ref, y0_ref, y1_ref, g0_ref, g1_ref, g_ref, b_ref, o_ref, ob_ref):
    ffn = y0_ref[...].astype(F32) * g0_ref[...] + y1_ref[...].astype(F32) * g1_ref[...]
    x2 = _layer_norm_rows(ALPHA * x_ref[...] + ffn, g_ref[...], b_ref[...])
    o_ref[...] = x2
    ob_ref[...] = x2.astype(BF16)


def _combine(x, y0, y1, g0, g1, ln_g, ln_b):
    T, D = x.shape
    tm = min(TM_OUT, T)
    rows = pl.BlockSpec((tm, D), lambda i: (i, 0))
    gate = pl.BlockSpec((tm, 1), lambda i: (i, 0))
    vec = pl.BlockSpec((1, D), lambda i: (0, 0))
    return pl.pallas_call(
        _combine_kernel,
        grid=(T // tm,),
        in_specs=[rows, rows, rows, gate, gate, vec, vec],
        out_specs=[rows, rows],
        out_shape=[jax.ShapeDtypeStruct((T, D), F32), jax.ShapeDtypeStruct((T, D), BF16)],
        compiler_params=_cparams(("parallel",)),
        name="moe_combine",
    )(x, y0, y1, g0, g1, ln_g, ln_b)


def _pad_w_in(w_in):
    L, D, _ = w_in.shape
    q_dim = MLA_HEADS * (QK_NOPE + QK_ROPE)
    wq = w_in[:, :, :q_dim].reshape(L, D, MLA_HEADS, QK_NOPE + QK_ROPE)
    kr0 = q_dim + KV_RANK
    w_kr = w_in[:, :, kr0:kr0 + QK_ROPE]
    pad = jnp.zeros((L, D, MLA_HEADS, HEAD_W - QK_NOPE - QK_ROPE), w_in.dtype).at[:, :, 0, :].set(w_kr)
    q_part = jnp.concatenate([wq, pad], axis=-1).reshape(L, D, Q_COLS)
    rest = jnp.concatenate([w_in[:, :, q_dim:kr0], w_in[:, :, kr0 + QK_ROPE:]], axis=-1)
    return jnp.concatenate([q_part, rest], axis=-1).astype(BF16)


def _rope_tables(positions):
    inv_freq = ROPE_THETA ** (-jnp.arange(0, QK_ROPE, 2, dtype=F32) / QK_ROPE)
    ang = positions.astype(F32).reshape(-1)[:, None] * inv_freq
    cos, sin = jnp.cos(ang), jnp.sin(ang)
    cos4 = jnp.concatenate([cos, cos, cos, cos], axis=-1)
    sin4 = jnp.concatenate([-sin, sin, -sin, sin], axis=-1)
    return cos4, sin4


def _route(logits, tm):
    T = logits.shape[0]
    A = T * TOP_K
    top_logit, top_e = lax.top_k(logits, TOP_K)
    gates = jax.nn.softmax(top_logit, axis=-1)
    flat_e = top_e.reshape(A)
    onehot = (flat_e[:, None] == jnp.arange(N_EXPERTS, dtype=flat_e.dtype)[None, :]).astype(jnp.int32)
    csum = jnp.cumsum(onehot, axis=0)
    counts = csum[-1]
    rank = jnp.take_along_axis(csum, flat_e[:, None], axis=1)[:, 0] - 1
    padded = (counts + tm - 1) // tm * tm
    pad_ends = jnp.cumsum(padded)
    pad_starts = pad_ends - padded
    dest = (pad_starts[flat_e] + rank).astype(jnp.int32)
    n_blocks = -(-(A + N_EXPERTS * (tm - 1)) // tm)
    n_pad = n_blocks * tm
    tok = (jnp.arange(A, dtype=jnp.int32) // TOP_K)
    slot_tok = jnp.zeros((n_pad,), jnp.int32).at[dest].set(tok)
    block_start = jnp.arange(n_blocks, dtype=jnp.int32) * tm
    block_e = jnp.minimum(jnp.searchsorted(pad_ends, block_start, side='right'), N_EXPERTS - 1).astype(jnp.int32)
    n_valid = (pad_ends[-1] // tm).astype(jnp.int32).reshape(1)
    return dest.reshape(T, TOP_K), gates, slot_tok, block_e, n_valid


def kernel(x, positions, w_in, kv_norm_g, w_ukv, conv_w, pool_w, sgu_ln_g, sgu_ln_b, sgu_w, sgu_b, mix_gain, w_o, ln1_g, ln1_b, ffn_wg, ffn_wu, ffn_wd, router_w, exp_wg, exp_wu, exp_wd, ln2_g, ln2_b):
    B, S, D = x.shape
    T = B * S
    L = w_in.shape[0]
    cos4, sin4 = _rope_tables(positions)
    w_in_b = _pad_w_in(w_in)
    w_ukv_b = w_ukv.astype(BF16)
    w_o_b = w_o.astype(BF16)
    pool_w_b = pool_w.astype(BF16)
    sgu_bt = jnp.swapaxes(sgu_b, 1, 2)
    router_pad = jnp.pad(router_w, ((0, 0), (0, 0), (0, LANES - N_EXPERTS)))
    router_hi = router_pad.astype(BF16)
    router_hl = jnp.stack([router_hi, (router_pad - router_hi.astype(F32)).astype(BF16)], axis=1)
    ffn_w = (ffn_wg.astype(BF16), ffn_wu.astype(BF16), ffn_wd.astype(BF16))
    exp_w = tuple(w.astype(BF16).reshape((-1,) + w.shape[2:]) for w in (exp_wg, exp_wu, exp_wd))

    xf = x.reshape(T, D)
    xb = xf.astype(BF16)
    tm_moe = min(TM_MOE, T)
    for l in range(L):
        proj = _in_proj(xb, w_in_b, l)
        q_cat, k_cat, v_t = _mla_prep(proj, cos4, sin4, kv_norm_g[l][None, :], w_ukv_b, l)
        o_mla = _attention(q_cat, k_cat, v_t, B, S)
        yc, yp, ys = _mixers(proj, conv_w[l], pool_w_b[l], sgu_ln_g[l][None, :], sgu_ln_b[l][None, :],
                             sgu_w[l], sgu_bt[l], mix_gain[l][None, :], S)
        ln1 = (ln1_g[l][None, :], ln1_b[l][None, :])
        ln2 = (ln2_g[l][None, :], ln2_b[l][None, :])
        e = l // 2
        if l % 2 == 0:
            x1, x1b = _out_proj(o_mla, yc, yp, ys, mix_gain[l][None, :], w_o_b, l, xf, *ln1)
            xf, xb = _ffn_dense(x1b, *ffn_w, e, x1, *ln2)
        else:
            x1, x1b, logits = _out_proj(o_mla, yc, yp, ys, mix_gain[l][None, :], w_o_b, l, xf, *ln1,
                                        router_hl=router_hl[e])
            dest, gates, slot_tok, block_e, n_valid = _route(logits[:, :N_EXPERTS], tm_moe)
            x_rows = x1b.at[slot_tok].get(mode="promise_in_bounds")
            y_rows = _ffn_moe(x_rows, *exp_w, block_e + e * N_EXPERTS, n_valid)
            y0 = y_rows.at[dest[:, 0]].get(mode="promise_in_bounds")
            y1 = y_rows.at[dest[:, 1]].get(mode="promise_in_bounds")
            xf, xb = _combine(x1, y0, y1, gates[:, 0:1], gates[:, 1:2], *ln2)
    return xf.reshape(B, S, D)
```

```python
import functools
import math

import jax
import jax.numpy as jnp
from jax import lax
from jax.experimental import pallas as pl
from jax.experimental.pallas import tpu as pltpu

F32 = jnp.float32
BF16 = jnp.bfloat16

D_MODEL = 2048
CHUNK = 64
GROUP_W = 512
N_GROUPS = 4
MLA_HEADS = 4
QK_NOPE = 128
QK_ROPE = 64
V_HEAD = 128
KV_RANK = 512
ROPE_THETA = 10000.0
CONV_W = 3
POOL_WINDOWS = (2, 4, 8, 16)
POOL_CH = 128
SGU_LEN = 128
SGU_GROUPS = 4
SGU_CH = 128
D_FF = 5632
N_EXPERTS = 8
TOP_K = 2
DEPTH = 4
ALPHA = (2.0 * DEPTH) ** 0.25
LN_EPS = 1e-5
RMS_EPS = 1e-6

LANES = 128
HEAD_W = 2 * LANES
VMEM_LIMIT = 56 * 1024 * 1024

Q_COLS = MLA_HEADS * HEAD_W
COL_Q, COL_CKV, COL_CB, COL_CC, COL_CH, COL_PH, COL_GU, COL_GV = 0, 2, 3, 4, 5, 6, 7, 8
IN_COLS_PAD = Q_COLS + 7 * GROUP_W

HALO = 32

TM_PROJ, TN_PROJ = 1024, 2304
TB_ATT = 1024
ATT_HEADS = 2
ATT_BLOCKS_PER_TRIP = 2
TM_PREP = TB_ATT
MASK_BIG = 2.0 ** 100
TM_MIX = 512
TM_OUT = 256
TM_FFN, TF_FFN = 512, 512
CAST_STEPS = 8
MOE_CHUNKS = 4


def _cparams(sem):
    return pltpu.CompilerParams(dimension_semantics=sem, vmem_limit_bytes=VMEM_LIMIT)


def _layer_norm_rows(z, g, b):
    mu = jnp.mean(z, axis=-1, keepdims=True)
    zc = z - mu
    var = jnp.mean(zc * zc, axis=-1, keepdims=True)
    return zc * lax.rsqrt(var + LN_EPS) * g + b


def _rms_rows(y):
    return y * lax.rsqrt(jnp.mean(y * y, axis=-1, keepdims=True) + RMS_EPS)


def _matmul_kernel(x_ref, w_ref, o_ref):
    o_ref[...] = jnp.dot(x_ref[...], w_ref[...], preferred_element_type=F32).astype(o_ref.dtype)


def _in_proj(xb, w_all, l):
    T, K = xb.shape
    N = w_all.shape[2]
    tm = min(TM_PROJ, T)
    return pl.pallas_call(
        _matmul_kernel,
        grid=(T // tm, N // TN_PROJ),
        in_specs=[pl.BlockSpec((tm, K), lambda i, j: (i, 0)),
                  pl.BlockSpec((None, K, TN_PROJ), lambda i, j: (l, 0, j))],
        out_specs=pl.BlockSpec((tm, TN_PROJ), lambda i, j: (i, j)),
        out_shape=jax.ShapeDtypeStruct((T, N), BF16),
        compiler_params=_cparams(("parallel", "arbitrary")),
        name="in_proj",
    )(xb, w_all)


def _mla_prep_kernel(q_ref, ckv_ref, cos_ref, sin_ref, g_ref, wukv_ref, qo_ref, ko_ref, vt_ref, *, qscale):
    cos = cos_ref[...]
    sin = sin_ref[...]
    lane = lax.broadcasted_iota(jnp.int32, cos.shape, 1)
    row_chunk = lax.broadcasted_iota(jnp.int32, cos.shape, 0) // CHUNK
    first_half = (lane & (QK_ROPE // 2)) == 0
    low = lane < QK_ROPE

    def rope(r):
        partner = jnp.where(first_half, pltpu.roll(r, LANES - QK_ROPE // 2, axis=1), pltpu.roll(r, QK_ROPE // 2, axis=1))
        return r * cos + partner * sin

    k_rope = None
    for h in range(MLA_HEADS):
        c0 = h * HEAD_W
        qo_ref[:, c0:c0 + LANES] = (q_ref[:, c0:c0 + LANES].astype(F32) * qscale).astype(BF16)
        rr = rope(q_ref[:, c0 + LANES:c0 + HEAD_W].astype(F32))
        qo_ref[:, c0 + LANES:c0 + HEAD_W] = jnp.where(low, rr * qscale, 0.0).astype(BF16)
        if h == 0:
            stair = jnp.where(row_chunk > lane - QK_ROPE, -MASK_BIG, 0.0)
            k_rope = jnp.where(low, pltpu.roll(rr, QK_ROPE, axis=1), stair).astype(BF16)

    c = ckv_ref[...].astype(F32)
    cn = (_rms_rows(c) * g_ref[...]).astype(BF16)
    kv = jnp.dot(cn, wukv_ref[...], preferred_element_type=F32)
    for h in range(MLA_HEADS):
        c0 = h * HEAD_W
        ko_ref[:, c0:c0 + LANES] = kv[:, c0:c0 + LANES].astype(BF16)
        ko_ref[:, c0 + LANES:c0 + HEAD_W] = k_rope
        vt_ref[h] = kv[:, c0 + LANES:c0 + HEAD_W].T.astype(BF16)


def _mla_prep(proj, cos4, sin4, kv_g, w_ukv_all, l):
    T = proj.shape[0]
    tm = min(TM_PREP, T)
    qscale = math.log2(math.e) / math.sqrt(QK_NOPE + QK_ROPE)
    return pl.pallas_call(
        functools.partial(_mla_prep_kernel, qscale=qscale),
        grid=(T // tm,),
        in_specs=[pl.BlockSpec((tm, Q_COLS), lambda i: (i, 0)),
                  pl.BlockSpec((tm, GROUP_W), lambda i: (i, COL_CKV)),
                  pl.BlockSpec((tm, LANES), lambda i: (i, 0)),
                  pl.BlockSpec((tm, LANES), lambda i: (i, 0)),
                  pl.BlockSpec((1, KV_RANK), lambda i: (0, 0)),
                  pl.BlockSpec((None,) + w_ukv_all.shape[1:], lambda i: (l, 0, 0))],
        out_specs=[pl.BlockSpec((tm, Q_COLS), lambda i: (i, 0)),
                   pl.BlockSpec((tm, Q_COLS), lambda i: (i, 0)),
                   pl.BlockSpec((None, MLA_HEADS, V_HEAD, tm), lambda i: (i, 0, 0, 0))],
        out_shape=[jax.ShapeDtypeStruct((T, Q_COLS), BF16),
                   jax.ShapeDtypeStruct((T, Q_COLS), BF16),
                   jax.ShapeDtypeStruct((T // tm, MLA_HEADS, V_HEAD, tm), BF16)],
        compiler_params=_cparams(("parallel",)),
        name="mla_prep",
    )(proj, proj, cos4, sin4, kv_g, w_ukv_all)


def _attention_kernel(q_ref, k_ref, vt_ref, o_ref, q2_ref, s0_ref, s1_ref, m_ref, l_ref, acc_ref, *, tb):
    i = pl.program_id(2)
    hp = ATT_HEADS
    q = q_ref[...]
    lane = lax.broadcasted_iota(jnp.int32, q.shape, 1) % HEAD_W
    row_chunk = lax.broadcasted_iota(jnp.int32, q.shape, 0) // CHUNK
    q2_ref[0] = q
    q2_ref[1] = jnp.where(lane == QK_NOPE + QK_ROPE + row_chunk, jnp.ones_like(q), q)
    m_ref[...] = jnp.full(m_ref.shape, -jnp.inf, F32)
    l_ref[...] = jnp.zeros(l_ref.shape, F32)
    acc_ref[...] = jnp.zeros(acc_ref.shape, F32)

    def scores(j, s_ref):
        r0 = pl.multiple_of(j * tb, tb)
        sel = (j == i).astype(jnp.int32)
        for h in range(hp):
            cs = slice(h * HEAD_W, (h + 1) * HEAD_W)
            s_ref[h] = lax.dot_general(k_ref[pl.ds(r0, tb), cs], q2_ref[sel, :, cs], (((1,), (1,)), ((), ())),
                                       preferred_element_type=F32)

    def softmax_pv(j, s_ref):
        for h in range(hp):
            s = s_ref[h]
            m_old = m_ref[h]
            m_new = jnp.maximum(m_old, jnp.max(s, axis=0, keepdims=True))
            p = jnp.exp2(s - m_new)
            a = jnp.exp2(m_old - m_new)
            l_ref[h] = a * l_ref[h] + jnp.sum(p, axis=0, keepdims=True)
            acc_ref[h] = a * acc_ref[h] + jnp.dot(vt_ref[j, h], p.astype(BF16), preferred_element_type=F32)
            m_ref[h] = m_new

    scores(0, s0_ref)
    bufs = (s0_ref, s1_ref)
    U = ATT_BLOCKS_PER_TRIP

    def trip(g, carry):
        j = U * g
        for u in range(U):
            scores(j + u + 1, bufs[(u + 1) % 2])
            softmax_pv(j + u, bufs[u % 2])
        return carry

    trips = i // U
    lax.fori_loop(0, trips, trip, 0)
    base = U * trips
    for u in range(U):
        j = base + u

        @pl.when(j < i)
        def _():
            scores(j + 1, bufs[(u + 1) % 2])
            softmax_pv(j, bufs[u % 2])

        @pl.when(j == i)
        def _():
            softmax_pv(j, bufs[u % 2])

    for h in range(hp):
        o_ref[:, h * V_HEAD:(h + 1) * V_HEAD] = (acc_ref[h] / l_ref[h]).T


def _attention(q_cat, k_cat, v_t, B, S):
    T = B * S
    tb = v_t.shape[-1]
    nq = S // tb
    hp = ATT_HEADS
    resident = dict(pipeline_mode=pl.Buffered(1))
    return pl.pallas_call(
        functools.partial(_attention_kernel, tb=tb),
        grid=(B, MLA_HEADS // hp, nq),
        in_specs=[pl.BlockSpec((tb, hp * HEAD_W), lambda b, h, i: (b * nq + i, h)),
                  pl.BlockSpec((S, hp * HEAD_W), lambda b, h, i: (b, h), **resident),
                  pl.BlockSpec((nq, hp, V_HEAD, tb), lambda b, h, i: (b, h, 0, 0), **resident)],
        out_specs=pl.BlockSpec((tb, hp * V_HEAD), lambda b, h, i: (b * nq + i, h)),
        out_shape=jax.ShapeDtypeStruct((T, MLA_HEADS * V_HEAD), F32),
        scratch_shapes=[pltpu.VMEM((2, tb, hp * HEAD_W), BF16),
                        pltpu.VMEM((hp, tb, tb), F32), pltpu.VMEM((hp, tb, tb), F32),
                        pltpu.VMEM((hp, 1, tb), F32), pltpu.VMEM((hp, 1, tb), F32),
                        pltpu.VMEM((hp, V_HEAD, tb), F32)],
        compiler_params=_cparams(("parallel", "parallel", "arbitrary")),
        name="attention",
    )(q_cat, k_cat, v_t)


def _gelu(x):
    return 0.5 * x * (1.0 + lax.erf(x * (1.0 / math.sqrt(2.0))))


def _mixers_kernel(cb_ref, cc_ref, ch_ref, ph_ref, gu_ref, gv_ref, hcc_ref, hch_ref, hph_ref,
                   convw_ref, poolw_ref, lng_ref, lnb_ref, sguw_ref, sgub_ref, gain_ref,
                   yc_ref, yp_ref, ys_ref,
                   g_scr, e_scr, a_scr, b_scr, y_scr, *, tm, tiles_per_seq):
    i = pl.program_id(0)
    t0 = (i % tiles_per_seq) * tm
    keep = jnp.where(t0 == 0, 0.0, 1.0).astype(F32)

    g_scr[0:HALO, :] = hcc_ref[...].astype(F32) * hch_ref[...].astype(F32) * keep
    g_scr[HALO:, :] = cc_ref[...].astype(F32) * ch_ref[...].astype(F32)
    conv = convw_ref[CONV_W - 1:CONV_W, :] * g_scr[HALO:HALO + tm, :]
    for j in range(CONV_W - 1):
        off = HALO - (CONV_W - 1) + j
        conv = conv + convw_ref[j:j + 1, :] * g_scr[off:off + tm, :]
    yc = cb_ref[...].astype(F32) * conv
    yc_ref[...] = (_rms_rows(yc) * gain_ref[:, GROUP_W:2 * GROUP_W]).astype(BF16)

    e_scr[0:HALO, :] = hph_ref[...].astype(F32) * keep
    e_scr[HALO:, :] = ph_ref[...].astype(F32)
    n = tm + HALO
    pos = (t0 + 1 + lax.broadcasted_iota(jnp.int32, (tm, 1), 0)).astype(F32)
    for g, w in enumerate(POOL_WINDOWS):
        cs = slice(g * POOL_CH, (g + 1) * POOL_CH)
        levels = g + 1
        src = None
        bufs = (a_scr, b_scr)
        total = None
        for lev in range(1, levels + 1):
            shift = 1 << (lev - 1)
            lo = 8 * lev if lev < levels else HALO
            if src is None:
                cur = e_scr[lo:n, cs] + e_scr[lo - shift:n - shift, cs]
            else:
                cur = src[lo:n, :] + src[lo - shift:n - shift, :]
            if lev < levels:
                dst = bufs[(lev - 1) % 2]
                dst[lo:n, :] = cur
                src = dst
            else:
                total = cur
        inv = 1.0 / jnp.minimum(pos, float(w))
        pooled = total * inv - e_scr[HALO:, cs]
        y_scr[:, cs] = jnp.dot(pooled.astype(BF16), poolw_ref[g], preferred_element_type=F32)
    yp = y_scr[...]
    yp_ref[...] = (_rms_rows(yp) * gain_ref[:, 2 * GROUP_W:3 * GROUP_W]).astype(BF16)

    u = _gelu(gu_ref[...].astype(F32))
    vn = _layer_norm_rows(_gelu(gv_ref[...].astype(F32)), lng_ref[...], lnb_ref[...]).astype(BF16)
    row = lax.broadcasted_iota(jnp.int32, (SGU_LEN, SGU_LEN), 0)
    col = lax.broadcasted_iota(jnp.int32, (SGU_LEN, SGU_LEN), 1)
    for g in range(SGU_GROUPS):
        cs = slice(g * SGU_CH, (g + 1) * SGU_CH)
        wg = jnp.where(col <= row, sguw_ref[g], 0.0).astype(BF16)
        bias = sgub_ref[:, g:g + 1]
        for c in range(tm // SGU_LEN):
            rs = slice(c * SGU_LEN, (c + 1) * SGU_LEN)
            mixed = jnp.dot(wg, vn[rs, cs], preferred_element_type=F32) + bias
            y_scr[rs, cs] = u[rs, cs] * mixed
    ysg = y_scr[...]
    ys_ref[...] = (_rms_rows(ysg) * gain_ref[:, 3 * GROUP_W:4 * GROUP_W]).astype(BF16)


def _mixers(proj, conv_w, pool_w, ln_g, ln_b, sgu_w, sgu_bt, gain, S):
    T = proj.shape[0]
    tm = min(TM_MIX, S)
    tiles_per_seq = S // tm
    hb = tm // HALO

    def col(c):
        return pl.BlockSpec((tm, GROUP_W), lambda i: (i, c))

    def halo(c):
        return pl.BlockSpec((HALO, GROUP_W), lambda i: (jnp.maximum(i * hb - 1, 0), c))

    def full(a):
        return pl.BlockSpec(a.shape, lambda i: (0,) * a.ndim)

    out = jax.ShapeDtypeStruct((T, GROUP_W), BF16)
    return pl.pallas_call(
        functools.partial(_mixers_kernel, tm=tm, tiles_per_seq=tiles_per_seq),
        grid=(T // tm,),
        in_specs=[col(COL_CB), col(COL_CC), col(COL_CH), col(COL_PH), col(COL_GU), col(COL_GV),
                  halo(COL_CC), halo(COL_CH), halo(COL_PH),
                  full(conv_w), full(pool_w), full(ln_g), full(ln_b), full(sgu_w), full(sgu_bt), full(gain)],
        out_specs=[pl.BlockSpec((tm, GROUP_W), lambda i: (i, 0))] * 3,
        out_shape=[out, out, out],
        scratch_shapes=[pltpu.VMEM((tm + HALO, GROUP_W), F32), pltpu.VMEM((tm + HALO, GROUP_W), F32),
                        pltpu.VMEM((tm + HALO, POOL_CH), F32), pltpu.VMEM((tm + HALO, POOL_CH), F32),
                        pltpu.VMEM((tm, GROUP_W), F32)],
        compiler_params=_cparams(("parallel",)),
        name="mixers",
    )(proj, proj, proj, proj, proj, proj, proj, proj, proj,
      conv_w, pool_w, ln_g, ln_b, sgu_w, sgu_bt, gain)


def _out_proj_kernel(*refs, with_router):
    if with_router:
        (o_ref, yc_ref, yp_ref, ys_ref, gain_ref, wo_ref, x_ref, g_ref, b_ref, rw_ref,
         x1_ref, x1b_ref, lg_ref) = refs
    else:
        (o_ref, yc_ref, yp_ref, ys_ref, gain_ref, wo_ref, x_ref, g_ref, b_ref,
         x1_ref, x1b_ref) = refs
    ym = (_rms_rows(o_ref[...]) * gain_ref[:, 0:GROUP_W]).astype(BF16)
    mix = jnp.dot(ym, wo_ref[0:GROUP_W, :], preferred_element_type=F32)
    for gi, y_ref in enumerate((yc_ref, yp_ref, ys_ref), start=1):
        mix = mix + jnp.dot(y_ref[...], wo_ref[gi * GROUP_W:(gi + 1) * GROUP_W, :], preferred_element_type=F32)
    x1 = _layer_norm_rows(ALPHA * x_ref[...] + mix, g_ref[...], b_ref[...])
    x1_ref[...] = x1
    x1_hi = x1.astype(BF16)
    x1b_ref[...] = x1_hi
    if with_router:
        x1_lo = (x1 - x1_hi.astype(F32)).astype(BF16)
        lg_ref[...] = (jnp.dot(x1_hi, rw_ref[0], preferred_element_type=F32)
                       + (jnp.dot(x1_hi, rw_ref[1], preferred_element_type=F32)
                          + jnp.dot(x1_lo, rw_ref[0], preferred_element_type=F32)))


def _out_proj(o_mla, yc, yp, ys, gain, wo_all, l, x, ln_g, ln_b, router_hl=None):
    T, D = x.shape
    tm = min(TM_OUT, T)
    with_router = router_hl is not None

    def rows(w):
        return pl.BlockSpec((tm, w), lambda i: (i, 0))

    def full(a):
        return pl.BlockSpec(a.shape, lambda i: (0,) * a.ndim)

    wo_spec = pl.BlockSpec((None,) + wo_all.shape[1:], lambda i: (l, 0, 0))
    in_specs = [rows(GROUP_W)] * 4 + [full(gain), wo_spec, rows(D), full(ln_g), full(ln_b)]
    args = [o_mla, yc, yp, ys, gain, wo_all, x, ln_g, ln_b]
    out_specs = [rows(D), rows(D)]
    out_shape = [jax.ShapeDtypeStruct((T, D), F32), jax.ShapeDtypeStruct((T, D), BF16)]
    if with_router:
        in_specs.append(full(router_hl))
        args.append(router_hl)
        out_specs.append(rows(LANES))
        out_shape.append(jax.ShapeDtypeStruct((T, LANES), F32))
    return pl.pallas_call(
        functools.partial(_out_proj_kernel, with_router=with_router),
        grid=(T // tm,),
        in_specs=in_specs,
        out_specs=out_specs,
        out_shape=out_shape,
        compiler_params=_cparams(("parallel",)),
        name="out_proj",
    )(*args)


def _swiglu(xb, wg, wu):
    hg = jnp.dot(xb, wg, preferred_element_type=F32)
    hu = jnp.dot(xb, wu, preferred_element_type=F32)
    return (hg * jax.nn.sigmoid(hg) * hu).astype(BF16)


def _ffn_kernel(be_ref, nv_ref, *refs, mode, n_cast):
    refs = list(refs)
    xb_ref, wg_ref, wu_ref, wd_ref = refs[:4]
    if mode == "dense":
        cast_in = refs[4:4 + n_cast]
        xres_ref, g_ref, b_ref, o_ref, ob_ref = refs[4 + n_cast:9 + n_cast]
        cast_out = refs[9 + n_cast:9 + 2 * n_cast]
        for src_ref, dst_ref in zip(cast_in, cast_out):
            dst_ref[...] = src_ref[...].astype(dst_ref.dtype)
    else:
        o_ref = refs[-2]
    acc_ref = refs[-1]
    i = pl.program_id(0)
    f = pl.program_id(1)

    @pl.when(f == 0)
    def _():
        acc_ref[...] = jnp.zeros(acc_ref.shape, F32)

    @pl.when(i < nv_ref[0])
    def _():
        h = _swiglu(xb_ref[...], wg_ref[...], wu_ref[...])
        acc_ref[...] += jnp.dot(h, wd_ref[...], preferred_element_type=F32)

    @pl.when(f == pl.num_programs(1) - 1)
    def _():
        if mode == "dense":
            x2 = _layer_norm_rows(ALPHA * xres_ref[...] + acc_ref[...], g_ref[...], b_ref[...])
            o_ref[...] = x2
            ob_ref[...] = x2.astype(BF16)
        else:
            o_ref[...] = acc_ref[...].astype(o_ref.dtype)


def _ffn_weight_specs(tm, D, tf, nf):
    def fidx(i, f, nv):
        return jnp.where(i < nv[0], f, nf - 1)

    return [pl.BlockSpec((tm, D), lambda i, f, be, nv: (i, 0)),
            pl.BlockSpec((None, D, tf), lambda i, f, be, nv: (be[i], 0, fidx(i, f, nv))),
            pl.BlockSpec((None, D, tf), lambda i, f, be, nv: (be[i], 0, fidx(i, f, nv))),
            pl.BlockSpec((None, tf, D), lambda i, f, be, nv: (be[i], fidx(i, f, nv), 0))]


def _ffn_dense(xb, wg, wu, wd, e, xres, ln_g, ln_b, cast=()):
    T, D = xb.shape
    F = wg.shape[2]
    tm = min(TM_FFN, T)
    nb, nf = T // tm, F // TF_FFN
    in_specs = _ffn_weight_specs(tm, D, TF_FFN, nf)
    cast_specs, cast_shapes = [], []
    for arr, part, n_parts in cast:
        rows, cols = arr.shape[0] // n_parts, arr.shape[1]
        steps = nf if rows % (nb * nf) == 0 else CAST_STEPS
        rb = rows // (nb * steps)
        assert rb * nb * steps == rows and rb % 16 == 0, (rows, nb, steps)
        off = part * nb * steps

        def idx(i, f, be, nv, steps=steps, off=off):
            return (off + i * steps + jnp.minimum(f, steps - 1), 0)

        cast_specs.append((pl.BlockSpec((rb, cols), idx),
                           pl.BlockSpec((rb, cols), lambda i, f, be, nv, steps=steps:
                                        (i * steps + jnp.minimum(f, steps - 1), 0))))
        cast_shapes.append(jax.ShapeDtypeStruct((rows, cols), BF16))
    rows_spec = pl.BlockSpec((tm, D), lambda i, f, be, nv: (i, 0))
    vec = pl.BlockSpec((1, D), lambda i, f, be, nv: (0, 0))
    in_specs += [s[0] for s in cast_specs] + [rows_spec, vec, vec]
    out_specs = [rows_spec, rows_spec] + [s[1] for s in cast_specs]
    out_shape = [jax.ShapeDtypeStruct((T, D), F32), jax.ShapeDtypeStruct((T, D), BF16)] + cast_shapes
    return pl.pallas_call(
        functools.partial(_ffn_kernel, mode="dense", n_cast=len(cast)),
        grid_spec=pltpu.PrefetchScalarGridSpec(
            num_scalar_prefetch=2, grid=(nb, nf), in_specs=in_specs, out_specs=out_specs,
            scratch_shapes=[pltpu.VMEM((tm, D), F32)]),
        out_shape=out_shape,
        compiler_params=_cparams(("parallel", "arbitrary")),
        name="ffn_dense",
    )(jnp.full((nb,), e, jnp.int32), jnp.full((1,), nb, jnp.int32), xb, wg, wu, wd,
      *[c[0] for c in cast], xres, ln_g, ln_b)


def _ffn_moe(x_rows_b, wg, wu, wd, block_e, n_valid, y_buf, blk0, total_rows):
    Rc, D = x_rows_b.shape
    F = wg.shape[2]
    tm = min(TM_FFN, Rc)
    nf = F // TF_FFN
    in_specs = _ffn_weight_specs(tm, D, TF_FFN, nf)
    args = [block_e, n_valid, x_rows_b, wg, wu, wd]
    aliases = {}
    if y_buf is not None:
        in_specs.append(pl.BlockSpec(memory_space=pl.ANY))
        args.append(y_buf)
        aliases = {len(args) - 1: 0}
    return pl.pallas_call(
        functools.partial(_ffn_kernel_moe_entry, has_buf=y_buf is not None),
        grid_spec=pltpu.PrefetchScalarGridSpec(
            num_scalar_prefetch=2, grid=(Rc // tm, nf), in_specs=in_specs,
            out_specs=pl.BlockSpec((tm, D), lambda i, f, be, nv: (blk0 + i, 0)),
            scratch_shapes=[pltpu.VMEM((tm, D), F32)]),
        out_shape=jax.ShapeDtypeStruct((total_rows, D), BF16),
        input_output_aliases=aliases,
        compiler_params=_cparams(("parallel", "arbitrary")),
        name="ffn_moe",
    )(*args)


def _ffn_kernel_moe_entry(be_ref, nv_ref, *refs, has_buf):
    refs = list(refs)
    if has_buf:
        del refs[4]
    _ffn_kernel(be_ref, nv_ref, *refs, mode="moe", n_cast=0)


def _combine_kernel(x_ref, y0_ref, y1_ref, g0_ref, g1_ref, g_ref, b_ref, o_ref, ob_ref):
    ffn = y0_ref[...].astype(F32) * g0_ref[...] + y1_ref[...].astype(F32) * g1_ref[...]
    x2 = _layer_norm_rows(ALPHA * x_ref[...] + ffn, g_ref[...], b_ref[...])
    o_ref[...] = x2
    ob_ref[...] = x2.astype(BF16)


def _combine(x, y0, y1, g0, g1, ln_g, ln_b):
    T, D = x.shape
    tm = min(TM_OUT, T)
    rows = pl.BlockSpec((tm, D), lambda i: (i, 0))
    gate = pl.BlockSpec((tm, 1), lambda i: (i, 0))
    vec = pl.BlockSpec((1, D), lambda i: (0, 0))
    return pl.pallas_call(
        _combine_kernel,
        grid=(T // tm,),
        in_specs=[rows, rows, rows, gate, gate, vec, vec],
        out_specs=[rows, rows],
        out_shape=[jax.ShapeDtypeStruct((T, D), F32), jax.ShapeDtypeStruct((T, D), BF16)],
        compiler_params=_cparams(("parallel",)),
        name="moe_combine",
    )(x, y0, y1, g0, g1, ln_g, ln_b)


def _pad_w_in(w_in):
    L, D, _ = w_in.shape
    q_dim = MLA_HEADS * (QK_NOPE + QK_ROPE)
    wq = w_in[:, :, :q_dim].reshape(L, D, MLA_HEADS, QK_NOPE + QK_ROPE)
    kr0 = q_dim + KV_RANK
    w_kr = w_in[:, :, kr0:kr0 + QK_ROPE]
    pad = jnp.zeros((L, D, MLA_HEADS, HEAD_W - QK_NOPE - QK_ROPE), w_in.dtype).at[:, :, 0, :].set(w_kr)
    q_part = jnp.concatenate([wq, pad], axis=-1).reshape(L, D, Q_COLS)
    rest = jnp.concatenate([w_in[:, :, q_dim:kr0], w_in[:, :, kr0 + QK_ROPE:]], axis=-1)
    return jnp.concatenate([q_part, rest], axis=-1).astype(BF16)


def _rope_tables(positions):
    inv_freq = ROPE_THETA ** (-jnp.arange(0, QK_ROPE, 2, dtype=F32) / QK_ROPE)
    ang = positions.astype(F32).reshape(-1)[:, None] * inv_freq
    cos, sin = jnp.cos(ang), jnp.sin(ang)
    cos4 = jnp.concatenate([cos, cos, cos, cos], axis=-1)
    sin4 = jnp.concatenate([-sin, sin, -sin, sin], axis=-1)
    return cos4, sin4


def _route(logits, tm):
    T = logits.shape[0]
    A = T * TOP_K
    top_logit, top_e = lax.top_k(logits, TOP_K)
    gates = jax.nn.softmax(top_logit, axis=-1)
    flat_e = top_e.reshape(A)
    onehot = (flat_e[:, None] == jnp.arange(N_EXPERTS, dtype=flat_e.dtype)[None, :]).astype(jnp.int32)
    csum = jnp.cumsum(onehot, axis=0)
    counts = csum[-1]
    rank = jnp.take_along_axis(csum, flat_e[:, None], axis=1)[:, 0] - 1
    padded = (counts + tm - 1) // tm * tm
    pad_ends = jnp.cumsum(padded)
    pad_starts = pad_ends - padded
    dest = (pad_starts[flat_e] + rank).astype(jnp.int32)
    n_blocks = -(-(A + N_EXPERTS * (tm - 1)) // tm)
    n_pad = n_blocks * tm
    tok = (jnp.arange(A, dtype=jnp.int32) // TOP_K)
    slot_tok = jnp.zeros((n_pad,), jnp.int32).at[dest].set(tok)
    block_start = jnp.arange(n_blocks, dtype=jnp.int32) * tm
    block_e = jnp.minimum(jnp.searchsorted(pad_ends, block_start, side='right'), N_EXPERTS - 1).astype(jnp.int32)
    n_valid = (pad_ends[-1] // tm).astype(jnp.int32).reshape(1)
    return dest.reshape(T, TOP_K), gates, slot_tok, block_e, n_valid


def kernel(x, positions, w_in, kv_norm_g, w_ukv, conv_w, pool_w, sgu_ln_g, sgu_ln_b, sgu_w, sgu_b, mix_gain, w_o, ln1_g, ln1_b, ffn_wg, ffn_wu, ffn_wd, router_w, exp_wg, exp_wu, exp_wd, ln2_g, ln2_b):
    B, S, D = x.shape
    T = B * S
    L = w_in.shape[0]
    cos4, sin4 = _rope_tables(positions)
    w_in_b = _pad_w_in(w_in)
    w_ukv_b = w_ukv.astype(BF16)
    w_o_b = w_o.astype(BF16)
    pool_w_b = pool_w.astype(BF16)
    sgu_bt = jnp.swapaxes(sgu_b, 1, 2)
    router_pad = jnp.pad(router_w, ((0, 0), (0, 0), (0, LANES - N_EXPERTS)))
    router_hi = router_pad.astype(BF16)
    router_hl = jnp.stack([router_hi, (router_pad - router_hi.astype(F32)).astype(BF16)], axis=1)
    ffn_w = (ffn_wg.astype(BF16), ffn_wu.astype(BF16), ffn_wd.astype(BF16))
    n_moe = exp_wg.shape[0]
    exp_f32 = tuple(w.reshape(-1, w.shape[-1]) for w in (exp_wg, exp_wu, exp_wd))
    exp_w = None

    xf = x.reshape(T, D)
    xb = xf.astype(BF16)
    tm_moe = min(TM_FFN, T)
    for l in range(L):
        proj = _in_proj(xb, w_in_b, l)
        q_cat, k_cat, v_t = _mla_prep(proj, cos4, sin4, kv_norm_g[l][None, :], w_ukv_b, l)
        o_mla = _attention(q_cat, k_cat, v_t, B, S)
        yc, yp, ys = _mixers(proj, conv_w[l], pool_w_b[l], sgu_ln_g[l][None, :], sgu_ln_b[l][None, :],
                             sgu_w[l], sgu_bt[l], mix_gain[l][None, :], S)
        ln1 = (ln1_g[l][None, :], ln1_b[l][None, :])
        ln2 = (ln2_g[l][None, :], ln2_b[l][None, :])
        e = l // 2
        if l % 2 == 0:
            x1, x1b = _out_proj(o_mla, yc, yp, ys, mix_gain[l][None, :], w_o_b, l, xf, *ln1)
            cast = [(w, e, n_moe) for w in exp_f32] if e < n_moe else []
            xf, xb, *cast_out = _ffn_dense(x1b, *ffn_w, e, x1, *ln2, cast=cast)
            if cast_out:
                exp_w = tuple(c.reshape((N_EXPERTS, -1, c.shape[-1])) for c in cast_out)
        else:
            x1, x1b, logits = _out_proj(o_mla, yc, yp, ys, mix_gain[l][None, :], w_o_b, l, xf, *ln1,
                                        router_hl=router_hl[e])
            dest, gates, slot_tok, block_e, n_valid = _route(logits[:, :N_EXPERTS], tm_moe)
            n_blocks = block_e.shape[0]
            chunks = max(c for c in (MOE_CHUNKS, 2, 1) if n_blocks % c == 0)
            cb = n_blocks // chunks
            y_rows = None
            for c in range(chunks):
                x_rows = x1b.at[slot_tok[c * cb * tm_moe:(c + 1) * cb * tm_moe]].get(mode="promise_in_bounds")
                y_rows = _ffn_moe(x_rows, *exp_w, block_e[c * cb:(c + 1) * cb],
                                  jnp.clip(n_valid - c * cb, 0, cb), y_rows, c * cb, n_blocks * tm_moe)
            y0 = y_rows.at[dest[:, 0]].get(mode="promise_in_bounds")
            y1 = y_rows.at[dest[:, 1]].get(mode="promise_in_bounds")
            xf, xb = _combine(x1, y0, y1, gates[:, 0:1], gates[:, 1:2], *ln2)
    return xf.reshape(B, S, D)
```

```python
import functools
import math

import jax
import jax.numpy as jnp
from jax import lax
from jax.experimental import pallas as pl
from jax.experimental.pallas import tpu as pltpu

F32 = jnp.float32
BF16 = jnp.bfloat16

D_MODEL = 2048
CHUNK = 64
GROUP_W = 512
N_GROUPS = 4
MLA_HEADS = 4
QK_NOPE = 128
QK_ROPE = 64
V_HEAD = 128
KV_RANK = 512
ROPE_THETA = 10000.0
CONV_W = 3
POOL_WINDOWS = (2, 4, 8, 16)
POOL_CH = 128
SGU_LEN = 128
SGU_GROUPS = 4
SGU_CH = 128
D_FF = 5632
N_EXPERTS = 8
TOP_K = 2
DEPTH = 4
ALPHA = (2.0 * DEPTH) ** 0.25
LN_EPS = 1e-5
RMS_EPS = 1e-6

LANES = 128
HEAD_W = 2 * LANES
VMEM_LIMIT = 56 * 1024 * 1024

Q_COLS = MLA_HEADS * HEAD_W
COL_Q, COL_CKV, COL_CB, COL_CC, COL_CH, COL_PH, COL_GU, COL_GV = 0, 2, 3, 4, 5, 6, 7, 8
IN_COLS_PAD = Q_COLS + 7 * GROUP_W

HALO = 32

TM_PROJ, TN_PROJ = 1024, 2304
TB_ATT = 1024
ATT_HEADS = 2
ATT_BLOCKS_PER_TRIP = 2
TM_PREP = TB_ATT
MASK_BIG = 2.0 ** 100
TM_MIX = 512
TM_OUT, OUT_PARTS = 512, 2
TM_LN = 256
TM_FFN, TF_FFN = 512, 512
CAST_STEPS = 8
MOE_F_SPLIT = 2
MOE_CHUNKS = 4


def _cparams(sem):
    return pltpu.CompilerParams(dimension_semantics=sem, vmem_limit_bytes=VMEM_LIMIT)


def _layer_norm_rows(z, g, b):
    mu = jnp.mean(z, axis=-1, keepdims=True)
    zc = z - mu
    var = jnp.mean(zc * zc, axis=-1, keepdims=True)
    return zc * lax.rsqrt(var + LN_EPS) * g + b


def _rms_rows(y):
    return y * lax.rsqrt(jnp.mean(y * y, axis=-1, keepdims=True) + RMS_EPS)


def _matmul_kernel(x_ref, w_ref, o_ref):
    o_ref[...] = jnp.dot(x_ref[...], w_ref[...], preferred_element_type=F32).astype(o_ref.dtype)


def _in_proj(xb, w_all, l):
    T, K = xb.shape
    N = w_all.shape[2]
    tm = min(TM_PROJ, T)
    return pl.pallas_call(
        _matmul_kernel,
        grid=(T // tm, N // TN_PROJ),
        in_specs=[pl.BlockSpec((tm, K), lambda i, j: (i, 0)),
                  pl.BlockSpec((None, K, TN_PROJ), lambda i, j: (l, 0, j))],
        out_specs=pl.BlockSpec((tm, TN_PROJ), lambda i, j: (i, j)),
        out_shape=jax.ShapeDtypeStruct((T, N), BF16),
        compiler_params=_cparams(("parallel", "arbitrary")),
        name="in_proj",
    )(xb, w_all)


def _mla_prep_kernel(q_ref, ckv_ref, cos_ref, sin_ref, g_ref, wukv_ref, qo_ref, ko_ref, vt_ref, *, qscale):
    cos = cos_ref[...]
    sin = sin_ref[...]
    lane = lax.broadcasted_iota(jnp.int32, cos.shape, 1)
    row_chunk = lax.broadcasted_iota(jnp.int32, cos.shape, 0) // CHUNK
    first_half = (lane & (QK_ROPE // 2)) == 0
    low = lane < QK_ROPE

    def rope(r):
        partner = jnp.where(first_half, pltpu.roll(r, LANES - QK_ROPE // 2, axis=1), pltpu.roll(r, QK_ROPE // 2, axis=1))
        return r * cos + partner * sin

    k_rope = None
    for h in range(MLA_HEADS):
        c0 = h * HEAD_W
        qo_ref[:, c0:c0 + LANES] = (q_ref[:, c0:c0 + LANES].astype(F32) * qscale).astype(BF16)
        rr = rope(q_ref[:, c0 + LANES:c0 + HEAD_W].astype(F32))
        qo_ref[:, c0 + LANES:c0 + HEAD_W] = jnp.where(low, rr * qscale, 0.0).astype(BF16)
        if h == 0:
            stair = jnp.where(row_chunk > lane - QK_ROPE, -MASK_BIG, 0.0)
            k_rope = jnp.where(low, pltpu.roll(rr, QK_ROPE, axis=1), stair).astype(BF16)

    c = ckv_ref[...].astype(F32)
    cn = (_rms_rows(c) * g_ref[...]).astype(BF16)
    kv = jnp.dot(cn, wukv_ref[...], preferred_element_type=F32)
    for h in range(MLA_HEADS):
        c0 = h * HEAD_W
        ko_ref[:, c0:c0 + LANES] = kv[:, c0:c0 + LANES].astype(BF16)
        ko_ref[:, c0 + LANES:c0 + HEAD_W] = k_rope
        vt_ref[h] = kv[:, c0 + LANES:c0 + HEAD_W].T.astype(BF16)


def _mla_prep(proj, cos4, sin4, kv_g, w_ukv_all, l):
    T = proj.shape[0]
    tm = min(TM_PREP, T)
    qscale = math.log2(math.e) / math.sqrt(QK_NOPE + QK_ROPE)
    return pl.pallas_call(
        functools.partial(_mla_prep_kernel, qscale=qscale),
        grid=(T // tm,),
        in_specs=[pl.BlockSpec((tm, Q_COLS), lambda i: (i, 0)),
                  pl.BlockSpec((tm, GROUP_W), lambda i: (i, COL_CKV)),
                  pl.BlockSpec((tm, LANES), lambda i: (i, 0)),
                  pl.BlockSpec((tm, LANES), lambda i: (i, 0)),
                  pl.BlockSpec((1, KV_RANK), lambda i: (0, 0)),
                  pl.BlockSpec((None,) + w_ukv_all.shape[1:], lambda i: (l, 0, 0))],
        out_specs=[pl.BlockSpec((tm, Q_COLS), lambda i: (i, 0)),
                   pl.BlockSpec((tm, Q_COLS), lambda i: (i, 0)),
                   pl.BlockSpec((None, MLA_HEADS, V_HEAD, tm), lambda i: (i, 0, 0, 0))],
        out_shape=[jax.ShapeDtypeStruct((T, Q_COLS), BF16),
                   jax.ShapeDtypeStruct((T, Q_COLS), BF16),
                   jax.ShapeDtypeStruct((T // tm, MLA_HEADS, V_HEAD, tm), BF16)],
        compiler_params=_cparams(("parallel",)),
        name="mla_prep",
    )(proj, proj, cos4, sin4, kv_g, w_ukv_all)


def _attention_kernel(q_ref, k_ref, vt_ref, o_ref, q2_ref, s0_ref, s1_ref, m_ref, l_ref, acc_ref, *, tb):
    i = pl.program_id(2)
    hp = ATT_HEADS
    q = q_ref[...]
    lane = lax.broadcasted_iota(jnp.int32, q.shape, 1) % HEAD_W
    row_chunk = lax.broadcasted_iota(jnp.int32, q.shape, 0) // CHUNK
    q2_ref[0] = q
    q2_ref[1] = jnp.where(lane == QK_NOPE + QK_ROPE + row_chunk, jnp.ones_like(q), q)
    m_ref[...] = jnp.full(m_ref.shape, -jnp.inf, F32)
    l_ref[...] = jnp.zeros(l_ref.shape, F32)
    acc_ref[...] = jnp.zeros(acc_ref.shape, F32)

    def scores(j, s_ref):
        r0 = pl.multiple_of(j * tb, tb)
        sel = (j == i).astype(jnp.int32)
        for h in range(hp):
            cs = slice(h * HEAD_W, (h + 1) * HEAD_W)
            s_ref[h] = lax.dot_general(k_ref[pl.ds(r0, tb), cs], q2_ref[sel, :, cs], (((1,), (1,)), ((), ())),
                                       preferred_element_type=F32)

    def softmax_pv(j, s_ref):
        for h in range(hp):
            s = s_ref[h]
            m_old = m_ref[h]
            m_new = jnp.maximum(m_old, jnp.max(s, axis=0, keepdims=True))
            p = jnp.exp2(s - m_new)
            a = jnp.exp2(m_old - m_new)
            l_ref[h] = a * l_ref[h] + jnp.sum(p, axis=0, keepdims=True)
            acc_ref[h] = a * acc_ref[h] + jnp.dot(vt_ref[j, h], p.astype(BF16), preferred_element_type=F32)
            m_ref[h] = m_new

    scores(0, s0_ref)
    bufs = (s0_ref, s1_ref)
    U = ATT_BLOCKS_PER_TRIP

    def trip(g, carry):
        j = U * g
        for u in range(U):
            scores(j + u + 1, bufs[(u + 1) % 2])
            softmax_pv(j + u, bufs[u % 2])
        return carry

    trips = i // U
    lax.fori_loop(0, trips, trip, 0)
    base = U * trips
    for u in range(U):
        j = base + u

        @pl.when(j < i)
        def _():
            scores(j + 1, bufs[(u + 1) % 2])
            softmax_pv(j, bufs[u % 2])

        @pl.when(j == i)
        def _():
            softmax_pv(j, bufs[u % 2])

    for h in range(hp):
        o_ref[:, h * V_HEAD:(h + 1) * V_HEAD] = (acc_ref[h] / l_ref[h]).T


def _attention(q_cat, k_cat, v_t, B, S):
    T = B * S
    tb = v_t.shape[-1]
    nq = S // tb
    hp = ATT_HEADS
    resident = dict(pipeline_mode=pl.Buffered(1))
    return pl.pallas_call(
        functools.partial(_attention_kernel, tb=tb),
        grid=(B, MLA_HEADS // hp, nq),
        in_specs=[pl.BlockSpec((tb, hp * HEAD_W), lambda b, h, i: (b * nq + i, h)),
                  pl.BlockSpec((S, hp * HEAD_W), lambda b, h, i: (b, h), **resident),
                  pl.BlockSpec((nq, hp, V_HEAD, tb), lambda b, h, i: (b, h, 0, 0), **resident)],
        out_specs=pl.BlockSpec((tb, hp * V_HEAD), lambda b, h, i: (b * nq + i, h)),
        out_shape=jax.ShapeDtypeStruct((T, MLA_HEADS * V_HEAD), F32),
        scratch_shapes=[pltpu.VMEM((2, tb, hp * HEAD_W), BF16),
                        pltpu.VMEM((hp, tb, tb), F32), pltpu.VMEM((hp, tb, tb), F32),
                        pltpu.VMEM((hp, 1, tb), F32), pltpu.VMEM((hp, 1, tb), F32),
                        pltpu.VMEM((hp, V_HEAD, tb), F32)],
        compiler_params=_cparams(("parallel", "parallel", "arbitrary")),
        name="attention",
    )(q_cat, k_cat, v_t)


def _gelu(x):
    return 0.5 * x * (1.0 + lax.erf(x * (1.0 / math.sqrt(2.0))))


def _mixers_kernel(cb_ref, cc_ref, ch_ref, ph_ref, gu_ref, gv_ref, hcc_ref, hch_ref, hph_ref,
                   convw_ref, poolw_ref, lng_ref, lnb_ref, sguw_ref, sgub_ref, gain_ref,
                   yc_ref, yp_ref, ys_ref,
                   g_scr, e_scr, a_scr, b_scr, y_scr, *, tm, tiles_per_seq):
    i = pl.program_id(0)
    t0 = (i % tiles_per_seq) * tm
    keep = jnp.where(t0 == 0, 0.0, 1.0).astype(F32)

    g_scr[0:HALO, :] = hcc_ref[...].astype(F32) * hch_ref[...].astype(F32) * keep
    g_scr[HALO:, :] = cc_ref[...].astype(F32) * ch_ref[...].astype(F32)
    conv = convw_ref[CONV_W - 1:CONV_W, :] * g_scr[HALO:HALO + tm, :]
    for j in range(CONV_W - 1):
        off = HALO - (CONV_W - 1) + j
        conv = conv + convw_ref[j:j + 1, :] * g_scr[off:off + tm, :]
    yc = cb_ref[...].astype(F32) * conv
    yc_ref[...] = (_rms_rows(yc) * gain_ref[:, GROUP_W:2 * GROUP_W]).astype(BF16)

    e_scr[0:HALO, :] = hph_ref[...].astype(F32) * keep
    e_scr[HALO:, :] = ph_ref[...].astype(F32)
    n = tm + HALO
    pos = (t0 + 1 + lax.broadcasted_iota(jnp.int32, (tm, 1), 0)).astype(F32)
    for g, w in enumerate(POOL_WINDOWS):
        cs = slice(g * POOL_CH, (g + 1) * POOL_CH)
        levels = g + 1
        src = None
        bufs = (a_scr, b_scr)
        total = None
        for lev in range(1, levels + 1):
            shift = 1 << (lev - 1)
            lo = 8 * lev if lev < levels else HALO
            if src is None:
                cur = e_scr[lo:n, cs] + e_scr[lo - shift:n - shift, cs]
            else:
                cur = src[lo:n, :] + src[lo - shift:n - shift, :]
            if lev < levels:
                dst = bufs[(lev - 1) % 2]
                dst[lo:n, :] = cur
                src = dst
            else:
                total = cur
        inv = 1.0 / jnp.minimum(pos, float(w))
        pooled = total * inv - e_scr[HALO:, cs]
        y_scr[:, cs] = jnp.dot(pooled.astype(BF16), poolw_ref[g], preferred_element_type=F32)
    yp = y_scr[...]
    yp_ref[...] = (_rms_rows(yp) * gain_ref[:, 2 * GROUP_W:3 * GROUP_W]).astype(BF16)

    u = _gelu(gu_ref[...].astype(F32))
    vn = _layer_norm_rows(_gelu(gv_ref[...].astype(F32)), lng_ref[...], lnb_ref[...]).astype(BF16)
    row = lax.broadcasted_iota(jnp.int32, (SGU_LEN, SGU_LEN), 0)
    col = lax.broadcasted_iota(jnp.int32, (SGU_LEN, SGU_LEN), 1)
    for g in range(SGU_GROUPS):
        cs = slice(g * SGU_CH, (g + 1) * SGU_CH)
        wg = jnp.where(col <= row, sguw_ref[g], 0.0).astype(BF16)
        bias = sgub_ref[:, g:g + 1]
        for c in range(tm // SGU_LEN):
            rs = slice(c * SGU_LEN, (c + 1) * SGU_LEN)
            mixed = jnp.dot(wg, vn[rs, cs], preferred_element_type=F32) + bias
            y_scr[rs, cs] = u[rs, cs] * mixed
    ysg = y_scr[...]
    ys_ref[...] = (_rms_rows(ysg) * gain_ref[:, 3 * GROUP_W:4 * GROUP_W]).astype(BF16)


def _mixers(proj, conv_w, pool_w, ln_g, ln_b, sgu_w, sgu_bt, gain, S):
    T = proj.shape[0]
    tm = min(TM_MIX, S)
    tiles_per_seq = S // tm
    hb = tm // HALO

    def col(c):
        return pl.BlockSpec((tm, GROUP_W), lambda i: (i, c))

    def halo(c):
        return pl.BlockSpec((HALO, GROUP_W), lambda i: (jnp.maximum(i * hb - 1, 0), c))

    def full(a):
        return pl.BlockSpec(a.shape, lambda i: (0,) * a.ndim)

    out = jax.ShapeDtypeStruct((T, GROUP_W), BF16)
    return pl.pallas_call(
        functools.partial(_mixers_kernel, tm=tm, tiles_per_seq=tiles_per_seq),
        grid=(T // tm,),
        in_specs=[col(COL_CB), col(COL_CC), col(COL_CH), col(COL_PH), col(COL_GU), col(COL_GV),
                  halo(COL_CC), halo(COL_CH), halo(COL_PH),
                  full(conv_w), full(pool_w), full(ln_g), full(ln_b), full(sgu_w), full(sgu_bt), full(gain)],
        out_specs=[pl.BlockSpec((tm, GROUP_W), lambda i: (i, 0))] * 3,
        out_shape=[out, out, out],
        scratch_shapes=[pltpu.VMEM((tm + HALO, GROUP_W), F32), pltpu.VMEM((tm + HALO, GROUP_W), F32),
                        pltpu.VMEM((tm + HALO, POOL_CH), F32), pltpu.VMEM((tm + HALO, POOL_CH), F32),
                        pltpu.VMEM((tm, GROUP_W), F32)],
        compiler_params=_cparams(("parallel",)),
        name="mixers",
    )(proj, proj, proj, proj, proj, proj, proj, proj, proj,
      conv_w, pool_w, ln_g, ln_b, sgu_w, sgu_bt, gain)


def _out_proj_kernel(*refs, with_router):
    if with_router:
        (o_ref, yc_ref, yp_ref, ys_ref, gain_ref, wo_ref, x_ref, g_ref, b_ref, rw_ref,
         x1_ref, x1b_ref, lg_ref) = refs
    else:
        (o_ref, yc_ref, yp_ref, ys_ref, gain_ref, wo_ref, x_ref, g_ref, b_ref,
         x1_ref, x1b_ref) = refs
    hr = o_ref.shape[0] // OUT_PARTS
    for part in range(OUT_PARTS):
        rs = slice(part * hr, (part + 1) * hr)
        ym = (_rms_rows(o_ref[rs, :]) * gain_ref[:, 0:GROUP_W]).astype(BF16)
        mix = jnp.dot(ym, wo_ref[0:GROUP_W, :], preferred_element_type=F32)
        for gi, y_ref in enumerate((yc_ref, yp_ref, ys_ref), start=1):
            mix = mix + jnp.dot(y_ref[rs, :], wo_ref[gi * GROUP_W:(gi + 1) * GROUP_W, :], preferred_element_type=F32)
        x1 = _layer_norm_rows(ALPHA * x_ref[rs, :] + mix, g_ref[...], b_ref[...])
        x1_ref[rs, :] = x1
        x1_hi = x1.astype(BF16)
        x1b_ref[rs, :] = x1_hi
        if with_router:
            x1_lo = (x1 - x1_hi.astype(F32)).astype(BF16)
            lg_ref[rs, :] = (jnp.dot(x1_hi, rw_ref[0], preferred_element_type=F32)
                             + (jnp.dot(x1_hi, rw_ref[1], preferred_element_type=F32)
                                + jnp.dot(x1_lo, rw_ref[0], preferred_element_type=F32)))


def _out_proj(o_mla, yc, yp, ys, gain, wo_all, l, x, ln_g, ln_b, router_hl=None):
    T, D = x.shape
    tm = min(TM_OUT, T)
    with_router = router_hl is not None

    def rows(w):
        return pl.BlockSpec((tm, w), lambda i: (i, 0))

    def full(a):
        return pl.BlockSpec(a.shape, lambda i: (0,) * a.ndim)

    wo_spec = pl.BlockSpec((None,) + wo_all.shape[1:], lambda i: (l, 0, 0))
    in_specs = [rows(GROUP_W)] * 4 + [full(gain), wo_spec, rows(D), full(ln_g), full(ln_b)]
    args = [o_mla, yc, yp, ys, gain, wo_all, x, ln_g, ln_b]
    out_specs = [rows(D), rows(D)]
    out_shape = [jax.ShapeDtypeStruct((T, D), F32), jax.ShapeDtypeStruct((T, D), BF16)]
    if with_router:
        in_specs.append(full(router_hl))
        args.append(router_hl)
        out_specs.append(rows(LANES))
        out_shape.append(jax.ShapeDtypeStruct((T, LANES), F32))
    return pl.pallas_call(
        functools.partial(_out_proj_kernel, with_router=with_router),
        grid=(T // tm,),
        in_specs=in_specs,
        out_specs=out_specs,
        out_shape=out_shape,
        compiler_params=_cparams(("parallel",)),
        name="out_proj",
    )(*args)


def _swiglu(xb, wg, wu):
    hg = jnp.dot(xb, wg, preferred_element_type=F32)
    hu = jnp.dot(xb, wu, preferred_element_type=F32)
    return (hg * jax.nn.sigmoid(hg) * hu).astype(BF16)


def _ffn_dense_kernel(*refs, n_cast):
    xb_ref, wg_ref, wu_ref, wd_ref = refs[:4]
    cast_in = refs[4:4 + n_cast]
    xres_ref, g_ref, b_ref, o_ref, ob_ref = refs[4 + n_cast:9 + n_cast]
    cast_out = refs[9 + n_cast:9 + 2 * n_cast]
    acc_ref = refs[-1]
    f = pl.program_id(1)
    for src_ref, dst_ref in zip(cast_in, cast_out):
        dst_ref[...] = src_ref[...].astype(dst_ref.dtype)

    @pl.when(f == 0)
    def _():
        acc_ref[...] = jnp.zeros(acc_ref.shape, F32)

    h = _swiglu(xb_ref[...], wg_ref[...], wu_ref[...])
    acc_ref[...] += jnp.dot(h, wd_ref[...], preferred_element_type=F32)

    @pl.when(f == pl.num_programs(1) - 1)
    def _():
        x2 = _layer_norm_rows(ALPHA * xres_ref[...] + acc_ref[...], g_ref[...], b_ref[...])
        o_ref[...] = x2
        ob_ref[...] = x2.astype(BF16)


def _ffn_dense(xb, wg, wu, wd, e, xres, ln_g, ln_b, cast=()):
    T, D = xb.shape
    F = wg.shape[2]
    tm = min(TM_FFN, T)
    nb, nf = T // tm, F // TF_FFN
    in_specs = [pl.BlockSpec((tm, D), lambda i, f: (i, 0)),
                pl.BlockSpec((None, D, TF_FFN), lambda i, f: (e, 0, f)),
                pl.BlockSpec((None, D, TF_FFN), lambda i, f: (e, 0, f)),
                pl.BlockSpec((None, TF_FFN, D), lambda i, f: (e, f, 0))]
    cast_specs, cast_shapes = [], []
    for arr, part, n_parts in cast:
        rows, cols = arr.shape[0] // n_parts, arr.shape[1]
        steps = nf if rows % (nb * nf) == 0 else CAST_STEPS
        rb = rows // (nb * steps)
        assert rb * nb * steps == rows and rb % 16 == 0, (rows, nb, steps)
        off = part * nb * steps

        def idx(i, f, steps=steps, off=off):
            return (off + i * steps + jnp.minimum(f, steps - 1), 0)

        cast_specs.append((pl.BlockSpec((rb, cols), idx),
                           pl.BlockSpec((rb, cols), lambda i, f, steps=steps:
                                        (i * steps + jnp.minimum(f, steps - 1), 0))))
        cast_shapes.append(jax.ShapeDtypeStruct((rows, cols), BF16))
    rows_spec = pl.BlockSpec((tm, D), lambda i, f: (i, 0))
    vec = pl.BlockSpec((1, D), lambda i, f: (0, 0))
    in_specs += [s[0] for s in cast_specs] + [rows_spec, vec, vec]
    out_specs = [rows_spec, rows_spec] + [s[1] for s in cast_specs]
    out_shape = [jax.ShapeDtypeStruct((T, D), F32), jax.ShapeDtypeStruct((T, D), BF16)] + cast_shapes
    return pl.pallas_call(
        functools.partial(_ffn_dense_kernel, n_cast=len(cast)),
        grid=(nb, nf),
        in_specs=in_specs,
        out_specs=out_specs,
        out_shape=out_shape,
        scratch_shapes=[pltpu.VMEM((tm, D), F32)],
        compiler_params=_cparams(("parallel", "arbitrary")),
        name="ffn_dense",
    )(xb, wg, wu, wd, *[c[0] for c in cast], xres, ln_g, ln_b)


def _moe_up_kernel(be_ref, nv_ref, x_ref, wg_ref, wu_ref, h_ref):
    i = pl.program_id(1)

    @pl.when(i < nv_ref[0])
    def _():
        h_ref[...] = _swiglu(x_ref[...], wg_ref[...], wu_ref[...])

    @pl.when(i >= nv_ref[0])
    def _():
        h_ref[...] = jnp.zeros(h_ref.shape, h_ref.dtype)


def _moe_down_kernel(be_ref, nv_ref, *refs):
    h_ref, wd_ref, o_ref = refs[0], refs[1], refs[-1]
    i = pl.program_id(0)

    @pl.when(i < nv_ref[0])
    def _():
        o_ref[...] = jnp.dot(h_ref[...], wd_ref[...], preferred_element_type=F32).astype(o_ref.dtype)

    @pl.when(i >= nv_ref[0])
    def _():
        o_ref[...] = jnp.zeros(o_ref.shape, o_ref.dtype)


def _ffn_moe(x_rows_b, wg, wu, wd, block_e, n_valid, y_buf, blk0, total_rows):
    Rc, D = x_rows_b.shape
    F = wg.shape[2]
    tm = min(TM_FFN, Rc)
    nb = Rc // tm
    fh = F // MOE_F_SPLIT
    resident = dict(pipeline_mode=pl.Buffered(1))
    h = pl.pallas_call(
        _moe_up_kernel,
        grid_spec=pltpu.PrefetchScalarGridSpec(
            num_scalar_prefetch=2,
            grid=(MOE_F_SPLIT, nb),
            in_specs=[pl.BlockSpec((tm, D), lambda f, i, be, nv: (i, 0)),
                      pl.BlockSpec((None, D, fh), lambda f, i, be, nv: (be[i], 0, f), **resident),
                      pl.BlockSpec((None, D, fh), lambda f, i, be, nv: (be[i], 0, f), **resident)],
            out_specs=pl.BlockSpec((tm, fh), lambda f, i, be, nv: (i, f))),
        out_shape=jax.ShapeDtypeStruct((Rc, F), BF16),
        compiler_params=_cparams(("arbitrary", "arbitrary")),
        name="moe_up",
    )(block_e, n_valid, x_rows_b, wg, wu)
    in_specs = [pl.BlockSpec((tm, F), lambda i, be, nv: (i, 0)),
                pl.BlockSpec((None, F, D), lambda i, be, nv: (be[i], 0, 0), **resident)]
    args = [block_e, n_valid, h, wd]
    aliases = {}
    if y_buf is not None:
        in_specs.append(pl.BlockSpec(memory_space=pl.ANY))
        args.append(y_buf)
        aliases = {len(args) - 1: 0}
    return pl.pallas_call(
        _moe_down_kernel,
        grid_spec=pltpu.PrefetchScalarGridSpec(
            num_scalar_prefetch=2,
            grid=(nb,),
            in_specs=in_specs,
            out_specs=pl.BlockSpec((tm, D), lambda i, be, nv: (blk0 + i, 0))),
        out_shape=jax.ShapeDtypeStruct((total_rows, D), BF16),
        input_output_aliases=aliases,
        compiler_params=_cparams(("arbitrary",)),
        name="moe_down",
    )(*args)


def _combine_kernel(x_ref, y0_ref, y1_ref, g0_ref, g1_ref, g_ref, b_ref, o_ref, ob_ref):
    ffn = y0_ref[...].astype(F32) * g0_ref[...] + y1_ref[...].astype(F32) * g1_ref[...]
    x2 = _layer_norm_rows(ALPHA * x_ref[...] + ffn, g_ref[...], b_ref[...])
    o_ref[...] = x2
    ob_ref[...] = x2.astype(BF16)


def _combine(x, y0, y1, g0, g1, ln_g, ln_b):
    T, D = x.shape
    tm = min(TM_LN, T)
    rows = pl.BlockSpec((tm, D), lambda i: (i, 0))
    gate = pl.BlockSpec((tm, 1), lambda i: (i, 0))
    vec = pl.BlockSpec((1, D), lambda i: (0, 0))
    return pl.pallas_call(
        _combine_kernel,
        grid=(T // tm,),
        in_specs=[rows, rows, rows, gate, gate, vec, vec],
        out_specs=[rows, rows],
        out_shape=[jax.ShapeDtypeStruct((T, D), F32), jax.ShapeDtypeStruct((T, D), BF16)],
        compiler_params=_cparams(("parallel",)),
        name="moe_combine",
    )(x, y0, y1, g0, g1, ln_g, ln_b)


def _pad_w_in(w_in):
    L, D, _ = w_in.shape
    q_dim = MLA_HEADS * (QK_NOPE + QK_ROPE)
    wq = w_in[:, :, :q_dim].reshape(L, D, MLA_HEADS, QK_NOPE + QK_ROPE)
    kr0 = q_dim + KV_RANK
    w_kr = w_in[:, :, kr0:kr0 + QK_ROPE]
    pad = jnp.zeros((L, D, MLA_HEADS, HEAD_W - QK_NOPE - QK_ROPE), w_in.dtype).at[:, :, 0, :].set(w_kr)
    q_part = jnp.concatenate([wq, pad], axis=-1).reshape(L, D, Q_COLS)
    rest = jnp.concatenate([w_in[:, :, q_dim:kr0], w_in[:, :, kr0 + QK_ROPE:]], axis=-1)
    return jnp.concatenate([q_part, rest], axis=-1).astype(BF16)


def _rope_tables(positions):
    inv_freq = ROPE_THETA ** (-jnp.arange(0, QK_ROPE, 2, dtype=F32) / QK_ROPE)
    ang = positions.astype(F32).reshape(-1)[:, None] * inv_freq
    cos, sin = jnp.cos(ang), jnp.sin(ang)
    cos4 = jnp.concatenate([cos, cos, cos, cos], axis=-1)
    sin4 = jnp.concatenate([-sin, sin, -sin, sin], axis=-1)
    return cos4, sin4


def _route(logits, tm):
    T = logits.shape[0]
    A = T * TOP_K
    top_logit, top_e = lax.top_k(logits, TOP_K)
    gates = jax.nn.softmax(top_logit, axis=-1)
    flat_e = top_e.reshape(A)
    onehot = (flat_e[:, None] == jnp.arange(N_EXPERTS, dtype=flat_e.dtype)[None, :]).astype(jnp.int32)
    csum = jnp.cumsum(onehot, axis=0)
    counts = csum[-1]
    rank = jnp.take_along_axis(csum, flat_e[:, None], axis=1)[:, 0] - 1
    padded = (counts + tm - 1) // tm * tm
    pad_ends = jnp.cumsum(padded)
    pad_starts = pad_ends - padded
    dest = (pad_starts[flat_e] + rank).astype(jnp.int32)
    n_blocks = -(-(A + N_EXPERTS * (tm - 1)) // tm)
    n_pad = n_blocks * tm
    tok = (jnp.arange(A, dtype=jnp.int32) // TOP_K)
    slot_tok = jnp.zeros((n_pad,), jnp.int32).at[dest].set(tok)
    block_start = jnp.arange(n_blocks, dtype=jnp.int32) * tm
    block_e = jnp.minimum(jnp.searchsorted(pad_ends, block_start, side='right'), N_EXPERTS - 1).astype(jnp.int32)
    n_valid = (pad_ends[-1] // tm).astype(jnp.int32).reshape(1)
    return dest.reshape(T, TOP_K), gates, slot_tok, block_e, n_valid


def kernel(x, positions, w_in, kv_norm_g, w_ukv, conv_w, pool_w, sgu_ln_g, sgu_ln_b, sgu_w, sgu_b, mix_gain, w_o, ln1_g, ln1_b, ffn_wg, ffn_wu, ffn_wd, router_w, exp_wg, exp_wu, exp_wd, ln2_g, ln2_b):
    B, S, D = x.shape
    T = B * S
    L = w_in.shape[0]
    cos4, sin4 = _rope_tables(positions)
    w_in_b = _pad_w_in(w_in)
    w_ukv_b = w_ukv.astype(BF16)
    w_o_b = w_o.astype(BF16)
    pool_w_b = pool_w.astype(BF16)
    sgu_bt = jnp.swapaxes(sgu_b, 1, 2)
    router_pad = jnp.pad(router_w, ((0, 0), (0, 0), (0, LANES - N_EXPERTS)))
    router_hi = router_pad.astype(BF16)
    router_hl = jnp.stack([router_hi, (router_pad - router_hi.astype(F32)).astype(BF16)], axis=1)
    ffn_w = (ffn_wg.astype(BF16), ffn_wu.astype(BF16), ffn_wd.astype(BF16))
    n_moe = exp_wg.shape[0]
    exp_f32 = tuple(w.reshape(-1, w.shape[-1]) for w in (exp_wg, exp_wu, exp_wd))
    exp_w = None

    xf = x.reshape(T, D)
    xb = xf.astype(BF16)
    tm_moe = min(TM_FFN, T)
    for l in range(L):
        proj = _in_proj(xb, w_in_b, l)
        q_cat, k_cat, v_t = _mla_prep(proj, cos4, sin4, kv_norm_g[l][None, :], w_ukv_b, l)
        o_mla = _attention(q_cat, k_cat, v_t, B, S)
        yc, yp, ys = _mixers(proj, conv_w[l], pool_w_b[l], sgu_ln_g[l][None, :], sgu_ln_b[l][None, :],
                             sgu_w[l], sgu_bt[l], mix_gain[l][None, :], S)
        ln1 = (ln1_g[l][None, :], ln1_b[l][None, :])
        ln2 = (ln2_g[l][None, :], ln2_b[l][None, :])
        e = l // 2
        if l % 2 == 0:
            x1, x1b = _out_proj(o_mla, yc, yp, ys, mix_gain[l][None, :], w_o_b, l, xf, *ln1)
            cast = [(w, e, n_moe) for w in exp_f32] if e < n_moe else []
            xf, xb, *cast_out = _ffn_dense(x1b, *ffn_w, e, x1, *ln2, cast=cast)
            if cast_out:
                exp_w = tuple(c.reshape((N_EXPERTS, -1, c.shape[-1])) for c in cast_out)
        else:
            x1, x1b, logits = _out_proj(o_mla, yc, yp, ys, mix_gain[l][None, :], w_o_b, l, xf, *ln1,
                                        router_hl=router_hl[e])
            dest, gates, slot_tok, block_e, n_valid = _route(logits[:, :N_EXPERTS], tm_moe)
            n_blocks = block_e.shape[0]
            chunks = max(c for c in (MOE_CHUNKS, 2, 1) if n_blocks % c == 0)
            cb = n_blocks // chunks
            y_rows = None
            for c in range(chunks):
                x_rows = x1b.at[slot_tok[c * cb * tm_moe:(c + 1) * cb * tm_moe]].get(mode="promise_in_bounds")
                y_rows = _ffn_moe(x_rows, *exp_w, block_e[c * cb:(c + 1) * cb],
                                  jnp.clip(n_valid - c * cb, 0, cb), y_rows, c * cb, n_blocks * tm_moe)
            y0 = y_rows.at[dest[:, 0]].get(mode="promise_in_bounds")
            y1 = y_rows.at[dest[:, 1]].get(mode="promise_in_bounds")
            xf, xb = _combine(x1, y0, y1, gates[:, 0:1], gates[:, 1:2], *ln2)
    return xf.reshape(B, S, D)
```

```python
import functools
import math

import jax
import jax.numpy as jnp
from jax import lax
from jax.experimental import pallas as pl
from jax.experimental.pallas import tpu as pltpu

F32 = jnp.float32
BF16 = jnp.bfloat16

D_MODEL = 2048
CHUNK = 64
GROUP_W = 512
N_GROUPS = 4
MLA_HEADS = 4
QK_NOPE = 128
QK_ROPE = 64
V_HEAD = 128
KV_RANK = 512
ROPE_THETA = 10000.0
CONV_W = 3
POOL_WINDOWS = (2, 4, 8, 16)
POOL_CH = 128
SGU_LEN = 128
SGU_GROUPS = 4
SGU_CH = 128
D_FF = 5632
N_EXPERTS = 8
TOP_K = 2
DEPTH = 4
ALPHA = (2.0 * DEPTH) ** 0.25
LN_EPS = 1e-5
RMS_EPS = 1e-6

LANES = 128
HEAD_W = 2 * LANES
VMEM_LIMIT = 56 * 1024 * 1024

Q_COLS = MLA_HEADS * HEAD_W
COL_Q, COL_CKV, COL_CB, COL_CC, COL_CH, COL_PH, COL_GU, COL_GV = 0, 2, 3, 4, 5, 6, 7, 8
IN_COLS_PAD = Q_COLS + 7 * GROUP_W

HALO = 32

TM_PROJ, TN_PROJ = 1024, 2304
TB_ATT = 1024
ATT_HEADS = 2
ATT_BLOCKS_PER_TRIP = 2
TM_PREP = TB_ATT
MASK_BIG = 2.0 ** 100
TM_MIX = 512
TM_OUT, OUT_PARTS = 512, 2
TM_LN = 256
TM_FFN = 512
TM_DENSE = 256
MOE_F_SPLIT = 2
MOE_CHUNKS = 4


def _cparams(sem):
    return pltpu.CompilerParams(dimension_semantics=sem, vmem_limit_bytes=VMEM_LIMIT)


def _layer_norm_rows(z, g, b):
    mu = jnp.mean(z, axis=-1, keepdims=True)
    zc = z - mu
    var = jnp.mean(zc * zc, axis=-1, keepdims=True)
    return zc * lax.rsqrt(var + LN_EPS) * g + b


def _rms_rows(y):
    return y * lax.rsqrt(jnp.mean(y * y, axis=-1, keepdims=True) + RMS_EPS)


def _matmul_kernel(x_ref, w_ref, o_ref):
    o_ref[...] = jnp.dot(x_ref[...], w_ref[...], preferred_element_type=F32).astype(o_ref.dtype)


def _in_proj(xb, w_all, l):
    T, K = xb.shape
    N = w_all.shape[2]
    tm = min(TM_PROJ, T)
    return pl.pallas_call(
        _matmul_kernel,
        grid=(T // tm, N // TN_PROJ),
        in_specs=[pl.BlockSpec((tm, K), lambda i, j: (i, 0)),
                  pl.BlockSpec((None, K, TN_PROJ), lambda i, j: (l, 0, j))],
        out_specs=pl.BlockSpec((tm, TN_PROJ), lambda i, j: (i, j)),
        out_shape=jax.ShapeDtypeStruct((T, N), BF16),
        compiler_params=_cparams(("parallel", "arbitrary")),
        name="in_proj",
    )(xb, w_all)


def _mla_prep_kernel(q_ref, ckv_ref, cos_ref, sin_ref, g_ref, wukv_ref, qo_ref, ko_ref, vt_ref, *, qscale):
    cos = cos_ref[...]
    sin = sin_ref[...]
    lane = lax.broadcasted_iota(jnp.int32, cos.shape, 1)
    row_chunk = lax.broadcasted_iota(jnp.int32, cos.shape, 0) // CHUNK
    first_half = (lane & (QK_ROPE // 2)) == 0
    low = lane < QK_ROPE

    def rope(r):
        partner = jnp.where(first_half, pltpu.roll(r, LANES - QK_ROPE // 2, axis=1), pltpu.roll(r, QK_ROPE // 2, axis=1))
        return r * cos + partner * sin

    k_rope = None
    for h in range(MLA_HEADS):
        c0 = h * HEAD_W
        qo_ref[:, c0:c0 + LANES] = (q_ref[:, c0:c0 + LANES].astype(F32) * qscale).astype(BF16)
        rr = rope(q_ref[:, c0 + LANES:c0 + HEAD_W].astype(F32))
        qo_ref[:, c0 + LANES:c0 + HEAD_W] = jnp.where(low, rr * qscale, 0.0).astype(BF16)
        if h == 0:
            stair = jnp.where(row_chunk > lane - QK_ROPE, -MASK_BIG, 0.0)
            k_rope = jnp.where(low, pltpu.roll(rr, QK_ROPE, axis=1), stair).astype(BF16)

    c = ckv_ref[...].astype(F32)
    cn = (_rms_rows(c) * g_ref[...]).astype(BF16)
    kv = jnp.dot(cn, wukv_ref[...], preferred_element_type=F32)
    for h in range(MLA_HEADS):
        c0 = h * HEAD_W
        ko_ref[:, c0:c0 + LANES] = kv[:, c0:c0 + LANES].astype(BF16)
        ko_ref[:, c0 + LANES:c0 + HEAD_W] = k_rope
        vt_ref[h] = kv[:, c0 + LANES:c0 + HEAD_W].T.astype(BF16)


def _mla_prep(proj, cos4, sin4, kv_g, w_ukv_all, l):
    T = proj.shape[0]
    tm = min(TM_PREP, T)
    qscale = math.log2(math.e) / math.sqrt(QK_NOPE + QK_ROPE)
    return pl.pallas_call(
        functools.partial(_mla_prep_kernel, qscale=qscale),
        grid=(T // tm,),
        in_specs=[pl.BlockSpec((tm, Q_COLS), lambda i: (i, 0)),
                  pl.BlockSpec((tm, GROUP_W), lambda i: (i, COL_CKV)),
                  pl.BlockSpec((tm, LANES), lambda i: (i, 0)),
                  pl.BlockSpec((tm, LANES), lambda i: (i, 0)),
                  pl.BlockSpec((1, KV_RANK), lambda i: (0, 0)),
                  pl.BlockSpec((None,) + w_ukv_all.shape[1:], lambda i: (l, 0, 0))],
        out_specs=[pl.BlockSpec((tm, Q_COLS), lambda i: (i, 0)),
                   pl.BlockSpec((tm, Q_COLS), lambda i: (i, 0)),
                   pl.BlockSpec((None, MLA_HEADS, V_HEAD, tm), lambda i: (i, 0, 0, 0))],
        out_shape=[jax.ShapeDtypeStruct((T, Q_COLS), BF16),
                   jax.ShapeDtypeStruct((T, Q_COLS), BF16),
                   jax.ShapeDtypeStruct((T // tm, MLA_HEADS, V_HEAD, tm), BF16)],
        compiler_params=_cparams(("parallel",)),
        name="mla_prep",
    )(proj, proj, cos4, sin4, kv_g, w_ukv_all)


def _attention_kernel(q_ref, k_ref, vt_ref, o_ref, q2_ref, s0_ref, s1_ref, m_ref, l_ref, acc_ref, *, tb):
    i = pl.program_id(2)
    hp = ATT_HEADS
    q = q_ref[...]
    lane = lax.broadcasted_iota(jnp.int32, q.shape, 1) % HEAD_W
    row_chunk = lax.broadcasted_iota(jnp.int32, q.shape, 0) // CHUNK
    q2_ref[0] = q
    q2_ref[1] = jnp.where(lane == QK_NOPE + QK_ROPE + row_chunk, jnp.ones_like(q), q)
    m_ref[...] = jnp.full(m_ref.shape, -jnp.inf, F32)
    l_ref[...] = jnp.zeros(l_ref.shape, F32)
    acc_ref[...] = jnp.zeros(acc_ref.shape, F32)

    def scores(j, s_ref):
        r0 = pl.multiple_of(j * tb, tb)
        sel = (j == i).astype(jnp.int32)
        for h in range(hp):
            cs = slice(h * HEAD_W, (h + 1) * HEAD_W)
            s_ref[h] = lax.dot_general(k_ref[pl.ds(r0, tb), cs], q2_ref[sel, :, cs], (((1,), (1,)), ((), ())),
                                       preferred_element_type=F32)

    def softmax_pv(j, s_ref):
        for h in range(hp):
            s = s_ref[h]
            m_old = m_ref[h]
            m_new = jnp.maximum(m_old, jnp.max(s, axis=0, keepdims=True))
            p = jnp.exp2(s - m_new)
            a = jnp.exp2(m_old - m_new)
            l_ref[h] = a * l_ref[h] + jnp.sum(p, axis=0, keepdims=True)
            acc_ref[h] = a * acc_ref[h] + jnp.dot(vt_ref[j, h], p.astype(BF16), preferred_element_type=F32)
            m_ref[h] = m_new

    scores(0, s0_ref)
    bufs = (s0_ref, s1_ref)
    U = ATT_BLOCKS_PER_TRIP

    def trip(g, carry):
        j = U * g
        for u in range(U):
            scores(j + u + 1, bufs[(u + 1) % 2])
            softmax_pv(j + u, bufs[u % 2])
        return carry

    trips = i // U
    lax.fori_loop(0, trips, trip, 0)
    base = U * trips
    for u in range(U):
        j = base + u

        @pl.when(j < i)
        def _():
            scores(j + 1, bufs[(u + 1) % 2])
            softmax_pv(j, bufs[u % 2])

        @pl.when(j == i)
        def _():
            softmax_pv(j, bufs[u % 2])

    for h in range(hp):
        o_ref[:, h * V_HEAD:(h + 1) * V_HEAD] = (acc_ref[h] / l_ref[h]).T


def _attention(q_cat, k_cat, v_t, B, S):
    T = B * S
    tb = v_t.shape[-1]
    nq = S // tb
    hp = ATT_HEADS
    resident = dict(pipeline_mode=pl.Buffered(1))
    return pl.pallas_call(
        functools.partial(_attention_kernel, tb=tb),
        grid=(B, MLA_HEADS // hp, nq),
        in_specs=[pl.BlockSpec((tb, hp * HEAD_W), lambda b, h, i: (b * nq + i, h)),
                  pl.BlockSpec((S, hp * HEAD_W), lambda b, h, i: (b, h), **resident),
                  pl.BlockSpec((nq, hp, V_HEAD, tb), lambda b, h, i: (b, h, 0, 0), **resident)],
        out_specs=pl.BlockSpec((tb, hp * V_HEAD), lambda b, h, i: (b * nq + i, h)),
        out_shape=jax.ShapeDtypeStruct((T, MLA_HEADS * V_HEAD), F32),
        scratch_shapes=[pltpu.VMEM((2, tb, hp * HEAD_W), BF16),
                        pltpu.VMEM((hp, tb, tb), F32), pltpu.VMEM((hp, tb, tb), F32),
                        pltpu.VMEM((hp, 1, tb), F32), pltpu.VMEM((hp, 1, tb), F32),
                        pltpu.VMEM((hp, V_HEAD, tb), F32)],
        compiler_params=_cparams(("parallel", "parallel", "arbitrary")),
        name="attention",
    )(q_cat, k_cat, v_t)


def _gelu(x):
    return 0.5 * x * (1.0 + lax.erf(x * (1.0 / math.sqrt(2.0))))


def _mixers_kernel(cb_ref, cc_ref, ch_ref, ph_ref, gu_ref, gv_ref, hcc_ref, hch_ref, hph_ref,
                   convw_ref, poolw_ref, lng_ref, lnb_ref, sguw_ref, sgub_ref, gain_ref,
                   yc_ref, yp_ref, ys_ref,
                   g_scr, e_scr, a_scr, b_scr, y_scr, *, tm, tiles_per_seq):
    i = pl.program_id(0)
    t0 = (i % tiles_per_seq) * tm
    keep = jnp.where(t0 == 0, 0.0, 1.0).astype(F32)

    g_scr[0:HALO, :] = hcc_ref[...].astype(F32) * hch_ref[...].astype(F32) * keep
    g_scr[HALO:, :] = cc_ref[...].astype(F32) * ch_ref[...].astype(F32)
    conv = convw_ref[CONV_W - 1:CONV_W, :] * g_scr[HALO:HALO + tm, :]
    for j in range(CONV_W - 1):
        off = HALO - (CONV_W - 1) + j
        conv = conv + convw_ref[j:j + 1, :] * g_scr[off:off + tm, :]
    yc = cb_ref[...].astype(F32) * conv
    yc_ref[...] = (_rms_rows(yc) * gain_ref[:, GROUP_W:2 * GROUP_W]).astype(BF16)

    e_scr[0:HALO, :] = hph_ref[...].astype(F32) * keep
    e_scr[HALO:, :] = ph_ref[...].astype(F32)
    n = tm + HALO
    pos = (t0 + 1 + lax.broadcasted_iota(jnp.int32, (tm, 1), 0)).astype(F32)
    for g, w in enumerate(POOL_WINDOWS):
        cs = slice(g * POOL_CH, (g + 1) * POOL_CH)
        levels = g + 1
        src = None
        bufs = (a_scr, b_scr)
        total = None
        for lev in range(1, levels + 1):
            shift = 1 << (lev - 1)
            lo = 8 * lev if lev < levels else HALO
            if src is None:
                cur = e_scr[lo:n, cs] + e_scr[lo - shift:n - shift, cs]
            else:
                cur = src[lo:n, :] + src[lo - shift:n - shift, :]
            if lev < levels:
                dst = bufs[(lev - 1) % 2]
                dst[lo:n, :] = cur
                src = dst
            else:
                total = cur
        inv = 1.0 / jnp.minimum(pos, float(w))
        pooled = total * inv - e_scr[HALO:, cs]
        y_scr[:, cs] = jnp.dot(pooled.astype(BF16), poolw_ref[g], preferred_element_type=F32)
    yp = y_scr[...]
    yp_ref[...] = (_rms_rows(yp) * gain_ref[:, 2 * GROUP_W:3 * GROUP_W]).astype(BF16)

    u = _gelu(gu_ref[...].astype(F32))
    vn = _layer_norm_rows(_gelu(gv_ref[...].astype(F32)), lng_ref[...], lnb_ref[...]).astype(BF16)
    row = lax.broadcasted_iota(jnp.int32, (SGU_LEN, SGU_LEN), 0)
    col = lax.broadcasted_iota(jnp.int32, (SGU_LEN, SGU_LEN), 1)
    for g in range(SGU_GROUPS):
        cs = slice(g * SGU_CH, (g + 1) * SGU_CH)
        wg = jnp.where(col <= row, sguw_ref[g], 0.0).astype(BF16)
        bias = sgub_ref[:, g:g + 1]
        for c in range(tm // SGU_LEN):
            rs = slice(c * SGU_LEN, (c + 1) * SGU_LEN)
            mixed = jnp.dot(wg, vn[rs, cs], preferred_element_type=F32) + bias
            y_scr[rs, cs] = u[rs, cs] * mixed
    ysg = y_scr[...]
    ys_ref[...] = (_rms_rows(ysg) * gain_ref[:, 3 * GROUP_W:4 * GROUP_W]).astype(BF16)


def _mixers(proj, conv_w, pool_w, ln_g, ln_b, sgu_w, sgu_bt, gain, S):
    T = proj.shape[0]
    tm = min(TM_MIX, S)
    tiles_per_seq = S // tm
    hb = tm // HALO

    def col(c):
        return pl.BlockSpec((tm, GROUP_W), lambda i: (i, c))

    def halo(c):
        return pl.BlockSpec((HALO, GROUP_W), lambda i: (jnp.maximum(i * hb - 1, 0), c))

    def full(a):
        return pl.BlockSpec(a.shape, lambda i: (0,) * a.ndim)

    out = jax.ShapeDtypeStruct((T, GROUP_W), BF16)
    return pl.pallas_call(
        functools.partial(_mixers_kernel, tm=tm, tiles_per_seq=tiles_per_seq),
        grid=(T // tm,),
        in_specs=[col(COL_CB), col(COL_CC), col(COL_CH), col(COL_PH), col(COL_GU), col(COL_GV),
                  halo(COL_CC), halo(COL_CH), halo(COL_PH),
                  full(conv_w), full(pool_w), full(ln_g), full(ln_b), full(sgu_w), full(sgu_bt), full(gain)],
        out_specs=[pl.BlockSpec((tm, GROUP_W), lambda i: (i, 0))] * 3,
        out_shape=[out, out, out],
        scratch_shapes=[pltpu.VMEM((tm + HALO, GROUP_W), F32), pltpu.VMEM((tm + HALO, GROUP_W), F32),
                        pltpu.VMEM((tm + HALO, POOL_CH), F32), pltpu.VMEM((tm + HALO, POOL_CH), F32),
                        pltpu.VMEM((tm, GROUP_W), F32)],
        compiler_params=_cparams(("parallel",)),
        name="mixers",
    )(proj, proj, proj, proj, proj, proj, proj, proj, proj,
      conv_w, pool_w, ln_g, ln_b, sgu_w, sgu_bt, gain)


def _out_proj_kernel(*refs, with_router):
    if with_router:
        (o_ref, yc_ref, yp_ref, ys_ref, gain_ref, wo_ref, x_ref, g_ref, b_ref, rw_ref,
         x1_ref, x1b_ref, lg_ref) = refs
    else:
        (o_ref, yc_ref, yp_ref, ys_ref, gain_ref, wo_ref, x_ref, g_ref, b_ref,
         x1_ref, x1b_ref) = refs
    hr = o_ref.shape[0] // OUT_PARTS
    for part in range(OUT_PARTS):
        rs = slice(part * hr, (part + 1) * hr)
        ym = (_rms_rows(o_ref[rs, :]) * gain_ref[:, 0:GROUP_W]).astype(BF16)
        mix = jnp.dot(ym, wo_ref[0:GROUP_W, :], preferred_element_type=F32)
        for gi, y_ref in enumerate((yc_ref, yp_ref, ys_ref), start=1):
            mix = mix + jnp.dot(y_ref[rs, :], wo_ref[gi * GROUP_W:(gi + 1) * GROUP_W, :], preferred_element_type=F32)
        x1 = _layer_norm_rows(ALPHA * x_ref[rs, :] + mix, g_ref[...], b_ref[...])
        x1_ref[rs, :] = x1
        x1_hi = x1.astype(BF16)
        x1b_ref[rs, :] = x1_hi
        if with_router:
            x1_lo = (x1 - x1_hi.astype(F32)).astype(BF16)
            lg_ref[rs, :] = (jnp.dot(x1_hi, rw_ref[0], preferred_element_type=F32)
                             + (jnp.dot(x1_hi, rw_ref[1], preferred_element_type=F32)
                                + jnp.dot(x1_lo, rw_ref[0], preferred_element_type=F32)))


def _out_proj(o_mla, yc, yp, ys, gain, wo_all, l, x, ln_g, ln_b, router_hl=None):
    T, D = x.shape
    tm = min(TM_OUT, T)
    with_router = router_hl is not None

    def rows(w):
        return pl.BlockSpec((tm, w), lambda i: (i, 0))

    def full(a):
        return pl.BlockSpec(a.shape, lambda i: (0,) * a.ndim)

    wo_spec = pl.BlockSpec((None,) + wo_all.shape[1:], lambda i: (l, 0, 0))
    in_specs = [rows(GROUP_W)] * 4 + [full(gain), wo_spec, rows(D), full(ln_g), full(ln_b)]
    args = [o_mla, yc, yp, ys, gain, wo_all, x, ln_g, ln_b]
    out_specs = [rows(D), rows(D)]
    out_shape = [jax.ShapeDtypeStruct((T, D), F32), jax.ShapeDtypeStruct((T, D), BF16)]
    if with_router:
        in_specs.append(full(router_hl))
        args.append(router_hl)
        out_specs.append(rows(LANES))
        out_shape.append(jax.ShapeDtypeStruct((T, LANES), F32))
    return pl.pallas_call(
        functools.partial(_out_proj_kernel, with_router=with_router),
        grid=(T // tm,),
        in_specs=in_specs,
        out_specs=out_specs,
        out_shape=out_shape,
        compiler_params=_cparams(("parallel",)),
        name="out_proj",
    )(*args)


def _swiglu(xb, wg, wu):
    hg = jnp.dot(xb, wg, preferred_element_type=F32)
    hu = jnp.dot(xb, wu, preferred_element_type=F32)
    return (hg * jax.nn.sigmoid(hg) * hu).astype(BF16)


def _dense_up_kernel(*refs, n_cast):
    x_ref, wg_ref, wu_ref = refs[:3]
    cast_in = refs[3:3 + n_cast]
    h_ref = refs[3 + n_cast]
    cast_out = refs[4 + n_cast:4 + 2 * n_cast]
    for src_ref, dst_ref in zip(cast_in, cast_out):
        dst_ref[...] = src_ref[...].astype(dst_ref.dtype)
    h_ref[...] = _swiglu(x_ref[...], wg_ref[...], wu_ref[...])


def _dense_down_kernel(h_ref, wd_ref, xres_ref, g_ref, b_ref, o_ref, ob_ref):
    y = jnp.dot(h_ref[...], wd_ref[...], preferred_element_type=F32)
    x2 = _layer_norm_rows(ALPHA * xres_ref[...] + y, g_ref[...], b_ref[...])
    o_ref[...] = x2
    ob_ref[...] = x2.astype(BF16)


def _ffn_dense(xb, wg, wu, wd, e, xres, ln_g, ln_b, cast=()):
    T, D = xb.shape
    F = wg.shape[2]
    tm = min(TM_DENSE, T)
    nb = T // tm
    fh = F // MOE_F_SPLIT
    resident = dict(pipeline_mode=pl.Buffered(1))
    in_specs = [pl.BlockSpec((tm, D), lambda f, i: (i, 0)),
                pl.BlockSpec((None, D, fh), lambda f, i: (e, 0, f), **resident),
                pl.BlockSpec((None, D, fh), lambda f, i: (e, 0, f), **resident)]
    out_specs = [pl.BlockSpec((tm, fh), lambda f, i: (i, f))]
    out_shape = [jax.ShapeDtypeStruct((T, F), BF16)]
    steps = MOE_F_SPLIT * nb
    for arr, part, n_parts in cast:
        rows, cols = arr.shape[0] // n_parts, arr.shape[1]
        rb = rows // steps
        assert rb * steps == rows and rb % 16 == 0, (rows, steps)
        in_specs.append(pl.BlockSpec((rb, cols), lambda f, i, off=part * steps: (off + f * nb + i, 0)))
        out_specs.append(pl.BlockSpec((rb, cols), lambda f, i: (f * nb + i, 0)))
        out_shape.append(jax.ShapeDtypeStruct((rows, cols), BF16))
    h, *cast_out = pl.pallas_call(
        functools.partial(_dense_up_kernel, n_cast=len(cast)),
        grid=(MOE_F_SPLIT, nb),
        in_specs=in_specs,
        out_specs=out_specs,
        out_shape=out_shape,
        compiler_params=_cparams(("arbitrary", "arbitrary")),
        name="dense_up",
    )(xb, wg, wu, *[c[0] for c in cast])
    rows_spec = pl.BlockSpec((tm, D), lambda i: (i, 0))
    vec = pl.BlockSpec((1, D), lambda i: (0, 0))
    x2, x2b = pl.pallas_call(
        _dense_down_kernel,
        grid=(nb,),
        in_specs=[pl.BlockSpec((tm, F), lambda i: (i, 0)),
                  pl.BlockSpec((None, F, D), lambda i: (e, 0, 0), **resident), rows_spec, vec, vec],
        out_specs=[rows_spec, rows_spec],
        out_shape=[jax.ShapeDtypeStruct((T, D), F32), jax.ShapeDtypeStruct((T, D), BF16)],
        compiler_params=_cparams(("arbitrary",)),
        name="dense_down",
    )(h, wd, xres, ln_g, ln_b)
    return (x2, x2b, *cast_out)


def _moe_up_kernel(be_ref, nv_ref, x_ref, wg_ref, wu_ref, h_ref):
    i = pl.program_id(1)

    @pl.when(i < nv_ref[0])
    def _():
        h_ref[...] = _swiglu(x_ref[...], wg_ref[...], wu_ref[...])

    @pl.when(i >= nv_ref[0])
    def _():
        h_ref[...] = jnp.zeros(h_ref.shape, h_ref.dtype)


def _moe_down_kernel(be_ref, nv_ref, *refs):
    h_ref, wd_ref, o_ref = refs[0], refs[1], refs[-1]
    i = pl.program_id(0)

    @pl.when(i < nv_ref[0])
    def _():
        o_ref[...] = jnp.dot(h_ref[...], wd_ref[...], preferred_element_type=F32).astype(o_ref.dtype)

    @pl.when(i >= nv_ref[0])
    def _():
        o_ref[...] = jnp.zeros(o_ref.shape, o_ref.dtype)


def _ffn_moe(x_rows_b, wg, wu, wd, block_e, n_valid, y_buf, blk0, total_rows):
    Rc, D = x_rows_b.shape
    F = wg.shape[2]
    tm = min(TM_FFN, Rc)
    nb = Rc // tm
    fh = F // MOE_F_SPLIT
    resident = dict(pipeline_mode=pl.Buffered(1))
    h = pl.pallas_call(
        _moe_up_kernel,
        grid_spec=pltpu.PrefetchScalarGridSpec(
            num_scalar_prefetch=2,
            grid=(MOE_F_SPLIT, nb),
            in_specs=[pl.BlockSpec((tm, D), lambda f, i, be, nv: (i, 0)),
                      pl.BlockSpec((None, D, fh), lambda f, i, be, nv: (be[i], 0, f), **resident),
                      pl.BlockSpec((None, D, fh), lambda f, i, be, nv: (be[i], 0, f), **resident)],
            out_specs=pl.BlockSpec((tm, fh), lambda f, i, be, nv: (i, f))),
        out_shape=jax.ShapeDtypeStruct((Rc, F), BF16),
        compiler_params=_cparams(("arbitrary", "arbitrary")),
        name="moe_up",
    )(block_e, n_valid, x_rows_b, wg, wu)
    in_specs = [pl.BlockSpec((tm, F), lambda i, be, nv: (i, 0)),
                pl.BlockSpec((None, F, D), lambda i, be, nv: (be[i], 0, 0), **resident)]
    args = [block_e, n_valid, h, wd]
    aliases = {}
    if y_buf is not None:
        in_specs.append(pl.BlockSpec(memory_space=pl.ANY))
        args.append(y_buf)
        aliases = {len(args) - 1: 0}
    return pl.pallas_call(
        _moe_down_kernel,
        grid_spec=pltpu.PrefetchScalarGridSpec(
            num_scalar_prefetch=2,
            grid=(nb,),
            in_specs=in_specs,
            out_specs=pl.BlockSpec((tm, D), lambda i, be, nv: (blk0 + i, 0))),
        out_shape=jax.ShapeDtypeStruct((total_rows, D), BF16),
        input_output_aliases=aliases,
        compiler_params=_cparams(("arbitrary",)),
        name="moe_down",
    )(*args)


def _combine_kernel(x_ref, y0_ref, y1_ref, g0_ref, g1_ref, g_ref, b_ref, o_ref, ob_ref):
    ffn = y0_ref[...].astype(F32) * g0_ref[...] + y1_ref[...].astype(F32) * g1_ref[...]
    x2 = _layer_norm_rows(ALPHA * x_ref[...] + ffn, g_ref[...], b_ref[...])
    o_ref[...] = x2
    ob_ref[...] = x2.astype(BF16)


def _combine(x, y0, y1, g0, g1, ln_g, ln_b):
    T, D = x.shape
    tm = min(TM_LN, T)
    rows = pl.BlockSpec((tm, D), lambda i: (i, 0))
    gate = pl.BlockSpec((tm, 1), lambda i: (i, 0))
    vec = pl.BlockSpec((1, D), lambda i: (0, 0))
    return pl.pallas_call(
        _combine_kernel,
        grid=(T // tm,),
        in_specs=[rows, rows, rows, gate, gate, vec, vec],
        out_specs=[rows, rows],
        out_shape=[jax.ShapeDtypeStruct((T, D), F32), jax.ShapeDtypeStruct((T, D), BF16)],
        compiler_params=_cparams(("parallel",)),
        name="moe_combine",
    )(x, y0, y1, g0, g1, ln_g, ln_b)


def _pad_w_in(w_in):
    L, D, _ = w_in.shape
    q_dim = MLA_HEADS * (QK_NOPE + QK_ROPE)
    wq = w_in[:, :, :q_dim].reshape(L, D, MLA_HEADS, QK_NOPE + QK_ROPE)
    kr0 = q_dim + KV_RANK
    w_kr = w_in[:, :, kr0:kr0 + QK_ROPE]
    pad = jnp.zeros((L, D, MLA_HEADS, HEAD_W - QK_NOPE - QK_ROPE), w_in.dtype).at[:, :, 0, :].set(w_kr)
    q_part = jnp.concatenate([wq, pad], axis=-1).reshape(L, D, Q_COLS)
    rest = jnp.concatenate([w_in[:, :, q_dim:kr0], w_in[:, :, kr0 + QK_ROPE:]], axis=-1)
    return jnp.concatenate([q_part, rest], axis=-1).astype(BF16)


def _rope_tables(positions):
    inv_freq = ROPE_THETA ** (-jnp.arange(0, QK_ROPE, 2, dtype=F32) / QK_ROPE)
    ang = positions.astype(F32).reshape(-1)[:, None] * inv_freq
    cos, sin = jnp.cos(ang), jnp.sin(ang)
    cos4 = jnp.concatenate([cos, cos, cos, cos], axis=-1)
    sin4 = jnp.concatenate([-sin, sin, -sin, sin], axis=-1)
    return cos4, sin4


def _route(logits, tm):
    T = logits.shape[0]
    A = T * TOP_K
    top_logit, top_e = lax.top_k(logits, TOP_K)
    gates = jax.nn.softmax(top_logit, axis=-1)
    flat_e = top_e.reshape(A)
    onehot = (flat_e[:, None] == jnp.arange(N_EXPERTS, dtype=flat_e.dtype)[None, :]).astype(jnp.int32)
    csum = jnp.cumsum(onehot, axis=0)
    counts = csum[-1]
    rank = jnp.take_along_axis(csum, flat_e[:, None], axis=1)[:, 0] - 1
    padded = (counts + tm - 1) // tm * tm
    pad_ends = jnp.cumsum(padded)
    pad_starts = pad_ends - padded
    dest = (pad_starts[flat_e] + rank).astype(jnp.int32)
    n_blocks = -(-(A + N_EXPERTS * (tm - 1)) // tm)
    n_pad = n_blocks * tm
    block_start = jnp.arange(n_blocks, dtype=jnp.int32) * tm
    block_e = jnp.minimum(jnp.searchsorted(pad_ends, block_start, side='right'), N_EXPERTS - 1).astype(jnp.int32)
    arange_a = jnp.arange(A, dtype=jnp.int32)
    _, order = lax.sort_key_val(flat_e.astype(jnp.int32) * A + arange_a, arange_a)
    starts = jnp.cumsum(counts) - counts
    e_s = jnp.repeat(block_e, tm)
    r = jnp.arange(n_pad, dtype=jnp.int32) - pad_starts[e_s].astype(jnp.int32)
    src = jnp.clip(starts[e_s].astype(jnp.int32) + r, 0, A - 1)
    tok_sorted = order // TOP_K
    slot_tok = jnp.where(r < counts[e_s], tok_sorted.at[src].get(mode="promise_in_bounds"), 0).astype(jnp.int32)
    n_valid = (pad_ends[-1] // tm).astype(jnp.int32).reshape(1)
    return dest.reshape(T, TOP_K), gates, slot_tok, block_e, n_valid


def kernel(x, positions, w_in, kv_norm_g, w_ukv, conv_w, pool_w, sgu_ln_g, sgu_ln_b, sgu_w, sgu_b, mix_gain, w_o, ln1_g, ln1_b, ffn_wg, ffn_wu, ffn_wd, router_w, exp_wg, exp_wu, exp_wd, ln2_g, ln2_b):
    B, S, D = x.shape
    T = B * S
    L = w_in.shape[0]
    cos4, sin4 = _rope_tables(positions)
    w_in_b = _pad_w_in(w_in)
    w_ukv_b = w_ukv.astype(BF16)
    w_o_b = w_o.astype(BF16)
    pool_w_b = pool_w.astype(BF16)
    sgu_bt = jnp.swapaxes(sgu_b, 1, 2)
    router_pad = jnp.pad(router_w, ((0, 0), (0, 0), (0, LANES - N_EXPERTS)))
    router_hi = router_pad.astype(BF16)
    router_hl = jnp.stack([router_hi, (router_pad - router_hi.astype(F32)).astype(BF16)], axis=1)
    ffn_w = (ffn_wg.astype(BF16), ffn_wu.astype(BF16), ffn_wd.astype(BF16))
    n_moe = exp_wg.shape[0]
    exp_f32 = tuple(w.reshape(-1, w.shape[-1]) for w in (exp_wg, exp_wu, exp_wd))
    exp_w = None

    xf = x.reshape(T, D)
    xb = xf.astype(BF16)
    tm_moe = min(TM_FFN, T)
    for l in range(L):
        proj = _in_proj(xb, w_in_b, l)
        q_cat, k_cat, v_t = _mla_prep(proj, cos4, sin4, kv_norm_g[l][None, :], w_ukv_b, l)
        o_mla = _attention(q_cat, k_cat, v_t, B, S)
        yc, yp, ys = _mixers(proj, conv_w[l], pool_w_b[l], sgu_ln_g[l][None, :], sgu_ln_b[l][None, :],
                             sgu_w[l], sgu_bt[l], mix_gain[l][None, :], S)
        ln1 = (ln1_g[l][None, :], ln1_b[l][None, :])
        ln2 = (ln2_g[l][None, :], ln2_b[l][None, :])
        e = l // 2
        if l % 2 == 0:
            x1, x1b = _out_proj(o_mla, yc, yp, ys, mix_gain[l][None, :], w_o_b, l, xf, *ln1)
            cast = [(w, e, n_moe) for w in exp_f32] if e < n_moe else []
            xf, xb, *cast_out = _ffn_dense(x1b, *ffn_w, e, x1, *ln2, cast=cast)
            if cast_out:
                exp_w = tuple(c.reshape((N_EXPERTS, -1, c.shape[-1])) for c in cast_out)
        else:
            x1, x1b, logits = _out_proj(o_mla, yc, yp, ys, mix_gain[l][None, :], w_o_b, l, xf, *ln1,
                                        router_hl=router_hl[e])
            dest, gates, slot_tok, block_e, n_valid = _route(logits[:, :N_EXPERTS], tm_moe)
            n_blocks = block_e.shape[0]
            chunks = max(c for c in (MOE_CHUNKS, 2, 1) if n_blocks % c == 0)
            cb = n_blocks // chunks
            y_rows = None
            for c in range(chunks):
                x_rows = x1b.at[slot_tok[c * cb * tm_moe:(c + 1) * cb * tm_moe]].get(mode="promise_in_bounds")
                y_rows = _ffn_moe(x_rows, *exp_w, block_e[c * cb:(c + 1) * cb],
                                  jnp.clip(n_valid - c * cb, 0, cb), y_rows, c * cb, n_blocks * tm_moe)
            y0 = y_rows.at[dest[:, 0]].get(mode="promise_in_bounds")
            y1 = y_rows.at[dest[:, 1]].get(mode="promise_in_bounds")
            xf, xb = _combine(x1, y0, y1, gates[:, 0:1], gates[:, 1:2], *ln2)
    return xf.reshape(B, S, D)
```

```python
import functools
import math

import jax
import jax.numpy as jnp
from jax import lax
from jax.experimental import pallas as pl
from jax.experimental.pallas import tpu as pltpu

F32 = jnp.float32
BF16 = jnp.bfloat16

CHUNK = 64
GROUP_W = 512
MLA_HEADS = 4
QK_NOPE = 128
QK_ROPE = 64
V_HEAD = 128
KV_RANK = 512
ROPE_THETA = 10000.0
CONV_W = 3
POOL_WINDOWS = (2, 4, 8, 16)
POOL_CH = 128
SGU_LEN = 128
SGU_GROUPS = 4
SGU_CH = 128
N_EXPERTS = 8
TOP_K = 2
DEPTH = 4
ALPHA = (2.0 * DEPTH) ** 0.25
LN_EPS = 1e-5
RMS_EPS = 1e-6

LANES = 128
HEAD_W = 2 * LANES
VMEM_LIMIT = 56 * 1024 * 1024

Q_COLS = MLA_HEADS * HEAD_W
COL_CKV, COL_CB, COL_CC, COL_CH, COL_PH, COL_GU, COL_GV = 2, 3, 4, 5, 6, 7, 8

HALO = 32

TM_PROJ, TN_PROJ = 1024, 2304
TB_ATT = 1024
ATT_HEADS = 2
ATT_BLOCKS_PER_TRIP = 2
TM_PREP = TB_ATT
MASK_BIG = 2.0 ** 100
TM_MIX = 512
TM_OUT, OUT_PARTS = 512, 2
TM_LN = 256
TM_FFN = 512
TM_DENSE = 256
MOE_F_SPLIT = 2
MOE_CHUNKS = 4
COMBINE_PARTS = 2


def _cparams(sem):
    return pltpu.CompilerParams(dimension_semantics=sem, vmem_limit_bytes=VMEM_LIMIT)


def _layer_norm_rows(z, g, b):
    mu = jnp.mean(z, axis=-1, keepdims=True)
    zc = z - mu
    var = jnp.mean(zc * zc, axis=-1, keepdims=True)
    return zc * lax.rsqrt(var + LN_EPS) * g + b


def _rms_rows(y):
    return y * lax.rsqrt(jnp.mean(y * y, axis=-1, keepdims=True) + RMS_EPS)


def _matmul_kernel(x_ref, w_ref, o_ref):
    o_ref[...] = jnp.dot(x_ref[...], w_ref[...], preferred_element_type=F32).astype(o_ref.dtype)


def _in_proj(xb, w_all, l):
    T, K = xb.shape
    N = w_all.shape[2]
    tm = min(TM_PROJ, T)
    return pl.pallas_call(
        _matmul_kernel,
        grid=(T // tm, N // TN_PROJ),
        in_specs=[pl.BlockSpec((tm, K), lambda i, j: (i, 0)),
                  pl.BlockSpec((None, K, TN_PROJ), lambda i, j: (l, 0, j))],
        out_specs=pl.BlockSpec((tm, TN_PROJ), lambda i, j: (i, j)),
        out_shape=jax.ShapeDtypeStruct((T, N), BF16),
        compiler_params=_cparams(("parallel", "arbitrary")),
        name="in_proj",
    )(xb, w_all)


def _mla_prep_kernel(q_ref, ckv_ref, cos_ref, sin_ref, g_ref, wukv_ref, qo_ref, ko_ref, vt_ref, *, qscale):
    cos = cos_ref[...]
    sin = sin_ref[...]
    lane = lax.broadcasted_iota(jnp.int32, cos.shape, 1)
    row_chunk = lax.broadcasted_iota(jnp.int32, cos.shape, 0) // CHUNK
    first_half = (lane & (QK_ROPE // 2)) == 0
    low = lane < QK_ROPE

    def rope(r):
        partner = jnp.where(first_half, pltpu.roll(r, LANES - QK_ROPE // 2, axis=1), pltpu.roll(r, QK_ROPE // 2, axis=1))
        return r * cos + partner * sin

    k_rope = None
    for h in range(MLA_HEADS):
        c0 = h * HEAD_W
        qo_ref[:, c0:c0 + LANES] = (q_ref[:, c0:c0 + LANES].astype(F32) * qscale).astype(BF16)
        rr = rope(q_ref[:, c0 + LANES:c0 + HEAD_W].astype(F32))
        qo_ref[:, c0 + LANES:c0 + HEAD_W] = jnp.where(low, rr * qscale, 0.0).astype(BF16)
        if h == 0:
            stair = jnp.where(row_chunk > lane - QK_ROPE, -MASK_BIG, 0.0)
            k_rope = jnp.where(low, pltpu.roll(rr, QK_ROPE, axis=1), stair).astype(BF16)

    c = ckv_ref[...].astype(F32)
    cn = (_rms_rows(c) * g_ref[...]).astype(BF16)
    kv = jnp.dot(cn, wukv_ref[...], preferred_element_type=F32)
    for h in range(MLA_HEADS):
        c0 = h * HEAD_W
        ko_ref[:, c0:c0 + LANES] = kv[:, c0:c0 + LANES].astype(BF16)
        ko_ref[:, c0 + LANES:c0 + HEAD_W] = k_rope
        vt_ref[h] = kv[:, c0 + LANES:c0 + HEAD_W].T.astype(BF16)


def _mla_prep(proj, cos4, sin4, kv_g, w_ukv_all, l):
    T = proj.shape[0]
    tm = min(TM_PREP, T)
    qscale = math.log2(math.e) / math.sqrt(QK_NOPE + QK_ROPE)
    return pl.pallas_call(
        functools.partial(_mla_prep_kernel, qscale=qscale),
        grid=(T // tm,),
        in_specs=[pl.BlockSpec((tm, Q_COLS), lambda i: (i, 0)),
                  pl.BlockSpec((tm, GROUP_W), lambda i: (i, COL_CKV)),
                  pl.BlockSpec((tm, LANES), lambda i: (i, 0)),
                  pl.BlockSpec((tm, LANES), lambda i: (i, 0)),
                  pl.BlockSpec((1, KV_RANK), lambda i: (0, 0)),
                  pl.BlockSpec((None,) + w_ukv_all.shape[1:], lambda i: (l, 0, 0))],
        out_specs=[pl.BlockSpec((tm, Q_COLS), lambda i: (i, 0)),
                   pl.BlockSpec((tm, Q_COLS), lambda i: (i, 0)),
                   pl.BlockSpec((None, MLA_HEADS, V_HEAD, tm), lambda i: (i, 0, 0, 0))],
        out_shape=[jax.ShapeDtypeStruct((T, Q_COLS), BF16),
                   jax.ShapeDtypeStruct((T, Q_COLS), BF16),
                   jax.ShapeDtypeStruct((T // tm, MLA_HEADS, V_HEAD, tm), BF16)],
        compiler_params=_cparams(("parallel",)),
        name="mla_prep",
    )(proj, proj, cos4, sin4, kv_g, w_ukv_all)


def _attention_kernel(q_ref, k_ref, vt_ref, o_ref, q2_ref, s0_ref, s1_ref, m_ref, l_ref, acc_ref, *, tb):
    i = pl.program_id(2)
    hp = ATT_HEADS
    q = q_ref[...]
    lane = lax.broadcasted_iota(jnp.int32, q.shape, 1) % HEAD_W
    row_chunk = lax.broadcasted_iota(jnp.int32, q.shape, 0) // CHUNK
    q2_ref[0] = q
    q2_ref[1] = jnp.where(lane == QK_NOPE + QK_ROPE + row_chunk, jnp.ones_like(q), q)
    m_ref[...] = jnp.full(m_ref.shape, -jnp.inf, F32)
    l_ref[...] = jnp.zeros(l_ref.shape, F32)
    acc_ref[...] = jnp.zeros(acc_ref.shape, F32)

    def scores(j, s_ref):
        r0 = pl.multiple_of(j * tb, tb)
        sel = (j == i).astype(jnp.int32)
        for h in range(hp):
            cs = slice(h * HEAD_W, (h + 1) * HEAD_W)
            s_ref[h] = lax.dot_general(k_ref[pl.ds(r0, tb), cs], q2_ref[sel, :, cs], (((1,), (1,)), ((), ())),
                                       preferred_element_type=F32)

    def softmax_pv(j, s_ref):
        for h in range(hp):
            s = s_ref[h]
            m_old = m_ref[h]
            m_new = jnp.maximum(m_old, jnp.max(s, axis=0, keepdims=True))
            p = jnp.exp2(s - m_new)
            a = jnp.exp2(m_old - m_new)
            l_ref[h] = a * l_ref[h] + jnp.sum(p, axis=0, keepdims=True)
            acc_ref[h] = a * acc_ref[h] + jnp.dot(vt_ref[j, h], p.astype(BF16), preferred_element_type=F32)
            m_ref[h] = m_new

    scores(0, s0_ref)
    bufs = (s0_ref, s1_ref)
    U = ATT_BLOCKS_PER_TRIP

    def trip(g, carry):
        j = U * g
        for u in range(U):
            scores(j + u + 1, bufs[(u + 1) % 2])
            softmax_pv(j + u, bufs[u % 2])
        return carry

    trips = i // U
    lax.fori_loop(0, trips, trip, 0)
    base = U * trips
    for u in range(U):
        j = base + u

        @pl.when(j < i)
        def _():
            scores(j + 1, bufs[(u + 1) % 2])
            softmax_pv(j, bufs[u % 2])

        @pl.when(j == i)
        def _():
            softmax_pv(j, bufs[u % 2])

    for h in range(hp):
        o_ref[:, h * V_HEAD:(h + 1) * V_HEAD] = (acc_ref[h] / l_ref[h]).T


def _attention(q_cat, k_cat, v_t, B, S):
    T = B * S
    tb = v_t.shape[-1]
    nq = S // tb
    hp = ATT_HEADS
    resident = dict(pipeline_mode=pl.Buffered(1))
    return pl.pallas_call(
        functools.partial(_attention_kernel, tb=tb),
        grid=(B, MLA_HEADS // hp, nq),
        in_specs=[pl.BlockSpec((tb, hp * HEAD_W), lambda b, h, i: (b * nq + i, h)),
                  pl.BlockSpec((S, hp * HEAD_W), lambda b, h, i: (b, h), **resident),
                  pl.BlockSpec((nq, hp, V_HEAD, tb), lambda b, h, i: (b, h, 0, 0), **resident)],
        out_specs=pl.BlockSpec((tb, hp * V_HEAD), lambda b, h, i: (b * nq + i, h)),
        out_shape=jax.ShapeDtypeStruct((T, MLA_HEADS * V_HEAD), F32),
        scratch_shapes=[pltpu.VMEM((2, tb, hp * HEAD_W), BF16),
                        pltpu.VMEM((hp, tb, tb), F32), pltpu.VMEM((hp, tb, tb), F32),
                        pltpu.VMEM((hp, 1, tb), F32), pltpu.VMEM((hp, 1, tb), F32),
                        pltpu.VMEM((hp, V_HEAD, tb), F32)],
        compiler_params=_cparams(("parallel", "parallel", "arbitrary")),
        name="attention",
    )(q_cat, k_cat, v_t)


def _gelu(x):
    return 0.5 * x * (1.0 + lax.erf(x * (1.0 / math.sqrt(2.0))))


def _mixers_kernel(cb_ref, cc_ref, ch_ref, ph_ref, gu_ref, gv_ref, hcc_ref, hch_ref, hph_ref,
                   convw_ref, poolw_ref, lng_ref, lnb_ref, sguw_ref, sgub_ref, gain_ref,
                   yc_ref, yp_ref, ys_ref,
                   g_scr, e_scr, a_scr, b_scr, y_scr, *, tm, tiles_per_seq):
    i = pl.program_id(0)
    t0 = (i % tiles_per_seq) * tm
    keep = jnp.where(t0 == 0, 0.0, 1.0).astype(F32)

    g_scr[0:HALO, :] = hcc_ref[...].astype(F32) * hch_ref[...].astype(F32) * keep
    g_scr[HALO:, :] = cc_ref[...].astype(F32) * ch_ref[...].astype(F32)
    conv = convw_ref[CONV_W - 1:CONV_W, :] * g_scr[HALO:HALO + tm, :]
    for j in range(CONV_W - 1):
        off = HALO - (CONV_W - 1) + j
        conv = conv + convw_ref[j:j + 1, :] * g_scr[off:off + tm, :]
    yc = cb_ref[...].astype(F32) * conv
    yc_ref[...] = (_rms_rows(yc) * gain_ref[:, GROUP_W:2 * GROUP_W]).astype(BF16)

    e_scr[0:HALO, :] = hph_ref[...].astype(F32) * keep
    e_scr[HALO:, :] = ph_ref[...].astype(F32)
    n = tm + HALO
    pos = (t0 + 1 + lax.broadcasted_iota(jnp.int32, (tm, 1), 0)).astype(F32)
    for g, w in enumerate(POOL_WINDOWS):
        cs = slice(g * POOL_CH, (g + 1) * POOL_CH)
        levels = g + 1
        src = None
        bufs = (a_scr, b_scr)
        total = None
        for lev in range(1, levels + 1):
            shift = 1 << (lev - 1)
            lo = 8 * lev if lev < levels else HALO
            if src is None:
                cur = e_scr[lo:n, cs] + e_scr[lo - shift:n - shift, cs]
            else:
                cur = src[lo:n, :] + src[lo - shift:n - shift, :]
            if lev < levels:
                dst = bufs[(lev - 1) % 2]
                dst[lo:n, :] = cur
                src = dst
            else:
                total = cur
        inv = 1.0 / jnp.minimum(pos, float(w))
        pooled = total * inv - e_scr[HALO:, cs]
        y_scr[:, cs] = jnp.dot(pooled.astype(BF16), poolw_ref[g], preferred_element_type=F32)
    yp = y_scr[...]
    yp_ref[...] = (_rms_rows(yp) * gain_ref[:, 2 * GROUP_W:3 * GROUP_W]).astype(BF16)

    u = _gelu(gu_ref[...].astype(F32))
    vn = _layer_norm_rows(_gelu(gv_ref[...].astype(F32)), lng_ref[...], lnb_ref[...]).astype(BF16)
    row = lax.broadcasted_iota(jnp.int32, (SGU_LEN, SGU_LEN), 0)
    col = lax.broadcasted_iota(jnp.int32, (SGU_LEN, SGU_LEN), 1)
    for g in range(SGU_GROUPS):
        cs = slice(g * SGU_CH, (g + 1) * SGU_CH)
        wg = jnp.where(col <= row, sguw_ref[g], 0.0).astype(BF16)
        bias = sgub_ref[:, g:g + 1]
        for c in range(tm // SGU_LEN):
            rs = slice(c * SGU_LEN, (c + 1) * SGU_LEN)
            mixed = jnp.dot(wg, vn[rs, cs], preferred_element_type=F32) + bias
            y_scr[rs, cs] = u[rs, cs] * mixed
    ysg = y_scr[...]
    ys_ref[...] = (_rms_rows(ysg) * gain_ref[:, 3 * GROUP_W:4 * GROUP_W]).astype(BF16)


def _mixers(proj, conv_w, pool_w, ln_g, ln_b, sgu_w, sgu_bt, gain, S):
    T = proj.shape[0]
    tm = min(TM_MIX, S)
    tiles_per_seq = S // tm
    hb = tm // HALO

    def col(c):
        return pl.BlockSpec((tm, GROUP_W), lambda i: (i, c))

    def halo(c):
        return pl.BlockSpec((HALO, GROUP_W), lambda i: (jnp.maximum(i * hb - 1, 0), c))

    def full(a):
        return pl.BlockSpec(a.shape, lambda i: (0,) * a.ndim)

    out = jax.ShapeDtypeStruct((T, GROUP_W), BF16)
    return pl.pallas_call(
        functools.partial(_mixers_kernel, tm=tm, tiles_per_seq=tiles_per_seq),
        grid=(T // tm,),
        in_specs=[col(COL_CB), col(COL_CC), col(COL_CH), col(COL_PH), col(COL_GU), col(COL_GV),
                  halo(COL_CC), halo(COL_CH), halo(COL_PH),
                  full(conv_w), full(pool_w), full(ln_g), full(ln_b), full(sgu_w), full(sgu_bt), full(gain)],
        out_specs=[pl.BlockSpec((tm, GROUP_W), lambda i: (i, 0))] * 3,
        out_shape=[out, out, out],
        scratch_shapes=[pltpu.VMEM((tm + HALO, GROUP_W), F32), pltpu.VMEM((tm + HALO, GROUP_W), F32),
                        pltpu.VMEM((tm + HALO, POOL_CH), F32), pltpu.VMEM((tm + HALO, POOL_CH), F32),
                        pltpu.VMEM((tm, GROUP_W), F32)],
        compiler_params=_cparams(("parallel",)),
        name="mixers",
    )(proj, proj, proj, proj, proj, proj, proj, proj, proj,
      conv_w, pool_w, ln_g, ln_b, sgu_w, sgu_bt, gain)


def _out_proj_kernel(*refs, with_router):
    if with_router:
        (o_ref, yc_ref, yp_ref, ys_ref, gain_ref, wo_ref, x_ref, g_ref, b_ref, rw_ref,
         x1_ref, x1b_ref, lg_ref) = refs
    else:
        (o_ref, yc_ref, yp_ref, ys_ref, gain_ref, wo_ref, x_ref, g_ref, b_ref,
         x1_ref, x1b_ref) = refs
    hr = o_ref.shape[0] // OUT_PARTS
    for part in range(OUT_PARTS):
        rs = slice(part * hr, (part + 1) * hr)
        ym = (_rms_rows(o_ref[rs, :]) * gain_ref[:, 0:GROUP_W]).astype(BF16)
        mix = jnp.dot(ym, wo_ref[0:GROUP_W, :], preferred_element_type=F32)
        for gi, y_ref in enumerate((yc_ref, yp_ref, ys_ref), start=1):
            mix = mix + jnp.dot(y_ref[rs, :], wo_ref[gi * GROUP_W:(gi + 1) * GROUP_W, :], preferred_element_type=F32)
        x1 = _layer_norm_rows(ALPHA * x_ref[rs, :] + mix, g_ref[...], b_ref[...])
        x1_ref[rs, :] = x1
        x1_hi = x1.astype(BF16)
        x1b_ref[rs, :] = x1_hi
        if with_router:
            x1_lo = (x1 - x1_hi.astype(F32)).astype(BF16)
            lg_ref[rs, :] = (jnp.dot(x1_hi, rw_ref[0], preferred_element_type=F32)
                             + (jnp.dot(x1_hi, rw_ref[1], preferred_element_type=F32)
                                + jnp.dot(x1_lo, rw_ref[0], preferred_element_type=F32)))


def _out_proj(o_mla, yc, yp, ys, gain, wo_all, l, x, ln_g, ln_b, router_hl=None):
    T, D = x.shape
    tm = min(TM_OUT, T)
    with_router = router_hl is not None

    def rows(w):
        return pl.BlockSpec((tm, w), lambda i: (i, 0))

    def full(a):
        return pl.BlockSpec(a.shape, lambda i: (0,) * a.ndim)

    wo_spec = pl.BlockSpec((None,) + wo_all.shape[1:], lambda i: (l, 0, 0))
    in_specs = [rows(GROUP_W)] * 4 + [full(gain), wo_spec, rows(D), full(ln_g), full(ln_b)]
    args = [o_mla, yc, yp, ys, gain, wo_all, x, ln_g, ln_b]
    out_specs = [rows(D), rows(D)]
    out_shape = [jax.ShapeDtypeStruct((T, D), F32), jax.ShapeDtypeStruct((T, D), BF16)]
    if with_router:
        in_specs.append(full(router_hl))
        args.append(router_hl)
        out_specs.append(rows(LANES))
        out_shape.append(jax.ShapeDtypeStruct((T, LANES), F32))
    return pl.pallas_call(
        functools.partial(_out_proj_kernel, with_router=with_router),
        grid=(T // tm,),
        in_specs=in_specs,
        out_specs=out_specs,
        out_shape=out_shape,
        compiler_params=_cparams(("parallel",)),
        name="out_proj",
    )(*args)


def _swiglu(xb, wg, wu):
    hg = jnp.dot(xb, wg, preferred_element_type=F32)
    hu = jnp.dot(xb, wu, preferred_element_type=F32)
    return (hg * jax.nn.sigmoid(hg) * hu).astype(BF16)


def _dense_up_kernel(*refs, n_cast):
    x_ref, wg_ref, wu_ref = refs[:3]
    cast_in = refs[3:3 + n_cast]
    h_ref = refs[3 + n_cast]
    cast_out = refs[4 + n_cast:4 + 2 * n_cast]
    for src_ref, dst_ref in zip(cast_in, cast_out):
        dst_ref[...] = src_ref[...].astype(dst_ref.dtype)
    h_ref[...] = _swiglu(x_ref[...], wg_ref[...], wu_ref[...])


def _dense_down_kernel(h_ref, wd_ref, xres_ref, g_ref, b_ref, o_ref, ob_ref):
    y = jnp.dot(h_ref[...], wd_ref[...], preferred_element_type=F32)
    x2 = _layer_norm_rows(ALPHA * xres_ref[...] + y, g_ref[...], b_ref[...])
    o_ref[...] = x2
    ob_ref[...] = x2.astype(BF16)


def _ffn_dense(xb, wg, wu, wd, e, xres, ln_g, ln_b, cast=()):
    T, D = xb.shape
    F = wg.shape[2]
    tm = min(TM_DENSE, T)
    nb = T // tm
    fh = F // MOE_F_SPLIT
    resident = dict(pipeline_mode=pl.Buffered(1))
    in_specs = [pl.BlockSpec((tm, D), lambda f, i: (i, 0)),
                pl.BlockSpec((None, D, fh), lambda f, i: (e, 0, f), **resident),
                pl.BlockSpec((None, D, fh), lambda f, i: (e, 0, f), **resident)]
    out_specs = [pl.BlockSpec((tm, fh), lambda f, i: (i, f))]
    out_shape = [jax.ShapeDtypeStruct((T, F), BF16)]
    steps = MOE_F_SPLIT * nb
    for arr, part, n_parts in cast:
        rows, cols = arr.shape[0] // n_parts, arr.shape[1]
        rb = rows // steps
        assert rb * steps == rows and rb % 16 == 0, (rows, steps)
        in_specs.append(pl.BlockSpec((rb, cols), lambda f, i, off=part * steps: (off + f * nb + i, 0)))
        out_specs.append(pl.BlockSpec((rb, cols), lambda f, i: (f * nb + i, 0)))
        out_shape.append(jax.ShapeDtypeStruct((rows, cols), BF16))
    h, *cast_out = pl.pallas_call(
        functools.partial(_dense_up_kernel, n_cast=len(cast)),
        grid=(MOE_F_SPLIT, nb),
        in_specs=in_specs,
        out_specs=out_specs,
        out_shape=out_shape,
        compiler_params=_cparams(("arbitrary", "arbitrary")),
        name="dense_up",
    )(xb, wg, wu, *[c[0] for c in cast])
    rows_spec = pl.BlockSpec((tm, D), lambda i: (i, 0))
    vec = pl.BlockSpec((1, D), lambda i: (0, 0))
    x2, x2b = pl.pallas_call(
        _dense_down_kernel,
        grid=(nb,),
        in_specs=[pl.BlockSpec((tm, F), lambda i: (i, 0)),
                  pl.BlockSpec((None, F, D), lambda i: (e, 0, 0), **resident), rows_spec, vec, vec],
        out_specs=[rows_spec, rows_spec],
        out_shape=[jax.ShapeDtypeStruct((T, D), F32), jax.ShapeDtypeStruct((T, D), BF16)],
        compiler_params=_cparams(("arbitrary",)),
        name="dense_down",
    )(h, wd, xres, ln_g, ln_b)
    return (x2, x2b, *cast_out)


def _moe_up_kernel(be_ref, nv_ref, x_ref, wg_ref, wu_ref, h_ref):
    i = pl.program_id(1)

    @pl.when(i < nv_ref[0])
    def _():
        h_ref[...] = _swiglu(x_ref[...], wg_ref[...], wu_ref[...])

    @pl.when(i >= nv_ref[0])
    def _():
        h_ref[...] = jnp.zeros(h_ref.shape, h_ref.dtype)


def _moe_down_kernel(be_ref, nv_ref, *refs):
    h_ref, wd_ref, o_ref = refs[0], refs[1], refs[-1]
    i = pl.program_id(0)

    @pl.when(i < nv_ref[0])
    def _():
        o_ref[...] = jnp.dot(h_ref[...], wd_ref[...], preferred_element_type=F32).astype(o_ref.dtype)

    @pl.when(i >= nv_ref[0])
    def _():
        o_ref[...] = jnp.zeros(o_ref.shape, o_ref.dtype)


def _ffn_moe(x_rows_b, wg, wu, wd, block_e, n_valid, y_buf, blk0, total_rows):
    Rc, D = x_rows_b.shape
    F = wg.shape[2]
    tm = min(TM_FFN, Rc)
    nb = Rc // tm
    fh = F // MOE_F_SPLIT
    resident = dict(pipeline_mode=pl.Buffered(1))
    h = pl.pallas_call(
        _moe_up_kernel,
        grid_spec=pltpu.PrefetchScalarGridSpec(
            num_scalar_prefetch=2,
            grid=(MOE_F_SPLIT, nb),
            in_specs=[pl.BlockSpec((tm, D), lambda f, i, be, nv: (i, 0)),
                      pl.BlockSpec((None, D, fh), lambda f, i, be, nv: (be[i], 0, f), **resident),
                      pl.BlockSpec((None, D, fh), lambda f, i, be, nv: (be[i], 0, f), **resident)],
            out_specs=pl.BlockSpec((tm, fh), lambda f, i, be, nv: (i, f))),
        out_shape=jax.ShapeDtypeStruct((Rc, F), BF16),
        compiler_params=_cparams(("arbitrary", "arbitrary")),
        name="moe_up",
    )(block_e, n_valid, x_rows_b, wg, wu)
    in_specs = [pl.BlockSpec((tm, F), lambda i, be, nv: (i, 0)),
                pl.BlockSpec((None, F, D), lambda i, be, nv: (be[i], 0, 0), **resident)]
    args = [block_e, n_valid, h, wd]
    aliases = {}
    if y_buf is not None:
        in_specs.append(pl.BlockSpec(memory_space=pl.ANY))
        args.append(y_buf)
        aliases = {len(args) - 1: 0}
    return pl.pallas_call(
        _moe_down_kernel,
        grid_spec=pltpu.PrefetchScalarGridSpec(
            num_scalar_prefetch=2,
            grid=(nb,),
            in_specs=in_specs,
            out_specs=pl.BlockSpec((tm, D), lambda i, be, nv: (blk0 + i, 0))),
        out_shape=jax.ShapeDtypeStruct((total_rows, D), BF16),
        input_output_aliases=aliases,
        compiler_params=_cparams(("arbitrary",)),
        name="moe_down",
    )(*args)


def _combine_kernel(x_ref, y0_ref, y1_ref, g0_ref, g1_ref, g_ref, b_ref, *refs):
    o_ref, ob_ref = refs[-2:]
    ffn = y0_ref[...].astype(F32) * g0_ref[...] + y1_ref[...].astype(F32) * g1_ref[...]
    x2 = _layer_norm_rows(ALPHA * x_ref[...] + ffn, g_ref[...], b_ref[...])
    o_ref[...] = x2
    ob_ref[...] = x2.astype(BF16)


def _combine(x, y0, y1, g0, g1, ln_g, ln_b, part, bufs):
    T, D = x.shape
    Tp = y0.shape[0]
    tm = min(TM_LN, Tp)
    nb = Tp // tm
    rows_x = pl.BlockSpec((tm, D), lambda i: (part * nb + i, 0))
    rows = pl.BlockSpec((tm, D), lambda i: (i, 0))
    gate = pl.BlockSpec((tm, 1), lambda i: (i, 0))
    vec = pl.BlockSpec((1, D), lambda i: (0, 0))
    in_specs = [rows_x, rows, rows, gate, gate, vec, vec]
    args = [x, y0, y1, g0, g1, ln_g, ln_b]
    aliases = {}
    if bufs is not None:
        in_specs += [pl.BlockSpec(memory_space=pl.ANY)] * 2
        args += list(bufs)
        aliases = {len(args) - 2: 0, len(args) - 1: 1}
    return pl.pallas_call(
        _combine_kernel,
        grid=(nb,),
        in_specs=in_specs,
        out_specs=[rows_x, rows_x],
        out_shape=[jax.ShapeDtypeStruct((T, D), F32), jax.ShapeDtypeStruct((T, D), BF16)],
        input_output_aliases=aliases,
        compiler_params=_cparams(("parallel",)),
        name="moe_combine",
    )(*args)


def _pad_w_in(w_in):
    L, D, _ = w_in.shape
    q_dim = MLA_HEADS * (QK_NOPE + QK_ROPE)
    wq = w_in[:, :, :q_dim].reshape(L, D, MLA_HEADS, QK_NOPE + QK_ROPE)
    kr0 = q_dim + KV_RANK
    w_kr = w_in[:, :, kr0:kr0 + QK_ROPE]
    pad = jnp.zeros((L, D, MLA_HEADS, HEAD_W - QK_NOPE - QK_ROPE), w_in.dtype).at[:, :, 0, :].set(w_kr)
    q_part = jnp.concatenate([wq, pad], axis=-1).reshape(L, D, Q_COLS)
    rest = jnp.concatenate([w_in[:, :, q_dim:kr0], w_in[:, :, kr0 + QK_ROPE:]], axis=-1)
    return jnp.concatenate([q_part, rest], axis=-1).astype(BF16)


def _rope_tables(positions):
    inv_freq = ROPE_THETA ** (-jnp.arange(0, QK_ROPE, 2, dtype=F32) / QK_ROPE)
    ang = positions.astype(F32).reshape(-1)[:, None] * inv_freq
    cos, sin = jnp.cos(ang), jnp.sin(ang)
    cos4 = jnp.concatenate([cos, cos, cos, cos], axis=-1)
    sin4 = jnp.concatenate([-sin, sin, -sin, sin], axis=-1)
    return cos4, sin4


def _route(logits, tm):
    T = logits.shape[0]
    A = T * TOP_K
    top_logit, top_e = lax.top_k(logits, TOP_K)
    gates = jax.nn.softmax(top_logit, axis=-1)
    flat_e = top_e.reshape(A)
    onehot = (flat_e[:, None] == jnp.arange(N_EXPERTS, dtype=flat_e.dtype)[None, :]).astype(jnp.int32)
    csum = jnp.cumsum(onehot, axis=0)
    counts = csum[-1]
    rank = jnp.take_along_axis(csum, flat_e[:, None], axis=1)[:, 0] - 1
    padded = (counts + tm - 1) // tm * tm
    pad_ends = jnp.cumsum(padded)
    pad_starts = pad_ends - padded
    dest = (pad_starts[flat_e] + rank).astype(jnp.int32)
    n_blocks = -(-(A + N_EXPERTS * (tm - 1)) // tm)
    n_pad = n_blocks * tm
    block_start = jnp.arange(n_blocks, dtype=jnp.int32) * tm
    block_e = jnp.minimum(jnp.searchsorted(pad_ends, block_start, side='right'), N_EXPERTS - 1).astype(jnp.int32)
    arange_a = jnp.arange(A, dtype=jnp.int32)
    _, order = lax.sort_key_val(flat_e.astype(jnp.int32) * A + arange_a, arange_a)
    starts = jnp.cumsum(counts) - counts
    e_s = jnp.repeat(block_e, tm)
    r = jnp.arange(n_pad, dtype=jnp.int32) - pad_starts[e_s].astype(jnp.int32)
    src = jnp.clip(starts[e_s].astype(jnp.int32) + r, 0, A - 1)
    tok_sorted = order // TOP_K
    slot_tok = jnp.where(r < counts[e_s], tok_sorted.at[src].get(mode="promise_in_bounds"), 0).astype(jnp.int32)
    n_valid = (pad_ends[-1] // tm).astype(jnp.int32).reshape(1)
    return dest.reshape(T, TOP_K), gates, slot_tok, block_e, n_valid


def kernel(x, positions, w_in, kv_norm_g, w_ukv, conv_w, pool_w, sgu_ln_g, sgu_ln_b, sgu_w, sgu_b, mix_gain, w_o, ln1_g, ln1_b, ffn_wg, ffn_wu, ffn_wd, router_w, exp_wg, exp_wu, exp_wd, ln2_g, ln2_b):
    B, S, D = x.shape
    T = B * S
    L = w_in.shape[0]
    cos4, sin4 = _rope_tables(positions)
    w_in_b = _pad_w_in(w_in)
    w_ukv_b = w_ukv.astype(BF16)
    w_o_b = w_o.astype(BF16)
    pool_w_b = pool_w.astype(BF16)
    sgu_bt = jnp.swapaxes(sgu_b, 1, 2)
    router_pad = jnp.pad(router_w, ((0, 0), (0, 0), (0, LANES - N_EXPERTS)))
    router_hi = router_pad.astype(BF16)
    router_hl = jnp.stack([router_hi, (router_pad - router_hi.astype(F32)).astype(BF16)], axis=1)
    ffn_w = (ffn_wg.astype(BF16), ffn_wu.astype(BF16), ffn_wd.astype(BF16))
    n_moe = exp_wg.shape[0]
    exp_f32 = tuple(w.reshape(-1, w.shape[-1]) for w in (exp_wg, exp_wu, exp_wd))
    exp_w = None

    xf = x.reshape(T, D)
    xb = xf.astype(BF16)
    tm_moe = min(TM_FFN, T)
    for l in range(L):
        proj = _in_proj(xb, w_in_b, l)
        q_cat, k_cat, v_t = _mla_prep(proj, cos4, sin4, kv_norm_g[l][None, :], w_ukv_b, l)
        o_mla = _attention(q_cat, k_cat, v_t, B, S)
        yc, yp, ys = _mixers(proj, conv_w[l], pool_w_b[l], sgu_ln_g[l][None, :], sgu_ln_b[l][None, :],
                             sgu_w[l], sgu_bt[l], mix_gain[l][None, :], S)
        ln1 = (ln1_g[l][None, :], ln1_b[l][None, :])
        ln2 = (ln2_g[l][None, :], ln2_b[l][None, :])
        e = l // 2
        if l % 2 == 0:
            x1, x1b = _out_proj(o_mla, yc, yp, ys, mix_gain[l][None, :], w_o_b, l, xf, *ln1)
            cast = [(w, e, n_moe) for w in exp_f32] if e < n_moe else []
            xf, xb, *cast_out = _ffn_dense(x1b, *ffn_w, e, x1, *ln2, cast=cast)
            if cast_out:
                exp_w = tuple(c.reshape((N_EXPERTS, -1, c.shape[-1])) for c in cast_out)
        else:
            x1, x1b, logits = _out_proj(o_mla, yc, yp, ys, mix_gain[l][None, :], w_o_b, l, xf, *ln1,
                                        router_hl=router_hl[e])
            dest, gates, slot_tok, block_e, n_valid = _route(logits[:, :N_EXPERTS], tm_moe)
            n_blocks = block_e.shape[0]
            chunks = max(c for c in (MOE_CHUNKS, 2, 1) if n_blocks % c == 0)
            cb = n_blocks // chunks
            y_rows = None
            for c in range(chunks):
                x_rows = x1b.at[slot_tok[c * cb * tm_moe:(c + 1) * cb * tm_moe]].get(mode="promise_in_bounds")
                y_rows = _ffn_moe(x_rows, *exp_w, block_e[c * cb:(c + 1) * cb],
                                  jnp.clip(n_valid - c * cb, 0, cb), y_rows, c * cb, n_blocks * tm_moe)
            parts = COMBINE_PARTS if T % (COMBINE_PARTS * TM_LN) == 0 else 1
            tp = T // parts
            bufs = None
            for p in range(parts):
                ts = slice(p * tp, (p + 1) * tp)
                y0 = y_rows.at[dest[ts, 0]].get(mode="promise_in_bounds")
                y1 = y_rows.at[dest[ts, 1]].get(mode="promise_in_bounds")
                bufs = _combine(x1, y0, y1, gates[ts, 0:1], gates[ts, 1:2], *ln2, p, bufs)
            xf, xb = bufs
    return xf.reshape(B, S, D)
```

```python
import functools
import math

import jax
import jax.numpy as jnp
from jax import lax
from jax.experimental import pallas as pl
from jax.experimental.pallas import tpu as pltpu

F32 = jnp.float32
BF16 = jnp.bfloat16

CHUNK = 64
GROUP_W = 512
MLA_HEADS = 4
QK_NOPE = 128
QK_ROPE = 64
V_HEAD = 128
KV_RANK = 512
ROPE_THETA = 10000.0
CONV_W = 3
POOL_WINDOWS = (2, 4, 8, 16)
POOL_CH = 128
SGU_LEN = 128
SGU_GROUPS = 4
SGU_CH = 128
N_EXPERTS = 8
TOP_K = 2
DEPTH = 4
ALPHA = (2.0 * DEPTH) ** 0.25
LN_EPS = 1e-5
RMS_EPS = 1e-6

LANES = 128
HEAD_W = 2 * LANES
VMEM_LIMIT = 56 * 1024 * 1024

Q_COLS = MLA_HEADS * HEAD_W
COL_CKV, COL_CB, COL_CC, COL_CH, COL_PH, COL_GU, COL_GV = 2, 3, 4, 5, 6, 7, 8

HALO = 32

TM_PROJ, TN_PROJ = 1024, 2304
TB_ATT = 1024
ATT_HEADS = 2
ATT_BLOCKS_PER_TRIP = 2
TM_PREP = TB_ATT
MASK_BIG = 2.0 ** 100
TM_MIX = 512
TM_OUT, OUT_PARTS = 512, 2
TM_LN = 256
TM_FFN = 512
TM_DENSE = 256
MOE_F_SPLIT = 2
MOE_CHUNKS = 4
COMBINE_PARTS = 2


def _cparams(sem):
    return pltpu.CompilerParams(dimension_semantics=sem, vmem_limit_bytes=VMEM_LIMIT)


def _layer_norm_rows(z, g, b):
    mu = jnp.mean(z, axis=-1, keepdims=True)
    zc = z - mu
    var = jnp.mean(zc * zc, axis=-1, keepdims=True)
    return zc * lax.rsqrt(var + LN_EPS) * g + b


def _rms_rows(y):
    return y * lax.rsqrt(jnp.mean(y * y, axis=-1, keepdims=True) + RMS_EPS)


def _matmul_kernel(x_ref, w_ref, o_ref):
    o_ref[...] = jnp.dot(x_ref[...].astype(BF16), w_ref[...], preferred_element_type=F32).astype(o_ref.dtype)


def _in_proj(xb, w_all, l):
    T, K = xb.shape
    N = w_all.shape[2]
    tm = min(TM_PROJ, T)
    return pl.pallas_call(
        _matmul_kernel,
        grid=(T // tm, N // TN_PROJ),
        in_specs=[pl.BlockSpec((tm, K), lambda i, j: (i, 0)),
                  pl.BlockSpec((None, K, TN_PROJ), lambda i, j: (l, 0, j))],
        out_specs=pl.BlockSpec((tm, TN_PROJ), lambda i, j: (i, j)),
        out_shape=jax.ShapeDtypeStruct((T, N), BF16),
        compiler_params=_cparams(("parallel", "arbitrary")),
        name="in_proj",
    )(xb, w_all)


def _mla_prep_kernel(q_ref, ckv_ref, cos_ref, sin_ref, g_ref, wukv_ref, qt_ref, ko_ref, vt_ref, *, qscale):
    cos = cos_ref[...]
    sin = sin_ref[...]
    lane = lax.broadcasted_iota(jnp.int32, cos.shape, 1)
    row_chunk = lax.broadcasted_iota(jnp.int32, cos.shape, 0) // CHUNK
    first_half = (lane & (QK_ROPE // 2)) == 0
    low = lane < QK_ROPE

    def rope(r):
        partner = jnp.where(first_half, pltpu.roll(r, LANES - QK_ROPE // 2, axis=1), pltpu.roll(r, QK_ROPE // 2, axis=1))
        return r * cos + partner * sin

    k_rope = None
    for h in range(MLA_HEADS):
        c0 = h * HEAD_W
        qt_ref[h, 0:LANES, :] = (q_ref[:, c0:c0 + LANES].astype(F32) * qscale).T.astype(BF16)
        rr = rope(q_ref[:, c0 + LANES:c0 + HEAD_W].astype(F32))
        qt_ref[h, LANES:HEAD_W, :] = jnp.where(low, rr * qscale, 0.0).T.astype(BF16)
        if h == 0:
            stair = jnp.where(row_chunk > lane - QK_ROPE, -MASK_BIG, 0.0)
            k_rope = jnp.where(low, pltpu.roll(rr, QK_ROPE, axis=1), stair).astype(BF16)

    c = ckv_ref[...].astype(F32)
    cn = (_rms_rows(c) * g_ref[...]).astype(BF16)
    kv = jnp.dot(cn, wukv_ref[...], preferred_element_type=F32)
    for h in range(MLA_HEADS):
        c0 = h * HEAD_W
        ko_ref[:, c0:c0 + LANES] = kv[:, c0:c0 + LANES].astype(BF16)
        ko_ref[:, c0 + LANES:c0 + HEAD_W] = k_rope
        vt_ref[h] = kv[:, c0 + LANES:c0 + HEAD_W].T.astype(BF16)


def _mla_prep(proj, cos4, sin4, kv_g, w_ukv_all, l):
    T = proj.shape[0]
    tm = min(TM_PREP, T)
    qscale = math.log2(math.e) / math.sqrt(QK_NOPE + QK_ROPE)
    return pl.pallas_call(
        functools.partial(_mla_prep_kernel, qscale=qscale),
        grid=(T // tm,),
        in_specs=[pl.BlockSpec((tm, Q_COLS), lambda i: (i, 0)),
                  pl.BlockSpec((tm, GROUP_W), lambda i: (i, COL_CKV)),
                  pl.BlockSpec((tm, LANES), lambda i: (i, 0)),
                  pl.BlockSpec((tm, LANES), lambda i: (i, 0)),
                  pl.BlockSpec((1, KV_RANK), lambda i: (0, 0)),
                  pl.BlockSpec((None,) + w_ukv_all.shape[1:], lambda i: (l, 0, 0))],
        out_specs=[pl.BlockSpec((None, MLA_HEADS, HEAD_W, tm), lambda i: (i, 0, 0, 0)),
                   pl.BlockSpec((tm, Q_COLS), lambda i: (i, 0)),
                   pl.BlockSpec((None, MLA_HEADS, V_HEAD, tm), lambda i: (i, 0, 0, 0))],
        out_shape=[jax.ShapeDtypeStruct((T // tm, MLA_HEADS, HEAD_W, tm), BF16),
                   jax.ShapeDtypeStruct((T, Q_COLS), BF16),
                   jax.ShapeDtypeStruct((T // tm, MLA_HEADS, V_HEAD, tm), BF16)],
        compiler_params=_cparams(("parallel",)),
        name="mla_prep",
    )(proj, proj, cos4, sin4, kv_g, w_ukv_all)


def _attention_kernel(q_ref, k_ref, vt_ref, o_ref, q2_ref, s0_ref, s1_ref, m_ref, l_ref, acc_ref, *, tb):
    i = pl.program_id(2)
    hp = ATT_HEADS
    q = q_ref[...]
    dim = lax.broadcasted_iota(jnp.int32, q.shape, 1)
    query_chunk = lax.broadcasted_iota(jnp.int32, q.shape, 2) // CHUNK
    q2_ref[0] = q
    q2_ref[1] = jnp.where(dim == QK_NOPE + QK_ROPE + query_chunk, jnp.ones_like(q), q)
    m_ref[...] = jnp.full(m_ref.shape, -jnp.inf, F32)
    l_ref[...] = jnp.zeros(l_ref.shape, F32)
    acc_ref[...] = jnp.zeros(acc_ref.shape, F32)

    def scores(j, s_ref):
        r0 = pl.multiple_of(j * tb, tb)
        sel = (j == i).astype(jnp.int32)
        for h in range(hp):
            cs = slice(h * HEAD_W, (h + 1) * HEAD_W)
            s_ref[h] = jnp.dot(k_ref[pl.ds(r0, tb), cs], q2_ref[sel, h],
                               preferred_element_type=F32)

    def softmax_pv(j, s_ref):
        for h in range(hp):
            s = s_ref[h]
            m_old = m_ref[h]
            m_new = jnp.maximum(m_old, jnp.max(s, axis=0, keepdims=True))
            p = jnp.exp2(s - m_new)
            a = jnp.exp2(m_old - m_new)
            l_ref[h] = a * l_ref[h] + jnp.sum(p, axis=0, keepdims=True)
            acc_ref[h] = a * acc_ref[h] + jnp.dot(vt_ref[j, h], p.astype(BF16), preferred_element_type=F32)
            m_ref[h] = m_new

    scores(0, s0_ref)
    bufs = (s0_ref, s1_ref)
    U = ATT_BLOCKS_PER_TRIP

    def trip(g, carry):
        j = U * g
        for u in range(U):
            scores(j + u + 1, bufs[(u + 1) % 2])
            softmax_pv(j + u, bufs[u % 2])
        return carry

    trips = i // U
    lax.fori_loop(0, trips, trip, 0)
    base = U * trips
    for u in range(U):
        j = base + u

        @pl.when(j < i)
        def _():
            scores(j + 1, bufs[(u + 1) % 2])
            softmax_pv(j, bufs[u % 2])

        @pl.when(j == i)
        def _():
            softmax_pv(j, bufs[u % 2])

    for h in range(hp):
        o_ref[:, h * V_HEAD:(h + 1) * V_HEAD] = (acc_ref[h] / l_ref[h]).T


def _attention(q_t, k_cat, v_t, B, S):
    T = B * S
    tb = v_t.shape[-1]
    nq = S // tb
    hp = ATT_HEADS
    resident = dict(pipeline_mode=pl.Buffered(1))
    return pl.pallas_call(
        functools.partial(_attention_kernel, tb=tb),
        grid=(B, MLA_HEADS // hp, nq),
        in_specs=[pl.BlockSpec((None, hp, HEAD_W, tb), lambda b, h, i: (b * nq + i, h, 0, 0)),
                  pl.BlockSpec((S, hp * HEAD_W), lambda b, h, i: (b, h), **resident),
                  pl.BlockSpec((nq, hp, V_HEAD, tb), lambda b, h, i: (b, h, 0, 0), **resident)],
        out_specs=pl.BlockSpec((tb, hp * V_HEAD), lambda b, h, i: (b * nq + i, h)),
        out_shape=jax.ShapeDtypeStruct((T, MLA_HEADS * V_HEAD), F32),
        scratch_shapes=[pltpu.VMEM((2, hp, HEAD_W, tb), BF16),
                        pltpu.VMEM((hp, tb, tb), F32), pltpu.VMEM((hp, tb, tb), F32),
                        pltpu.VMEM((hp, 1, tb), F32), pltpu.VMEM((hp, 1, tb), F32),
                        pltpu.VMEM((hp, V_HEAD, tb), F32)],
        compiler_params=_cparams(("parallel", "parallel", "arbitrary")),
        name="attention",
    )(q_t, k_cat, v_t)


def _gelu(x):
    return 0.5 * x * (1.0 + lax.erf(x * (1.0 / math.sqrt(2.0))))


def _mixers_kernel(cb_ref, cc_ref, ch_ref, ph_ref, gu_ref, gv_ref, hcc_ref, hch_ref, hph_ref,
                   convw_ref, poolw_ref, lng_ref, lnb_ref, sguw_ref, sgub_ref, gain_ref,
                   yc_ref, yp_ref, ys_ref,
                   g_scr, e_scr, a_scr, b_scr, y_scr, *, tm, tiles_per_seq):
    i = pl.program_id(0)
    t0 = (i % tiles_per_seq) * tm
    keep = jnp.where(t0 == 0, 0.0, 1.0).astype(F32)

    g_scr[0:HALO, :] = hcc_ref[...].astype(F32) * hch_ref[...].astype(F32) * keep
    g_scr[HALO:, :] = cc_ref[...].astype(F32) * ch_ref[...].astype(F32)
    conv = convw_ref[CONV_W - 1:CONV_W, :] * g_scr[HALO:HALO + tm, :]
    for j in range(CONV_W - 1):
        off = HALO - (CONV_W - 1) + j
        conv = conv + convw_ref[j:j + 1, :] * g_scr[off:off + tm, :]
    yc = cb_ref[...].astype(F32) * conv
    yc_ref[...] = (_rms_rows(yc) * gain_ref[:, GROUP_W:2 * GROUP_W]).astype(BF16)

    e_scr[0:HALO, :] = hph_ref[...].astype(F32) * keep
    e_scr[HALO:, :] = ph_ref[...].astype(F32)
    n = tm + HALO
    pos = (t0 + 1 + lax.broadcasted_iota(jnp.int32, (tm, 1), 0)).astype(F32)
    for g, w in enumerate(POOL_WINDOWS):
        cs = slice(g * POOL_CH, (g + 1) * POOL_CH)
        levels = g + 1
        src = None
        bufs = (a_scr, b_scr)
        total = None
        for lev in range(1, levels + 1):
            shift = 1 << (lev - 1)
            lo = 8 * lev if lev < levels else HALO
            if src is None:
                cur = e_scr[lo:n, cs] + e_scr[lo - shift:n - shift, cs]
            else:
                cur = src[lo:n, :] + src[lo - shift:n - shift, :]
            if lev < levels:
                dst = bufs[(lev - 1) % 2]
                dst[lo:n, :] = cur
                src = dst
            else:
                total = cur
        inv = 1.0 / jnp.minimum(pos, float(w))
        pooled = total * inv - e_scr[HALO:, cs]
        y_scr[:, cs] = jnp.dot(pooled.astype(BF16), poolw_ref[g], preferred_element_type=F32)
    yp = y_scr[...]
    yp_ref[...] = (_rms_rows(yp) * gain_ref[:, 2 * GROUP_W:3 * GROUP_W]).astype(BF16)

    u = _gelu(gu_ref[...].astype(F32))
    vn = _layer_norm_rows(_gelu(gv_ref[...].astype(F32)), lng_ref[...], lnb_ref[...]).astype(BF16)
    row = lax.broadcasted_iota(jnp.int32, (SGU_LEN, SGU_LEN), 0)
    col = lax.broadcasted_iota(jnp.int32, (SGU_LEN, SGU_LEN), 1)
    for g in range(SGU_GROUPS):
        cs = slice(g * SGU_CH, (g + 1) * SGU_CH)
        wg = jnp.where(col <= row, sguw_ref[g], 0.0).astype(BF16)
        bias = sgub_ref[:, g:g + 1]
        for c in range(tm // SGU_LEN):
            rs = slice(c * SGU_LEN, (c + 1) * SGU_LEN)
            mixed = jnp.dot(wg, vn[rs, cs], preferred_element_type=F32) + bias
            y_scr[rs, cs] = u[rs, cs] * mixed
    ysg = y_scr[...]
    ys_ref[...] = (_rms_rows(ysg) * gain_ref[:, 3 * GROUP_W:4 * GROUP_W]).astype(BF16)


def _mixers(proj, conv_w, pool_w, ln_g, ln_b, sgu_w, sgu_bt, gain, S):
    T = proj.shape[0]
    tm = min(TM_MIX, S)
    tiles_per_seq = S // tm
    hb = tm // HALO

    def col(c):
        return pl.BlockSpec((tm, GROUP_W), lambda i: (i, c))

    def halo(c):
        return pl.BlockSpec((HALO, GROUP_W), lambda i: (jnp.maximum(i * hb - 1, 0), c))

    def full(a):
        return pl.BlockSpec(a.shape, lambda i: (0,) * a.ndim)

    out = jax.ShapeDtypeStruct((T, GROUP_W), BF16)
    return pl.pallas_call(
        functools.partial(_mixers_kernel, tm=tm, tiles_per_seq=tiles_per_seq),
        grid=(T // tm,),
        in_specs=[col(COL_CB), col(COL_CC), col(COL_CH), col(COL_PH), col(COL_GU), col(COL_GV),
                  halo(COL_CC), halo(COL_CH), halo(COL_PH),
                  full(conv_w), full(pool_w), full(ln_g), full(ln_b), full(sgu_w), full(sgu_bt), full(gain)],
        out_specs=[pl.BlockSpec((tm, GROUP_W), lambda i: (i, 0))] * 3,
        out_shape=[out, out, out],
        scratch_shapes=[pltpu.VMEM((tm + HALO, GROUP_W), F32), pltpu.VMEM((tm + HALO, GROUP_W), F32),
                        pltpu.VMEM((tm + HALO, POOL_CH), F32), pltpu.VMEM((tm + HALO, POOL_CH), F32),
                        pltpu.VMEM((tm, GROUP_W), F32)],
        compiler_params=_cparams(("parallel",)),
        name="mixers",
    )(proj, proj, proj, proj, proj, proj, proj, proj, proj,
      conv_w, pool_w, ln_g, ln_b, sgu_w, sgu_bt, gain)


def _out_proj_kernel(*refs, with_router):
    if with_router:
        (o_ref, yc_ref, yp_ref, ys_ref, gain_ref, wo_ref, x_ref, g_ref, b_ref, rw_ref,
         x1_ref, x1b_ref, lg_ref) = refs
    else:
        (o_ref, yc_ref, yp_ref, ys_ref, gain_ref, wo_ref, x_ref, g_ref, b_ref,
         x1_ref, x1b_ref) = refs
    hr = o_ref.shape[0] // OUT_PARTS
    for part in range(OUT_PARTS):
        rs = slice(part * hr, (part + 1) * hr)
        ym = (_rms_rows(o_ref[rs, :]) * gain_ref[:, 0:GROUP_W]).astype(BF16)
        mix = jnp.dot(ym, wo_ref[0:GROUP_W, :], preferred_element_type=F32)
        for gi, y_ref in enumerate((yc_ref, yp_ref, ys_ref), start=1):
            mix = mix + jnp.dot(y_ref[rs, :], wo_ref[gi * GROUP_W:(gi + 1) * GROUP_W, :], preferred_element_type=F32)
        x1 = _layer_norm_rows(ALPHA * x_ref[rs, :] + mix, g_ref[...], b_ref[...])
        x1_ref[rs, :] = x1
        x1_hi = x1.astype(BF16)
        x1b_ref[rs, :] = x1_hi
        if with_router:
            x1_lo = (x1 - x1_hi.astype(F32)).astype(BF16)
            hh_hl = jnp.dot(x1_hi, rw_ref[...], preferred_element_type=F32)
            lh = jnp.dot(x1_lo, rw_ref[:, 0:LANES], preferred_element_type=F32)
            lg_ref[rs, :] = hh_hl[:, 0:LANES] + (hh_hl[:, LANES:] + lh)


def _out_proj(o_mla, yc, yp, ys, gain, wo_all, l, x, ln_g, ln_b, router_hl=None):
    T, D = x.shape
    tm = min(TM_OUT, T)
    with_router = router_hl is not None

    def rows(w):
        return pl.BlockSpec((tm, w), lambda i: (i, 0))

    def full(a):
        return pl.BlockSpec(a.shape, lambda i: (0,) * a.ndim)

    wo_spec = pl.BlockSpec((None,) + wo_all.shape[1:], lambda i: (l, 0, 0))
    in_specs = [rows(GROUP_W)] * 4 + [full(gain), wo_spec, rows(D), full(ln_g), full(ln_b)]
    args = [o_mla, yc, yp, ys, gain, wo_all, x, ln_g, ln_b]
    out_specs = [rows(D), rows(D)]
    out_shape = [jax.ShapeDtypeStruct((T, D), F32), jax.ShapeDtypeStruct((T, D), BF16)]
    if with_router:
        in_specs.append(full(router_hl))
        args.append(router_hl)
        out_specs.append(rows(LANES))
        out_shape.append(jax.ShapeDtypeStruct((T, LANES), F32))
    return pl.pallas_call(
        functools.partial(_out_proj_kernel, with_router=with_router),
        grid=(T // tm,),
        in_specs=in_specs,
        out_specs=out_specs,
        out_shape=out_shape,
        compiler_params=_cparams(("parallel",)),
        name="out_proj",
    )(*args)


def _swiglu(xb, wg, wu):
    hg = jnp.dot(xb, wg, preferred_element_type=F32)
    hu = jnp.dot(xb, wu, preferred_element_type=F32)
    return (hg * jax.nn.sigmoid(hg) * hu).astype(BF16)


def _dense_up_kernel(*refs, n_cast):
    x_ref, wg_ref, wu_ref = refs[:3]
    cast_in = refs[3:3 + n_cast]
    h_ref = refs[3 + n_cast]
    cast_out = refs[4 + n_cast:4 + 2 * n_cast]
    for src_ref, dst_ref in zip(cast_in, cast_out):
        dst_ref[...] = src_ref[...].astype(dst_ref.dtype)
    h_ref[...] = _swiglu(x_ref[...], wg_ref[...], wu_ref[...])


def _dense_down_kernel(h_ref, wd_ref, xres_ref, g_ref, b_ref, o_ref, ob_ref):
    y = jnp.dot(h_ref[...], wd_ref[...], preferred_element_type=F32)
    x2 = _layer_norm_rows(ALPHA * xres_ref[...] + y, g_ref[...], b_ref[...])
    o_ref[...] = x2
    ob_ref[...] = x2.astype(BF16)


def _ffn_dense(xb, wg, wu, wd, e, xres, ln_g, ln_b, cast=()):
    T, D = xb.shape
    F = wg.shape[2]
    tm = min(TM_DENSE, T)
    nb = T // tm
    fh = F // MOE_F_SPLIT
    resident = dict(pipeline_mode=pl.Buffered(1))
    in_specs = [pl.BlockSpec((tm, D), lambda f, i: (i, 0)),
                pl.BlockSpec((None, D, fh), lambda f, i: (e, 0, f), **resident),
                pl.BlockSpec((None, D, fh), lambda f, i: (e, 0, f), **resident)]
    out_specs = [pl.BlockSpec((tm, fh), lambda f, i: (i, f))]
    out_shape = [jax.ShapeDtypeStruct((T, F), BF16)]
    steps = MOE_F_SPLIT * nb
    for arr, part, n_parts in cast:
        rows, cols = arr.shape[0] // n_parts, arr.shape[1]
        rb = rows // steps
        assert rb * steps == rows and rb % 16 == 0, (rows, steps)
        in_specs.append(pl.BlockSpec((rb, cols), lambda f, i, off=part * steps: (off + f * nb + i, 0)))
        out_specs.append(pl.BlockSpec((rb, cols), lambda f, i: (f * nb + i, 0)))
        out_shape.append(jax.ShapeDtypeStruct((rows, cols), BF16))
    h, *cast_out = pl.pallas_call(
        functools.partial(_dense_up_kernel, n_cast=len(cast)),
        grid=(MOE_F_SPLIT, nb),
        in_specs=in_specs,
        out_specs=out_specs,
        out_shape=out_shape,
        compiler_params=_cparams(("arbitrary", "arbitrary")),
        name="dense_up",
    )(xb, wg, wu, *[c[0] for c in cast])
    rows_spec = pl.BlockSpec((tm, D), lambda i: (i, 0))
    vec = pl.BlockSpec((1, D), lambda i: (0, 0))
    x2, x2b = pl.pallas_call(
        _dense_down_kernel,
        grid=(nb,),
        in_specs=[pl.BlockSpec((tm, F), lambda i: (i, 0)),
                  pl.BlockSpec((None, F, D), lambda i: (e, 0, 0), **resident), rows_spec, vec, vec],
        out_specs=[rows_spec, rows_spec],
        out_shape=[jax.ShapeDtypeStruct((T, D), F32), jax.ShapeDtypeStruct((T, D), BF16)],
        compiler_params=_cparams(("arbitrary",)),
        name="dense_down",
    )(h, wd, xres, ln_g, ln_b)
    return (x2, x2b, *cast_out)


def _moe_up_kernel(be_ref, nv_ref, x_ref, wg_ref, wu_ref, h_ref):
    i = pl.program_id(1)

    @pl.when(i < nv_ref[0])
    def _():
        h_ref[...] = _swiglu(x_ref[...], wg_ref[...], wu_ref[...])

    @pl.when(i >= nv_ref[0])
    def _():
        h_ref[...] = jnp.zeros(h_ref.shape, h_ref.dtype)


def _moe_down_kernel(be_ref, nv_ref, *refs):
    h_ref, wd_ref, o_ref = refs[0], refs[1], refs[-1]
    i = pl.program_id(0)

    @pl.when(i < nv_ref[0])
    def _():
        o_ref[...] = jnp.dot(h_ref[...], wd_ref[...], preferred_element_type=F32).astype(o_ref.dtype)

    @pl.when(i >= nv_ref[0])
    def _():
        o_ref[...] = jnp.zeros(o_ref.shape, o_ref.dtype)


def _ffn_moe(x_rows_b, wg, wu, wd, block_e, n_valid, y_buf, blk0, total_rows):
    Rc, D = x_rows_b.shape
    F = wg.shape[2]
    tm = min(TM_FFN, Rc)
    nb = Rc // tm
    fh = F // MOE_F_SPLIT
    resident = dict(pipeline_mode=pl.Buffered(1))
    h = pl.pallas_call(
        _moe_up_kernel,
        grid_spec=pltpu.PrefetchScalarGridSpec(
            num_scalar_prefetch=2,
            grid=(MOE_F_SPLIT, nb),
            in_specs=[pl.BlockSpec((tm, D), lambda f, i, be, nv: (i, 0)),
                      pl.BlockSpec((None, D, fh), lambda f, i, be, nv: (be[i], 0, f), **resident),
                      pl.BlockSpec((None, D, fh), lambda f, i, be, nv: (be[i], 0, f), **resident)],
            out_specs=pl.BlockSpec((tm, fh), lambda f, i, be, nv: (i, f))),
        out_shape=jax.ShapeDtypeStruct((Rc, F), BF16),
        compiler_params=_cparams(("arbitrary", "arbitrary")),
        name="moe_up",
    )(block_e, n_valid, x_rows_b, wg, wu)
    in_specs = [pl.BlockSpec((tm, F), lambda i, be, nv: (i, 0)),
                pl.BlockSpec((None, F, D), lambda i, be, nv: (be[i], 0, 0), **resident)]
    args = [block_e, n_valid, h, wd]
    aliases = {}
    if y_buf is not None:
        in_specs.append(pl.BlockSpec(memory_space=pl.ANY))
        args.append(y_buf)
        aliases = {len(args) - 1: 0}
    return pl.pallas_call(
        _moe_down_kernel,
        grid_spec=pltpu.PrefetchScalarGridSpec(
            num_scalar_prefetch=2,
            grid=(nb,),
            in_specs=in_specs,
            out_specs=pl.BlockSpec((tm, D), lambda i, be, nv: (blk0 + i, 0))),
        out_shape=jax.ShapeDtypeStruct((total_rows, D), BF16),
        input_output_aliases=aliases,
        compiler_params=_cparams(("arbitrary",)),
        name="moe_down",
    )(*args)


def _combine_kernel(x_ref, y0_ref, y1_ref, g0_ref, g1_ref, g_ref, b_ref, *refs):
    o_ref, ob_ref = refs[-2:]
    ffn = y0_ref[...].astype(F32) * g0_ref[...] + y1_ref[...].astype(F32) * g1_ref[...]
    x2 = _layer_norm_rows(ALPHA * x_ref[...] + ffn, g_ref[...], b_ref[...])
    o_ref[...] = x2
    ob_ref[...] = x2.astype(BF16)


def _combine(x, y0, y1, g0, g1, ln_g, ln_b, part, bufs):
    T, D = x.shape
    Tp = y0.shape[0]
    tm = min(TM_LN, Tp)
    nb = Tp // tm
    rows_x = pl.BlockSpec((tm, D), lambda i: (part * nb + i, 0))
    rows = pl.BlockSpec((tm, D), lambda i: (i, 0))
    gate = pl.BlockSpec((tm, 1), lambda i: (i, 0))
    vec = pl.BlockSpec((1, D), lambda i: (0, 0))
    in_specs = [rows_x, rows, rows, gate, gate, vec, vec]
    args = [x, y0, y1, g0, g1, ln_g, ln_b]
    aliases = {}
    if bufs is not None:
        in_specs += [pl.BlockSpec(memory_space=pl.ANY)] * 2
        args += list(bufs)
        aliases = {len(args) - 2: 0, len(args) - 1: 1}
    return pl.pallas_call(
        _combine_kernel,
        grid=(nb,),
        in_specs=in_specs,
        out_specs=[rows_x, rows_x],
        out_shape=[jax.ShapeDtypeStruct((T, D), F32), jax.ShapeDtypeStruct((T, D), BF16)],
        input_output_aliases=aliases,
        compiler_params=_cparams(("parallel",)),
        name="moe_combine",
    )(*args)


def _pad_w_in(w_in):
    L, D, _ = w_in.shape
    q_dim = MLA_HEADS * (QK_NOPE + QK_ROPE)
    wq = w_in[:, :, :q_dim].reshape(L, D, MLA_HEADS, QK_NOPE + QK_ROPE)
    kr0 = q_dim + KV_RANK
    w_kr = w_in[:, :, kr0:kr0 + QK_ROPE]
    pad = jnp.zeros((L, D, MLA_HEADS, HEAD_W - QK_NOPE - QK_ROPE), w_in.dtype).at[:, :, 0, :].set(w_kr)
    q_part = jnp.concatenate([wq, pad], axis=-1).reshape(L, D, Q_COLS)
    rest = jnp.concatenate([w_in[:, :, q_dim:kr0], w_in[:, :, kr0 + QK_ROPE:]], axis=-1)
    return jnp.concatenate([q_part, rest], axis=-1).astype(BF16)


def _rope_tables(positions):
    inv_freq = ROPE_THETA ** (-jnp.arange(0, QK_ROPE, 2, dtype=F32) / QK_ROPE)
    ang = positions.astype(F32).reshape(-1)[:, None] * inv_freq
    cos, sin = jnp.cos(ang), jnp.sin(ang)
    cos4 = jnp.concatenate([cos, cos, cos, cos], axis=-1)
    sin4 = jnp.concatenate([-sin, sin, -sin, sin], axis=-1)
    return cos4, sin4


def _route(logits, tm):
    T = logits.shape[0]
    A = T * TOP_K
    top_logit, top_e = lax.top_k(logits, TOP_K)
    gates = jax.nn.softmax(top_logit, axis=-1)
    flat_e = top_e.reshape(A)
    onehot = (flat_e[:, None] == jnp.arange(N_EXPERTS, dtype=flat_e.dtype)[None, :]).astype(jnp.int32)
    csum = jnp.cumsum(onehot, axis=0)
    counts = csum[-1]
    rank = jnp.take_along_axis(csum, flat_e[:, None], axis=1)[:, 0] - 1
    padded = (counts + tm - 1) // tm * tm
    pad_ends = jnp.cumsum(padded)
    pad_starts = pad_ends - padded
    dest = (pad_starts[flat_e] + rank).astype(jnp.int32)
    n_blocks = -(-(A + N_EXPERTS * (tm - 1)) // tm)
    n_pad = n_blocks * tm
    block_start = jnp.arange(n_blocks, dtype=jnp.int32) * tm
    block_e = jnp.minimum(jnp.searchsorted(pad_ends, block_start, side='right'), N_EXPERTS - 1).astype(jnp.int32)
    arange_a = jnp.arange(A, dtype=jnp.int32)
    _, order = lax.sort_key_val(flat_e.astype(jnp.int32) * A + arange_a, arange_a)
    starts = jnp.cumsum(counts) - counts
    e_s = jnp.repeat(block_e, tm)
    r = jnp.arange(n_pad, dtype=jnp.int32) - pad_starts[e_s].astype(jnp.int32)
    src = jnp.clip(starts[e_s].astype(jnp.int32) + r, 0, A - 1)
    tok_sorted = order // TOP_K
    slot_tok = jnp.where(r < counts[e_s], tok_sorted.at[src].get(mode="promise_in_bounds"), 0).astype(jnp.int32)
    n_valid = (pad_ends[-1] // tm).astype(jnp.int32).reshape(1)
    return dest.reshape(T, TOP_K), gates, slot_tok, block_e, n_valid


def kernel(x, positions, w_in, kv_norm_g, w_ukv, conv_w, pool_w, sgu_ln_g, sgu_ln_b, sgu_w, sgu_b, mix_gain, w_o, ln1_g, ln1_b, ffn_wg, ffn_wu, ffn_wd, router_w, exp_wg, exp_wu, exp_wd, ln2_g, ln2_b):
    B, S, D = x.shape
    T = B * S
    L = w_in.shape[0]
    cos4, sin4 = _rope_tables(positions)
    w_in_b = _pad_w_in(w_in)
    w_ukv_b = w_ukv.astype(BF16)
    w_o_b = w_o.astype(BF16)
    pool_w_b = pool_w.astype(BF16)
    sgu_bt = jnp.swapaxes(sgu_b, 1, 2)
    router_pad = jnp.pad(router_w, ((0, 0), (0, 0), (0, LANES - N_EXPERTS)))
    router_hi = router_pad.astype(BF16)
    router_hl = jnp.concatenate([router_hi, (router_pad - router_hi.astype(F32)).astype(BF16)], axis=2)
    ffn_w = (ffn_wg.astype(BF16), ffn_wu.astype(BF16), ffn_wd.astype(BF16))
    n_moe = exp_wg.shape[0]
    exp_f32 = tuple(w.reshape(-1, w.shape[-1]) for w in (exp_wg, exp_wu, exp_wd))
    exp_w = None

    xf = x.reshape(T, D)
    xb = xf
    tm_moe = min(TM_FFN, T)
    for l in range(L):
        proj = _in_proj(xb, w_in_b, l)
        q_t, k_cat, v_t = _mla_prep(proj, cos4, sin4, kv_norm_g[l][None, :], w_ukv_b, l)
        o_mla = _attention(q_t, k_cat, v_t, B, S)
        yc, yp, ys = _mixers(proj, conv_w[l], pool_w_b[l], sgu_ln_g[l][None, :], sgu_ln_b[l][None, :],
                             sgu_w[l], sgu_bt[l], mix_gain[l][None, :], S)
        ln1 = (ln1_g[l][None, :], ln1_b[l][None, :])
        ln2 = (ln2_g[l][None, :], ln2_b[l][None, :])
        e = l // 2
        if l % 2 == 0:
            x1, x1b = _out_proj(o_mla, yc, yp, ys, mix_gain[l][None, :], w_o_b, l, xf, *ln1)
            cast = [(w, e, n_moe) for w in exp_f32] if e < n_moe else []
            xf, xb, *cast_out = _ffn_dense(x1b, *ffn_w, e, x1, *ln2, cast=cast)
            if cast_out:
                exp_w = tuple(c.reshape((N_EXPERTS, -1, c.shape[-1])) for c in cast_out)
        else:
            x1, x1b, logits = _out_proj(o_mla, yc, yp, ys, mix_gain[l][None, :], w_o_b, l, xf, *ln1,
                                        router_hl=router_hl[e])
            dest, gates, slot_tok, block_e, n_valid = _route(logits[:, :N_EXPERTS], tm_moe)
            n_blocks = block_e.shape[0]
            chunks = max(c for c in (MOE_CHUNKS, 2, 1) if n_blocks % c == 0)
            cb = n_blocks // chunks
            y_rows = None
            for c in range(chunks):
                x_rows = x1b.at[slot_tok[c * cb * tm_moe:(c + 1) * cb * tm_moe]].get(mode="promise_in_bounds")
                y_rows = _ffn_moe(x_rows, *exp_w, block_e[c * cb:(c + 1) * cb],
                                  jnp.clip(n_valid - c * cb, 0, cb), y_rows, c * cb, n_blocks * tm_moe)
            parts = COMBINE_PARTS if T % (COMBINE_PARTS * TM_LN) == 0 else 1
            tp = T // parts
            bufs = None
            for p in range(parts):
                ts = slice(p * tp, (p + 1) * tp)
                y0 = y_rows.at[dest[ts, 0]].get(mode="promise_in_bounds")
                y1 = y_rows.at[dest[ts, 1]].get(mode="promise_in_bounds")
                bufs = _combine(x1, y0, y1, gates[ts, 0:1], gates[ts, 1:2], *ln2, p, bufs)
            xf, xb = bufs
    return xf.reshape(B, S, D)
```

```python
import functools
import math

import jax
import jax.numpy as jnp
from jax import lax
from jax.experimental import pallas as pl
from jax.experimental.pallas import tpu as pltpu

F32 = jnp.float32
BF16 = jnp.bfloat16

CHUNK = 64
GROUP_W = 512
MLA_HEADS = 4
QK_NOPE = 128
QK_ROPE = 64
V_HEAD = 128
KV_RANK = 512
ROPE_THETA = 10000.0
CONV_W = 3
POOL_WINDOWS = (2, 4, 8, 16)
POOL_CH = 128
SGU_LEN = 128
SGU_GROUPS = 4
SGU_CH = 128
N_EXPERTS = 8
TOP_K = 2
DEPTH = 4
ALPHA = (2.0 * DEPTH) ** 0.25
LN_EPS = 1e-5
RMS_EPS = 1e-6

LANES = 128
HEAD_W = 2 * LANES
VMEM_LIMIT = 56 * 1024 * 1024

Q_COLS = MLA_HEADS * HEAD_W
COL_CKV, COL_CB, COL_CC, COL_CH, COL_PH, COL_GU, COL_GV = 2, 3, 4, 5, 6, 7, 8

HALO = 32

TM_PROJ, TN_PROJ = 1024, 2304
TB_ATT = 1024
ATT_HEADS = 2
ATT_BLOCKS_PER_TRIP = 2
TM_PREP = TB_ATT
MASK_BIG = 2.0 ** 100
TM_MIX = 512
TM_OUT, OUT_PARTS = 512, 2
TM_LN = 256
TM_FFN = 512
TM_DENSE = 256
MOE_F_SPLIT = 2
MOE_CHUNKS = 4
COMBINE_PARTS = 2


def _cparams(sem):
    return pltpu.CompilerParams(dimension_semantics=sem, vmem_limit_bytes=VMEM_LIMIT)


def _layer_norm_rows(z, g, b):
    mu = jnp.mean(z, axis=-1, keepdims=True)
    zc = z - mu
    var = jnp.mean(zc * zc, axis=-1, keepdims=True)
    return zc * lax.rsqrt(var + LN_EPS) * g + b


def _rms_rows(y):
    return y * lax.rsqrt(jnp.mean(y * y, axis=-1, keepdims=True) + RMS_EPS)


def _matmul_kernel(x_ref, w_ref, o_ref):
    o_ref[...] = jnp.dot(x_ref[...].astype(BF16), w_ref[...], preferred_element_type=F32).astype(o_ref.dtype)


def _in_proj(xb, w_all, l):
    T, K = xb.shape
    N = w_all.shape[2]
    tm = min(TM_PROJ, T)
    return pl.pallas_call(
        _matmul_kernel,
        grid=(T // tm, N // TN_PROJ),
        in_specs=[pl.BlockSpec((tm, K), lambda i, j: (i, 0)),
                  pl.BlockSpec((None, K, TN_PROJ), lambda i, j: (l, 0, j))],
        out_specs=pl.BlockSpec((tm, TN_PROJ), lambda i, j: (i, j)),
        out_shape=jax.ShapeDtypeStruct((T, N), BF16),
        compiler_params=_cparams(("parallel", "arbitrary")),
        name="in_proj",
    )(xb, w_all)


def _mla_prep_kernel(q_ref, ckv_ref, cos_ref, sin_ref, g_ref, wukv_ref, qt_ref, ko_ref, vt_ref, *, qscale):
    cos = cos_ref[...]
    sin = sin_ref[...]
    lane = lax.broadcasted_iota(jnp.int32, cos.shape, 1)
    row_chunk = lax.broadcasted_iota(jnp.int32, cos.shape, 0) // CHUNK
    first_half = (lane & (QK_ROPE // 2)) == 0
    low = lane < QK_ROPE

    def rope(r):
        partner = jnp.where(first_half, pltpu.roll(r, LANES - QK_ROPE // 2, axis=1), pltpu.roll(r, QK_ROPE // 2, axis=1))
        return r * cos + partner * sin

    k_rope = None
    for h in range(MLA_HEADS):
        c0 = h * HEAD_W
        qt_ref[h, 0:LANES, :] = (q_ref[:, c0:c0 + LANES].astype(F32) * qscale).T.astype(BF16)
        rr = rope(q_ref[:, c0 + LANES:c0 + HEAD_W].astype(F32))
        qt_ref[h, LANES:HEAD_W, :] = jnp.where(low, rr * qscale, 0.0).T.astype(BF16)
        if h == 0:
            stair = jnp.where(row_chunk > lane - QK_ROPE, -MASK_BIG, 0.0)
            k_rope = jnp.where(low, pltpu.roll(rr, QK_ROPE, axis=1), stair).astype(BF16)

    c = ckv_ref[...].astype(F32)
    cn = (_rms_rows(c) * g_ref[...]).astype(BF16)
    kv = jnp.dot(cn, wukv_ref[...], preferred_element_type=F32)
    for h in range(MLA_HEADS):
        c0 = h * HEAD_W
        ko_ref[:, c0:c0 + LANES] = kv[:, c0:c0 + LANES].astype(BF16)
        ko_ref[:, c0 + LANES:c0 + HEAD_W] = k_rope
        vt_ref[h] = kv[:, c0 + LANES:c0 + HEAD_W].T.astype(BF16)


def _mla_prep(proj, cos4, sin4, kv_g, w_ukv_all, l):
    T = proj.shape[0]
    tm = min(TM_PREP, T)
    qscale = math.log2(math.e) / math.sqrt(QK_NOPE + QK_ROPE)
    return pl.pallas_call(
        functools.partial(_mla_prep_kernel, qscale=qscale),
        grid=(T // tm,),
        in_specs=[pl.BlockSpec((tm, Q_COLS), lambda i: (i, 0)),
                  pl.BlockSpec((tm, GROUP_W), lambda i: (i, COL_CKV)),
                  pl.BlockSpec((tm, LANES), lambda i: (i, 0)),
                  pl.BlockSpec((tm, LANES), lambda i: (i, 0)),
                  pl.BlockSpec((1, KV_RANK), lambda i: (0, 0)),
                  pl.BlockSpec((None,) + w_ukv_all.shape[1:], lambda i: (l, 0, 0))],
        out_specs=[pl.BlockSpec((None, MLA_HEADS, HEAD_W, tm), lambda i: (i, 0, 0, 0)),
                   pl.BlockSpec((tm, Q_COLS), lambda i: (i, 0)),
                   pl.BlockSpec((None, MLA_HEADS, V_HEAD, tm), lambda i: (i, 0, 0, 0))],
        out_shape=[jax.ShapeDtypeStruct((T // tm, MLA_HEADS, HEAD_W, tm), BF16),
                   jax.ShapeDtypeStruct((T, Q_COLS), BF16),
                   jax.ShapeDtypeStruct((T // tm, MLA_HEADS, V_HEAD, tm), BF16)],
        compiler_params=_cparams(("parallel",)),
        name="mla_prep",
    )(proj, proj, cos4, sin4, kv_g, w_ukv_all)


def _attention_kernel(q_ref, k_ref, vt_ref, o_ref, q2_ref, s0_ref, s1_ref, m_ref, l_ref, acc_ref, *, tb):
    i = pl.program_id(2)
    hp = ATT_HEADS
    q = q_ref[...]
    dim = lax.broadcasted_iota(jnp.int32, q.shape, 1)
    query_chunk = lax.broadcasted_iota(jnp.int32, q.shape, 2) // CHUNK
    q2_ref[0] = q
    q2_ref[1] = jnp.where(dim == QK_NOPE + QK_ROPE + query_chunk, jnp.ones_like(q), q)
    m_ref[...] = jnp.full(m_ref.shape, -jnp.inf, F32)
    l_ref[...] = jnp.zeros(l_ref.shape, F32)
    acc_ref[...] = jnp.zeros(acc_ref.shape, F32)

    def scores(j, s_ref):
        r0 = pl.multiple_of(j * tb, tb)
        sel = (j == i).astype(jnp.int32)
        for h in range(hp):
            cs = slice(h * HEAD_W, (h + 1) * HEAD_W)
            s_ref[h] = jnp.dot(k_ref[pl.ds(r0, tb), cs], q2_ref[sel, h],
                               preferred_element_type=F32)

    def softmax_pv(j, s_ref):
        for h in range(hp):
            s = s_ref[h]
            m_old = m_ref[h]
            m_new = jnp.maximum(m_old, jnp.max(s, axis=0, keepdims=True))
            p = jnp.exp2(s - m_new)
            a = jnp.exp2(m_old - m_new)
            l_ref[h] = a * l_ref[h] + jnp.sum(p, axis=0, keepdims=True)
            acc_ref[h] = a * acc_ref[h] + jnp.dot(vt_ref[j, h], p.astype(BF16), preferred_element_type=F32)
            m_ref[h] = m_new

    scores(0, s0_ref)
    bufs = (s0_ref, s1_ref)
    U = ATT_BLOCKS_PER_TRIP

    def trip(g, carry):
        j = U * g
        for u in range(U):
            scores(j + u + 1, bufs[(u + 1) % 2])
            softmax_pv(j + u, bufs[u % 2])
        return carry

    trips = i // U
    lax.fori_loop(0, trips, trip, 0)
    base = U * trips
    for u in range(U):
        j = base + u

        @pl.when(j < i)
        def _():
            scores(j + 1, bufs[(u + 1) % 2])
            softmax_pv(j, bufs[u % 2])

        @pl.when(j == i)
        def _():
            softmax_pv(j, bufs[u % 2])

    for h in range(hp):
        o_ref[:, h * V_HEAD:(h + 1) * V_HEAD] = (acc_ref[h] / l_ref[h]).T


def _attention(q_t, k_cat, v_t, B, S):
    T = B * S
    tb = v_t.shape[-1]
    nq = S // tb
    hp = ATT_HEADS
    resident = dict(pipeline_mode=pl.Buffered(1))
    return pl.pallas_call(
        functools.partial(_attention_kernel, tb=tb),
        grid=(B, MLA_HEADS // hp, nq),
        in_specs=[pl.BlockSpec((None, hp, HEAD_W, tb), lambda b, h, i: (b * nq + i, h, 0, 0)),
                  pl.BlockSpec((S, hp * HEAD_W), lambda b, h, i: (b, h), **resident),
                  pl.BlockSpec((nq, hp, V_HEAD, tb), lambda b, h, i: (b, h, 0, 0), **resident)],
        out_specs=pl.BlockSpec((tb, hp * V_HEAD), lambda b, h, i: (b * nq + i, h)),
        out_shape=jax.ShapeDtypeStruct((T, MLA_HEADS * V_HEAD), F32),
        scratch_shapes=[pltpu.VMEM((2, hp, HEAD_W, tb), BF16),
                        pltpu.VMEM((hp, tb, tb), F32), pltpu.VMEM((hp, tb, tb), F32),
                        pltpu.VMEM((hp, 1, tb), F32), pltpu.VMEM((hp, 1, tb), F32),
                        pltpu.VMEM((hp, V_HEAD, tb), F32)],
        compiler_params=_cparams(("parallel", "parallel", "arbitrary")),
        name="attention",
    )(q_t, k_cat, v_t)


def _gelu(x):
    return 0.5 * x * (1.0 + lax.erf(x * (1.0 / math.sqrt(2.0))))


def _mixers_kernel(cb_ref, cc_ref, ch_ref, ph_ref, gu_ref, gv_ref, hcc_ref, hch_ref, hph_ref,
                   convw_ref, poolw_ref, lng_ref, lnb_ref, sguw_ref, sgub_ref, gain_ref,
                   yc_ref, yp_ref, ys_ref,
                   g_scr, e_scr, a_scr, b_scr, y_scr, *, tm, tiles_per_seq):
    i = pl.program_id(0)
    t0 = (i % tiles_per_seq) * tm
    keep = jnp.where(t0 == 0, 0.0, 1.0).astype(F32)

    g_scr[0:HALO, :] = hcc_ref[...].astype(F32) * hch_ref[...].astype(F32) * keep
    g_scr[HALO:, :] = cc_ref[...].astype(F32) * ch_ref[...].astype(F32)
    conv = convw_ref[CONV_W - 1:CONV_W, :] * g_scr[HALO:HALO + tm, :]
    for j in range(CONV_W - 1):
        off = HALO - (CONV_W - 1) + j
        conv = conv + convw_ref[j:j + 1, :] * g_scr[off:off + tm, :]
    yc = cb_ref[...].astype(F32) * conv
    yc_ref[...] = (_rms_rows(yc) * gain_ref[:, GROUP_W:2 * GROUP_W]).astype(BF16)

    e_scr[0:HALO, :] = hph_ref[...].astype(F32) * keep
    e_scr[HALO:, :] = ph_ref[...].astype(F32)
    n = tm + HALO
    pos = (t0 + 1 + lax.broadcasted_iota(jnp.int32, (tm, 1), 0)).astype(F32)
    for g, w in enumerate(POOL_WINDOWS):
        cs = slice(g * POOL_CH, (g + 1) * POOL_CH)
        levels = g + 1
        src = None
        bufs = (a_scr, b_scr)
        total = None
        for lev in range(1, levels + 1):
            shift = 1 << (lev - 1)
            lo = 8 * lev if lev < levels else HALO
            if src is None:
                cur = e_scr[lo:n, cs] + e_scr[lo - shift:n - shift, cs]
            else:
                cur = src[lo:n, :] + src[lo - shift:n - shift, :]
            if lev < levels:
                dst = bufs[(lev - 1) % 2]
                dst[lo:n, :] = cur
                src = dst
            else:
                total = cur
        inv = 1.0 / jnp.minimum(pos, float(w))
        pooled = total * inv - e_scr[HALO:, cs]
        y_scr[:, cs] = jnp.dot(pooled.astype(BF16), poolw_ref[g], preferred_element_type=F32)
    yp = y_scr[...]
    yp_ref[...] = (_rms_rows(yp) * gain_ref[:, 2 * GROUP_W:3 * GROUP_W]).astype(BF16)

    u = _gelu(gu_ref[...].astype(F32))
    vn = _layer_norm_rows(_gelu(gv_ref[...].astype(F32)), lng_ref[...], lnb_ref[...]).astype(BF16)
    row = lax.broadcasted_iota(jnp.int32, (SGU_LEN, SGU_LEN), 0)
    col = lax.broadcasted_iota(jnp.int32, (SGU_LEN, SGU_LEN), 1)
    for g in range(SGU_GROUPS):
        cs = slice(g * SGU_CH, (g + 1) * SGU_CH)
        wg = jnp.where(col <= row, sguw_ref[g], 0.0).astype(BF16)
        bias = sgub_ref[:, g:g + 1]
        for c in range(tm // SGU_LEN):
            rs = slice(c * SGU_LEN, (c + 1) * SGU_LEN)
            mixed = jnp.dot(wg, vn[rs, cs], preferred_element_type=F32) + bias
            y_scr[rs, cs] = u[rs, cs] * mixed
    ysg = y_scr[...]
    ys_ref[...] = (_rms_rows(ysg) * gain_ref[:, 3 * GROUP_W:4 * GROUP_W]).astype(BF16)


def _mixers(proj, conv_w, pool_w, ln_g, ln_b, sgu_w, sgu_bt, gain, S):
    T = proj.shape[0]
    tm = min(TM_MIX, S)
    tiles_per_seq = S // tm
    hb = tm // HALO

    def col(c):
        return pl.BlockSpec((tm, GROUP_W), lambda i: (i, c))

    def halo(c):
        return pl.BlockSpec((HALO, GROUP_W), lambda i: (jnp.maximum(i * hb - 1, 0), c))

    def full(a):
        return pl.BlockSpec(a.shape, lambda i: (0,) * a.ndim)

    out = jax.ShapeDtypeStruct((T, GROUP_W), BF16)
    return pl.pallas_call(
        functools.partial(_mixers_kernel, tm=tm, tiles_per_seq=tiles_per_seq),
        grid=(T // tm,),
        in_specs=[col(COL_CB), col(COL_CC), col(COL_CH), col(COL_PH), col(COL_GU), col(COL_GV),
                  halo(COL_CC), halo(COL_CH), halo(COL_PH),
                  full(conv_w), full(pool_w), full(ln_g), full(ln_b), full(sgu_w), full(sgu_bt), full(gain)],
        out_specs=[pl.BlockSpec((tm, GROUP_W), lambda i: (i, 0))] * 3,
        out_shape=[out, out, out],
        scratch_shapes=[pltpu.VMEM((tm + HALO, GROUP_W), F32), pltpu.VMEM((tm + HALO, GROUP_W), F32),
                        pltpu.VMEM((tm + HALO, POOL_CH), F32), pltpu.VMEM((tm + HALO, POOL_CH), F32),
                        pltpu.VMEM((tm, GROUP_W), F32)],
        compiler_params=_cparams(("parallel",)),
        name="mixers",
    )(proj, proj, proj, proj, proj, proj, proj, proj, proj,
      conv_w, pool_w, ln_g, ln_b, sgu_w, sgu_bt, gain)


def _out_proj_kernel(*refs, with_router):
    if with_router:
        (o_ref, yc_ref, yp_ref, ys_ref, gain_ref, wo_ref, x_ref, g_ref, b_ref, rw_ref,
         x1_ref, x1b_ref, lg_ref) = refs
    else:
        (o_ref, yc_ref, yp_ref, ys_ref, gain_ref, wo_ref, x_ref, g_ref, b_ref,
         x1_ref, x1b_ref) = refs
    hr = o_ref.shape[0] // OUT_PARTS
    for part in range(OUT_PARTS):
        rs = slice(part * hr, (part + 1) * hr)
        ym = (_rms_rows(o_ref[rs, :]) * gain_ref[:, 0:GROUP_W]).astype(BF16)
        mix = jnp.dot(ym, wo_ref[0:GROUP_W, :], preferred_element_type=F32)
        for gi, y_ref in enumerate((yc_ref, yp_ref, ys_ref), start=1):
            mix = mix + jnp.dot(y_ref[rs, :], wo_ref[gi * GROUP_W:(gi + 1) * GROUP_W, :], preferred_element_type=F32)
        x1 = _layer_norm_rows(ALPHA * x_ref[rs, :] + mix, g_ref[...], b_ref[...])
        x1_ref[rs, :] = x1
        x1_hi = x1.astype(BF16)
        x1b_ref[rs, :] = x1_hi
        if with_router:
            x1_lo = (x1 - x1_hi.astype(F32)).astype(BF16)
            hh_hl = jnp.dot(x1_hi, rw_ref[...], preferred_element_type=F32)
            lh = jnp.dot(x1_lo, rw_ref[:, 0:LANES], preferred_element_type=F32)
            lg_ref[rs, :] = hh_hl[:, 0:LANES] + (hh_hl[:, LANES:] + lh)


def _out_proj(o_mla, yc, yp, ys, gain, wo_all, l, x, ln_g, ln_b, router_hl=None):
    T, D = x.shape
    tm = min(TM_OUT, T)
    with_router = router_hl is not None

    def rows(w):
        return pl.BlockSpec((tm, w), lambda i: (i, 0))

    def full(a):
        return pl.BlockSpec(a.shape, lambda i: (0,) * a.ndim)

    wo_spec = pl.BlockSpec((None,) + wo_all.shape[1:], lambda i: (l, 0, 0))
    in_specs = [rows(GROUP_W)] * 4 + [full(gain), wo_spec, rows(D), full(ln_g), full(ln_b)]
    args = [o_mla, yc, yp, ys, gain, wo_all, x, ln_g, ln_b]
    out_specs = [rows(D), rows(D)]
    out_shape = [jax.ShapeDtypeStruct((T, D), F32), jax.ShapeDtypeStruct((T, D), BF16)]
    if with_router:
        in_specs.append(full(router_hl))
        args.append(router_hl)
        out_specs.append(rows(LANES))
        out_shape.append(jax.ShapeDtypeStruct((T, LANES), F32))
    return pl.pallas_call(
        functools.partial(_out_proj_kernel, with_router=with_router),
        grid=(T // tm,),
        in_specs=in_specs,
        out_specs=out_specs,
        out_shape=out_shape,
        compiler_params=_cparams(("parallel",)),
        name="out_proj",
    )(*args)


def _swiglu(xb, wg, wu):
    hg = jnp.dot(xb, wg, preferred_element_type=F32)
    hu = jnp.dot(xb, wu, preferred_element_type=F32)
    return (hg * jax.nn.sigmoid(hg) * hu).astype(BF16)


def _dense_up_kernel(*refs, n_cast):
    x_ref, wg_ref, wu_ref = refs[:3]
    cast_in = refs[3:3 + n_cast]
    h_ref = refs[3 + n_cast]
    cast_out = refs[4 + n_cast:4 + 2 * n_cast]
    for src_ref, dst_ref in zip(cast_in, cast_out):
        dst_ref[...] = src_ref[...].astype(dst_ref.dtype)
    h_ref[...] = _swiglu(x_ref[...], wg_ref[...], wu_ref[...])


def _dense_down_kernel(h_ref, wd_ref, xres_ref, g_ref, b_ref, o_ref, ob_ref):
    y = jnp.dot(h_ref[...], wd_ref[...], preferred_element_type=F32)
    x2 = _layer_norm_rows(ALPHA * xres_ref[...] + y, g_ref[...], b_ref[...])
    o_ref[...] = x2
    ob_ref[...] = x2.astype(BF16)


def _ffn_dense(xb, wg, wu, wd, e, xres, ln_g, ln_b, cast=()):
    T, D = xb.shape
    F = wg.shape[2]
    tm = min(TM_DENSE, T)
    nb = T // tm
    fh = F // MOE_F_SPLIT
    resident = dict(pipeline_mode=pl.Buffered(1))
    in_specs = [pl.BlockSpec((tm, D), lambda f, i: (i, 0)),
                pl.BlockSpec((None, D, fh), lambda f, i: (e, 0, f), **resident),
                pl.BlockSpec((None, D, fh), lambda f, i: (e, 0, f), **resident)]
    out_specs = [pl.BlockSpec((tm, fh), lambda f, i: (i, f))]
    out_shape = [jax.ShapeDtypeStruct((T, F), BF16)]
    steps = MOE_F_SPLIT * nb
    for arr, part, n_parts in cast:
        rows, cols = arr.shape[0] // n_parts, arr.shape[1]
        rb = rows // steps
        assert rb * steps == rows and rb % 16 == 0, (rows, steps)
        in_specs.append(pl.BlockSpec((rb, cols), lambda f, i, off=part * steps: (off + f * nb + i, 0)))
        out_specs.append(pl.BlockSpec((rb, cols), lambda f, i: (f * nb + i, 0)))
        out_shape.append(jax.ShapeDtypeStruct((rows, cols), BF16))
    h, *cast_out = pl.pallas_call(
        functools.partial(_dense_up_kernel, n_cast=len(cast)),
        grid=(MOE_F_SPLIT, nb),
        in_specs=in_specs,
        out_specs=out_specs,
        out_shape=out_shape,
        compiler_params=_cparams(("arbitrary", "arbitrary")),
        name="dense_up",
    )(xb, wg, wu, *[c[0] for c in cast])
    rows_spec = pl.BlockSpec((tm, D), lambda i: (i, 0))
    vec = pl.BlockSpec((1, D), lambda i: (0, 0))
    x2, x2b = pl.pallas_call(
        _dense_down_kernel,
        grid=(nb,),
        in_specs=[pl.BlockSpec((tm, F), lambda i: (i, 0)),
                  pl.BlockSpec((None, F, D), lambda i: (e, 0, 0), **resident), rows_spec, vec, vec],
        out_specs=[rows_spec, rows_spec],
        out_shape=[jax.ShapeDtypeStruct((T, D), F32), jax.ShapeDtypeStruct((T, D), BF16)],
        compiler_params=_cparams(("arbitrary",)),
        name="dense_down",
    )(h, wd, xres, ln_g, ln_b)
    return (x2, x2b, *cast_out)


def _moe_up_kernel(be_ref, nv_ref, x_ref, wg_ref, wu_ref, h_ref):
    i = pl.program_id(1)

    @pl.when(i < nv_ref[0])
    def _():
        h_ref[...] = _swiglu(x_ref[...], wg_ref[...], wu_ref[...])

    @pl.when(i >= nv_ref[0])
    def _():
        h_ref[...] = jnp.zeros(h_ref.shape, h_ref.dtype)


def _moe_down_kernel(be_ref, nv_ref, *refs):
    h_ref, wd_ref, o_ref = refs[0], refs[1], refs[-1]
    i = pl.program_id(0)

    @pl.when(i < nv_ref[0])
    def _():
        o_ref[...] = jnp.dot(h_ref[...], wd_ref[...], preferred_element_type=F32).astype(o_ref.dtype)

    @pl.when(i >= nv_ref[0])
    def _():
        o_ref[...] = jnp.zeros(o_ref.shape, o_ref.dtype)


def _ffn_moe(x_rows_b, wg, wu, wd, block_e, n_valid, y_buf, blk0, total_rows):
    Rc, D = x_rows_b.shape
    F = wg.shape[2]
    tm = min(TM_FFN, Rc)
    nb = Rc // tm
    fh = F // MOE_F_SPLIT
    resident = dict(pipeline_mode=pl.Buffered(1))
    h = pl.pallas_call(
        _moe_up_kernel,
        grid_spec=pltpu.PrefetchScalarGridSpec(
            num_scalar_prefetch=2,
            grid=(MOE_F_SPLIT, nb),
            in_specs=[pl.BlockSpec((tm, D), lambda f, i, be, nv: (i, 0)),
                      pl.BlockSpec((None, D, fh), lambda f, i, be, nv: (be[i], 0, f), **resident),
                      pl.BlockSpec((None, D, fh), lambda f, i, be, nv: (be[i], 0, f), **resident)],
            out_specs=pl.BlockSpec((tm, fh), lambda f, i, be, nv: (i, f))),
        out_shape=jax.ShapeDtypeStruct((Rc, F), BF16),
        compiler_params=_cparams(("arbitrary", "arbitrary")),
        name="moe_up",
    )(block_e, n_valid, x_rows_b, wg, wu)
    in_specs = [pl.BlockSpec((tm, F), lambda i, be, nv: (i, 0)),
                pl.BlockSpec((None, F, D), lambda i, be, nv: (be[i], 0, 0), **resident)]
    args = [block_e, n_valid, h, wd]
    aliases = {}
    if y_buf is not None:
        in_specs.append(pl.BlockSpec(memory_space=pl.ANY))
        args.append(y_buf)
        aliases = {len(args) - 1: 0}
    return pl.pallas_call(
        _moe_down_kernel,
        grid_spec=pltpu.PrefetchScalarGridSpec(
            num_scalar_prefetch=2,
            grid=(nb,),
            in_specs=in_specs,
            out_specs=pl.BlockSpec((tm, D), lambda i, be, nv: (blk0 + i, 0))),
        out_shape=jax.ShapeDtypeStruct((total_rows, D), BF16),
        input_output_aliases=aliases,
        compiler_params=_cparams(("arbitrary",)),
        name="moe_down",
    )(*args)


def _combine_kernel(x_ref, y0_ref, y1_ref, g0_ref, g1_ref, g_ref, b_ref, *refs):
    o_ref, ob_ref = refs[-2:]
    ffn = y0_ref[...].astype(F32) * g0_ref[...] + y1_ref[...].astype(F32) * g1_ref[...]
    x2 = _layer_norm_rows(ALPHA * x_ref[...] + ffn, g_ref[...], b_ref[...])
    o_ref[...] = x2
    ob_ref[...] = x2.astype(BF16)


def _combine(x, y0, y1, g0, g1, ln_g, ln_b, part, bufs):
    T, D = x.shape
    Tp = y0.shape[0]
    tm = min(TM_LN, Tp)
    nb = Tp // tm
    rows_x = pl.BlockSpec((tm, D), lambda i: (part * nb + i, 0))
    rows = pl.BlockSpec((tm, D), lambda i: (i, 0))
    gate = pl.BlockSpec((tm, 1), lambda i: (i, 0))
    vec = pl.BlockSpec((1, D), lambda i: (0, 0))
    in_specs = [rows_x, rows, rows, gate, gate, vec, vec]
    args = [x, y0, y1, g0, g1, ln_g, ln_b]
    aliases = {}
    if bufs is not None:
        in_specs += [pl.BlockSpec(memory_space=pl.ANY)] * 2
        args += list(bufs)
        aliases = {len(args) - 2: 0, len(args) - 1: 1}
    return pl.pallas_call(
        _combine_kernel,
        grid=(nb,),
        in_specs=in_specs,
        out_specs=[rows_x, rows_x],
        out_shape=[jax.ShapeDtypeStruct((T, D), F32), jax.ShapeDtypeStruct((T, D), BF16)],
        input_output_aliases=aliases,
        compiler_params=_cparams(("parallel",)),
        name="moe_combine",
    )(*args)


def _pad_w_in(w_in):
    L, D, _ = w_in.shape
    q_dim = MLA_HEADS * (QK_NOPE + QK_ROPE)
    wq = w_in[:, :, :q_dim].reshape(L, D, MLA_HEADS, QK_NOPE + QK_ROPE)
    kr0 = q_dim + KV_RANK
    w_kr = w_in[:, :, kr0:kr0 + QK_ROPE]
    pad = jnp.zeros((L, D, MLA_HEADS, HEAD_W - QK_NOPE - QK_ROPE), w_in.dtype).at[:, :, 0, :].set(w_kr)
    q_part = jnp.concatenate([wq, pad], axis=-1).reshape(L, D, Q_COLS)
    rest = jnp.concatenate([w_in[:, :, q_dim:kr0], w_in[:, :, kr0 + QK_ROPE:]], axis=-1)
    return jnp.concatenate([q_part, rest], axis=-1).astype(BF16)


def _rope_tables(positions):
    inv_freq = ROPE_THETA ** (-jnp.arange(0, QK_ROPE, 2, dtype=F32) / QK_ROPE)
    ang = positions.astype(F32).reshape(-1)[:, None] * inv_freq
    cos, sin = jnp.cos(ang), jnp.sin(ang)
    cos4 = jnp.concatenate([cos, cos, cos, cos], axis=-1)
    sin4 = jnp.concatenate([-sin, sin, -sin, sin], axis=-1)
    return cos4, sin4


def _route(logits, tm):
    T = logits.shape[0]
    A = T * TOP_K
    top_logit, top_e = lax.top_k(logits, TOP_K)
    gates = jax.nn.softmax(top_logit, axis=-1)
    flat_e = jnp.concatenate([top_e[:, k] for k in range(TOP_K)]).astype(jnp.int32)
    onehot = (flat_e[:, None] == jnp.arange(N_EXPERTS, dtype=jnp.int32)[None, :]).astype(jnp.int32)
    csum = jnp.cumsum(onehot, axis=0)
    counts = csum[-1]
    rank = jnp.sum(csum * onehot, axis=1) - 1
    padded = (counts + tm - 1) // tm * tm
    pad_ends = jnp.cumsum(padded)
    pad_starts = pad_ends - padded
    dest = (pad_starts[flat_e] + rank).astype(jnp.int32)
    n_blocks = -(-(A + N_EXPERTS * (tm - 1)) // tm)
    n_pad = n_blocks * tm
    block_start = jnp.arange(n_blocks, dtype=jnp.int32) * tm
    block_e = jnp.minimum(jnp.searchsorted(pad_ends, block_start, side='right'), N_EXPERTS - 1).astype(jnp.int32)
    arange_a = jnp.arange(A, dtype=jnp.int32)
    _, order = lax.sort_key_val(flat_e * A + arange_a, arange_a)
    starts = jnp.cumsum(counts) - counts
    e_s = jnp.repeat(block_e, tm)
    r = jnp.arange(n_pad, dtype=jnp.int32) - pad_starts[e_s].astype(jnp.int32)
    src = jnp.clip(starts[e_s].astype(jnp.int32) + r, 0, A - 1)
    tok_sorted = order % T
    slot_tok = jnp.where(r < counts[e_s], tok_sorted.at[src].get(mode="promise_in_bounds"), 0).astype(jnp.int32)
    n_valid = (pad_ends[-1] // tm).astype(jnp.int32).reshape(1)
    return dest.reshape(TOP_K, T), gates, slot_tok, block_e, n_valid


def kernel(x, positions, w_in, kv_norm_g, w_ukv, conv_w, pool_w, sgu_ln_g, sgu_ln_b, sgu_w, sgu_b, mix_gain, w_o, ln1_g, ln1_b, ffn_wg, ffn_wu, ffn_wd, router_w, exp_wg, exp_wu, exp_wd, ln2_g, ln2_b):
    B, S, D = x.shape
    T = B * S
    L = w_in.shape[0]
    cos4, sin4 = _rope_tables(positions)
    w_in_b = _pad_w_in(w_in)
    w_ukv_b = w_ukv.astype(BF16)
    w_o_b = w_o.astype(BF16)
    pool_w_b = pool_w.astype(BF16)
    sgu_bt = jnp.swapaxes(sgu_b, 1, 2)
    router_pad = jnp.pad(router_w, ((0, 0), (0, 0), (0, LANES - N_EXPERTS)))
    router_hi = router_pad.astype(BF16)
    router_hl = jnp.concatenate([router_hi, (router_pad - router_hi.astype(F32)).astype(BF16)], axis=2)
    ffn_w = (ffn_wg.astype(BF16), ffn_wu.astype(BF16), ffn_wd.astype(BF16))
    n_moe = exp_wg.shape[0]
    exp_f32 = tuple(w.reshape(-1, w.shape[-1]) for w in (exp_wg, exp_wu, exp_wd))
    exp_w = None

    xf = x.reshape(T, D)
    xb = xf
    tm_moe = min(TM_FFN, T)
    for l in range(L):
        proj = _in_proj(xb, w_in_b, l)
        q_t, k_cat, v_t = _mla_prep(proj, cos4, sin4, kv_norm_g[l][None, :], w_ukv_b, l)
        o_mla = _attention(q_t, k_cat, v_t, B, S)
        yc, yp, ys = _mixers(proj, conv_w[l], pool_w_b[l], sgu_ln_g[l][None, :], sgu_ln_b[l][None, :],
                             sgu_w[l], sgu_bt[l], mix_gain[l][None, :], S)
        ln1 = (ln1_g[l][None, :], ln1_b[l][None, :])
        ln2 = (ln2_g[l][None, :], ln2_b[l][None, :])
        e = l // 2
        if l % 2 == 0:
            x1, x1b = _out_proj(o_mla, yc, yp, ys, mix_gain[l][None, :], w_o_b, l, xf, *ln1)
            cast = [(w, e, n_moe) for w in exp_f32] if e < n_moe else []
            xf, xb, *cast_out = _ffn_dense(x1b, *ffn_w, e, x1, *ln2, cast=cast)
            if cast_out:
                exp_w = tuple(c.reshape((N_EXPERTS, -1, c.shape[-1])) for c in cast_out)
        else:
            x1, x1b, logits = _out_proj(o_mla, yc, yp, ys, mix_gain[l][None, :], w_o_b, l, xf, *ln1,
                                        router_hl=router_hl[e])
            dest, gates, slot_tok, block_e, n_valid = _route(logits[:, :N_EXPERTS], tm_moe)
            n_blocks = block_e.shape[0]
            chunks = max(c for c in (MOE_CHUNKS, 2, 1) if n_blocks % c == 0)
            cb = n_blocks // chunks
            y_rows = None
            for c in range(chunks):
                x_rows = x1b.at[slot_tok[c * cb * tm_moe:(c + 1) * cb * tm_moe]].get(mode="promise_in_bounds")
                y_rows = _ffn_moe(x_rows, *exp_w, block_e[c * cb:(c + 1) * cb],
                                  jnp.clip(n_valid - c * cb, 0, cb), y_rows, c * cb, n_blocks * tm_moe)
            parts = COMBINE_PARTS if T % (COMBINE_PARTS * TM_LN) == 0 else 1
            tp = T // parts
            bufs = None
            for p in range(parts):
                ts = slice(p * tp, (p + 1) * tp)
                y0 = y_rows.at[dest[0, ts]].get(mode="promise_in_bounds")
                y1 = y_rows.at[dest[1, ts]].get(mode="promise_in_bounds")
                bufs = _combine(x1, y0, y1, gates[ts, 0:1], gates[ts, 1:2], *ln2, p, bufs)
            xf, xb = bufs
    return xf.reshape(B, S, D)
```

```python
import functools
import math

import jax
import jax.numpy as jnp
from jax import lax
from jax.experimental import pallas as pl
from jax.experimental.pallas import tpu as pltpu

F32 = jnp.float32
BF16 = jnp.bfloat16

CHUNK = 64
GROUP_W = 512
MLA_HEADS = 4
QK_NOPE = 128
QK_ROPE = 64
V_HEAD = 128
KV_RANK = 512
ROPE_THETA = 10000.0
CONV_W = 3
POOL_WINDOWS = (2, 4, 8, 16)
POOL_CH = 128
SGU_LEN = 128
SGU_GROUPS = 4
SGU_CH = 128
N_EXPERTS = 8
TOP_K = 2
DEPTH = 4
ALPHA = (2.0 * DEPTH) ** 0.25
LN_EPS = 1e-5
RMS_EPS = 1e-6

LANES = 128
HEAD_W = 2 * LANES
V_ROWS = 128 + 16
VMEM_LIMIT = 56 * 1024 * 1024

Q_COLS = MLA_HEADS * HEAD_W
COL_CKV, COL_CB, COL_CC, COL_CH, COL_PH, COL_GU, COL_GV = 2, 3, 4, 5, 6, 7, 8

HALO = 32

TM_PROJ, TN_PROJ = 1024, 2304
TB_ATT = 1024
ATT_HEADS = 2
ATT_BLOCKS_PER_TRIP = 2
TM_PREP = TB_ATT
MASK_BIG = 2.0 ** 100
TM_MIX = 512
TM_OUT, OUT_PARTS = 512, 2
TM_LN = 256
TM_FFN = 512
TM_DENSE = 256
MOE_F_SPLIT = 2
MOE_CHUNKS = 4
COMBINE_PARTS = 2


def _cparams(sem):
    return pltpu.CompilerParams(dimension_semantics=sem, vmem_limit_bytes=VMEM_LIMIT)


def _layer_norm_rows(z, g, b):
    mu = jnp.mean(z, axis=-1, keepdims=True)
    zc = z - mu
    var = jnp.mean(zc * zc, axis=-1, keepdims=True)
    return zc * lax.rsqrt(var + LN_EPS) * g + b


def _rms_rows(y):
    return y * lax.rsqrt(jnp.mean(y * y, axis=-1, keepdims=True) + RMS_EPS)


def _matmul_kernel(x_ref, w_ref, o_ref):
    o_ref[...] = jnp.dot(x_ref[...].astype(BF16), w_ref[...], preferred_element_type=F32).astype(o_ref.dtype)


def _in_proj(xb, w_all, l):
    T, K = xb.shape
    N = w_all.shape[2]
    tm = min(TM_PROJ, T)
    return pl.pallas_call(
        _matmul_kernel,
        grid=(T // tm, N // TN_PROJ),
        in_specs=[pl.BlockSpec((tm, K), lambda i, j: (i, 0)),
                  pl.BlockSpec((None, K, TN_PROJ), lambda i, j: (l, 0, j))],
        out_specs=pl.BlockSpec((tm, TN_PROJ), lambda i, j: (i, j)),
        out_shape=jax.ShapeDtypeStruct((T, N), BF16),
        compiler_params=_cparams(("parallel", "arbitrary")),
        name="in_proj",
    )(xb, w_all)


def _mla_prep_kernel(q_ref, ckv_ref, cos_ref, sin_ref, g_ref, wukv_ref, qt_ref, ko_ref, vt_ref, *, qscale):
    cos = cos_ref[...]
    sin = sin_ref[...]
    lane = lax.broadcasted_iota(jnp.int32, cos.shape, 1)
    row_chunk = lax.broadcasted_iota(jnp.int32, cos.shape, 0) // CHUNK
    first_half = (lane & (QK_ROPE // 2)) == 0
    low = lane < QK_ROPE

    def rope(r):
        partner = jnp.where(first_half, pltpu.roll(r, LANES - QK_ROPE // 2, axis=1), pltpu.roll(r, QK_ROPE // 2, axis=1))
        return r * cos + partner * sin

    k_rope = None
    for h in range(MLA_HEADS):
        c0 = h * HEAD_W
        qt_ref[h, 0:LANES, :] = (q_ref[:, c0:c0 + LANES].astype(F32) * qscale).T.astype(BF16)
        rr = rope(q_ref[:, c0 + LANES:c0 + HEAD_W].astype(F32))
        qt_ref[h, LANES:HEAD_W, :] = jnp.where(low, rr * qscale, 0.0).T.astype(BF16)
        if h == 0:
            stair = jnp.where(row_chunk > lane - QK_ROPE, -MASK_BIG, 0.0)
            k_rope = jnp.where(low, pltpu.roll(rr, QK_ROPE, axis=1), stair).astype(BF16)

    c = ckv_ref[...].astype(F32)
    cn = (_rms_rows(c) * g_ref[...]).astype(BF16)
    kv = jnp.dot(cn, wukv_ref[...], preferred_element_type=F32)
    for h in range(MLA_HEADS):
        c0 = h * HEAD_W
        ko_ref[:, c0:c0 + LANES] = kv[:, c0:c0 + LANES].astype(BF16)
        ko_ref[:, c0 + LANES:c0 + HEAD_W] = k_rope
        vt_ref[h, 0:V_HEAD, :] = kv[:, c0 + LANES:c0 + HEAD_W].T.astype(BF16)
        vt_ref[h, V_HEAD:V_ROWS, :] = jnp.ones((V_ROWS - V_HEAD, kv.shape[0]), BF16)


def _mla_prep(proj, cos4, sin4, kv_g, w_ukv_all, l):
    T = proj.shape[0]
    tm = min(TM_PREP, T)
    qscale = math.log2(math.e) / math.sqrt(QK_NOPE + QK_ROPE)
    return pl.pallas_call(
        functools.partial(_mla_prep_kernel, qscale=qscale),
        grid=(T // tm,),
        in_specs=[pl.BlockSpec((tm, Q_COLS), lambda i: (i, 0)),
                  pl.BlockSpec((tm, GROUP_W), lambda i: (i, COL_CKV)),
                  pl.BlockSpec((tm, LANES), lambda i: (i, 0)),
                  pl.BlockSpec((tm, LANES), lambda i: (i, 0)),
                  pl.BlockSpec((1, KV_RANK), lambda i: (0, 0)),
                  pl.BlockSpec((None,) + w_ukv_all.shape[1:], lambda i: (l, 0, 0))],
        out_specs=[pl.BlockSpec((None, MLA_HEADS, HEAD_W, tm), lambda i: (i, 0, 0, 0)),
                   pl.BlockSpec((tm, Q_COLS), lambda i: (i, 0)),
                   pl.BlockSpec((None, MLA_HEADS, V_ROWS, tm), lambda i: (i, 0, 0, 0))],
        out_shape=[jax.ShapeDtypeStruct((T // tm, MLA_HEADS, HEAD_W, tm), BF16),
                   jax.ShapeDtypeStruct((T, Q_COLS), BF16),
                   jax.ShapeDtypeStruct((T // tm, MLA_HEADS, V_ROWS, tm), BF16)],
        compiler_params=_cparams(("parallel",)),
        name="mla_prep",
    )(proj, proj, cos4, sin4, kv_g, w_ukv_all)


def _attention_kernel(q_ref, k_ref, vt_ref, o_ref, q2_ref, s0_ref, s1_ref, m_ref, acc_ref, *, tb):
    i = pl.program_id(2)
    hp = ATT_HEADS
    q = q_ref[...]
    dim = lax.broadcasted_iota(jnp.int32, q.shape, 1)
    query_chunk = lax.broadcasted_iota(jnp.int32, q.shape, 2) // CHUNK
    q2_ref[0] = q
    q2_ref[1] = jnp.where(dim == QK_NOPE + QK_ROPE + query_chunk, jnp.ones_like(q), q)
    m_ref[...] = jnp.full(m_ref.shape, -jnp.inf, F32)
    acc_ref[...] = jnp.zeros(acc_ref.shape, F32)
    all_heads = tuple(range(hp))

    def scores(j, s_ref, heads=all_heads):
        r0 = pl.multiple_of(j * tb, tb)
        sel = (j == i).astype(jnp.int32)
        for h in heads:
            cs = slice(h * HEAD_W, (h + 1) * HEAD_W)
            s_ref[h] = jnp.dot(k_ref[pl.ds(r0, tb), cs], q2_ref[sel, h],
                               preferred_element_type=F32)

    def softmax_pv(j, s_ref, heads=all_heads):
        for h in heads:
            s = s_ref[h]
            m_old = m_ref[h]
            m_new = jnp.maximum(m_old, jnp.max(s, axis=0, keepdims=True))
            p = jnp.exp2(s - m_new)
            a = jnp.exp2(m_old - m_new)
            acc_ref[h] = a * acc_ref[h] + jnp.dot(vt_ref[j, h], p.astype(BF16), preferred_element_type=F32)
            m_ref[h] = m_new

    scores(0, s0_ref)
    bufs = (s0_ref, s1_ref)
    U = ATT_BLOCKS_PER_TRIP

    def trip(g, carry):
        j = U * g
        for u in range(U):
            for h in all_heads:
                scores(j + u + 1, bufs[(u + 1) % 2], (h,))
                softmax_pv(j + u, bufs[u % 2], (h,))
        return carry

    trips = i // U
    lax.fori_loop(0, trips, trip, 0)
    base = U * trips
    for u in range(U):
        j = base + u

        @pl.when(j < i)
        def _():
            scores(j + 1, bufs[(u + 1) % 2])
            softmax_pv(j, bufs[u % 2])

        @pl.when(j == i)
        def _():
            softmax_pv(j, bufs[u % 2])

    for h in range(hp):
        o_ref[:, h * V_HEAD:(h + 1) * V_HEAD] = (acc_ref[h, 0:V_HEAD] / acc_ref[h, V_HEAD:V_HEAD + 1]).T


def _attention(q_t, k_cat, v_t, B, S):
    T = B * S
    tb = v_t.shape[-1]
    nq = S // tb
    hp = ATT_HEADS
    resident = dict(pipeline_mode=pl.Buffered(1))
    return pl.pallas_call(
        functools.partial(_attention_kernel, tb=tb),
        grid=(B, MLA_HEADS // hp, nq),
        in_specs=[pl.BlockSpec((None, hp, HEAD_W, tb), lambda b, h, i: (b * nq + i, h, 0, 0)),
                  pl.BlockSpec((S, hp * HEAD_W), lambda b, h, i: (b, h), **resident),
                  pl.BlockSpec((nq, hp, V_ROWS, tb), lambda b, h, i: (b, h, 0, 0), **resident)],
        out_specs=pl.BlockSpec((tb, hp * V_HEAD), lambda b, h, i: (b * nq + i, h)),
        out_shape=jax.ShapeDtypeStruct((T, MLA_HEADS * V_HEAD), F32),
        scratch_shapes=[pltpu.VMEM((2, hp, HEAD_W, tb), BF16),
                        pltpu.VMEM((hp, tb, tb), F32), pltpu.VMEM((hp, tb, tb), F32),
                        pltpu.VMEM((hp, 1, tb), F32), pltpu.VMEM((hp, V_ROWS, tb), F32)],
        compiler_params=_cparams(("parallel", "parallel", "arbitrary")),
        name="attention",
    )(q_t, k_cat, v_t)


def _gelu(x):
    return 0.5 * x * (1.0 + lax.erf(x * (1.0 / math.sqrt(2.0))))


def _mixers_kernel(cb_ref, cc_ref, ch_ref, ph_ref, gu_ref, gv_ref, hcc_ref, hch_ref, hph_ref,
                   convw_ref, poolw_ref, lng_ref, lnb_ref, sguw_ref, sgub_ref, gain_ref,
                   yc_ref, yp_ref, ys_ref,
                   g_scr, e_scr, a_scr, b_scr, y_scr, *, tm, tiles_per_seq):
    i = pl.program_id(0)
    t0 = (i % tiles_per_seq) * tm
    keep = jnp.where(t0 == 0, 0.0, 1.0).astype(F32)

    g_scr[0:HALO, :] = hcc_ref[...].astype(F32) * hch_ref[...].astype(F32) * keep
    g_scr[HALO:, :] = cc_ref[...].astype(F32) * ch_ref[...].astype(F32)
    conv = convw_ref[CONV_W - 1:CONV_W, :] * g_scr[HALO:HALO + tm, :]
    for j in range(CONV_W - 1):
        off = HALO - (CONV_W - 1) + j
        conv = conv + convw_ref[j:j + 1, :] * g_scr[off:off + tm, :]
    yc = cb_ref[...].astype(F32) * conv
    yc_ref[...] = (_rms_rows(yc) * gain_ref[:, GROUP_W:2 * GROUP_W]).astype(BF16)

    e_scr[0:HALO, :] = hph_ref[...].astype(F32) * keep
    e_scr[HALO:, :] = ph_ref[...].astype(F32)
    n = tm + HALO
    pos = (t0 + 1 + lax.broadcasted_iota(jnp.int32, (tm, 1), 0)).astype(F32)
    for g, w in enumerate(POOL_WINDOWS):
        cs = slice(g * POOL_CH, (g + 1) * POOL_CH)
        levels = g + 1
        src = None
        bufs = (a_scr, b_scr)
        total = None
        for lev in range(1, levels + 1):
            shift = 1 << (lev - 1)
            lo = 8 * lev if lev < levels else HALO
            if src is None:
                cur = e_scr[lo:n, cs] + e_scr[lo - shift:n - shift, cs]
            else:
                cur = src[lo:n, :] + src[lo - shift:n - shift, :]
            if lev < levels:
                dst = bufs[(lev - 1) % 2]
                dst[lo:n, :] = cur
                src = dst
            else:
                total = cur
        inv = 1.0 / jnp.minimum(pos, float(w))
        pooled = total * inv - e_scr[HALO:, cs]
        y_scr[:, cs] = jnp.dot(pooled.astype(BF16), poolw_ref[g], preferred_element_type=F32)
    yp = y_scr[...]
    yp_ref[...] = (_rms_rows(yp) * gain_ref[:, 2 * GROUP_W:3 * GROUP_W]).astype(BF16)

    u = _gelu(gu_ref[...].astype(F32))
    vn = _layer_norm_rows(_gelu(gv_ref[...].astype(F32)), lng_ref[...], lnb_ref[...]).astype(BF16)
    row = lax.broadcasted_iota(jnp.int32, (SGU_LEN, SGU_LEN), 0)
    col = lax.broadcasted_iota(jnp.int32, (SGU_LEN, SGU_LEN), 1)
    for g in range(SGU_GROUPS):
        cs = slice(g * SGU_CH, (g + 1) * SGU_CH)
        wg = jnp.where(col <= row, sguw_ref[g], 0.0).astype(BF16)
        bias = sgub_ref[:, g:g + 1]
        for c in range(tm // SGU_LEN):
            rs = slice(c * SGU_LEN, (c + 1) * SGU_LEN)
            mixed = jnp.dot(wg, vn[rs, cs], preferred_element_type=F32) + bias
            y_scr[rs, cs] = u[rs, cs] * mixed
    ysg = y_scr[...]
    ys_ref[...] = (_rms_rows(ysg) * gain_ref[:, 3 * GROUP_W:4 * GROUP_W]).astype(BF16)


def _mixers(proj, conv_w, pool_w, ln_g, ln_b, sgu_w, sgu_bt, gain, S):
    T = proj.shape[0]
    tm = min(TM_MIX, S)
    tiles_per_seq = S // tm
    hb = tm // HALO

    def col(c):
        return pl.BlockSpec((tm, GROUP_W), lambda i: (i, c))

    def halo(c):
        return pl.BlockSpec((HALO, GROUP_W), lambda i: (jnp.maximum(i * hb - 1, 0), c))

    def full(a):
        return pl.BlockSpec(a.shape, lambda i: (0,) * a.ndim)

    out = jax.ShapeDtypeStruct((T, GROUP_W), BF16)
    return pl.pallas_call(
        functools.partial(_mixers_kernel, tm=tm, tiles_per_seq=tiles_per_seq),
        grid=(T // tm,),
        in_specs=[col(COL_CB), col(COL_CC), col(COL_CH), col(COL_PH), col(COL_GU), col(COL_GV),
                  halo(COL_CC), halo(COL_CH), halo(COL_PH),
                  full(conv_w), full(pool_w), full(ln_g), full(ln_b), full(sgu_w), full(sgu_bt), full(gain)],
        out_specs=[pl.BlockSpec((tm, GROUP_W), lambda i: (i, 0))] * 3,
        out_shape=[out, out, out],
        scratch_shapes=[pltpu.VMEM((tm + HALO, GROUP_W), F32), pltpu.VMEM((tm + HALO, GROUP_W), F32),
                        pltpu.VMEM((tm + HALO, POOL_CH), F32), pltpu.VMEM((tm + HALO, POOL_CH), F32),
                        pltpu.VMEM((tm, GROUP_W), F32)],
        compiler_params=_cparams(("parallel",)),
        name="mixers",
    )(proj, proj, proj, proj, proj, proj, proj, proj, proj,
      conv_w, pool_w, ln_g, ln_b, sgu_w, sgu_bt, gain)


def _out_proj_kernel(*refs, with_router):
    if with_router:
        (o_ref, yc_ref, yp_ref, ys_ref, gain_ref, wo_ref, x_ref, g_ref, b_ref, rw_ref,
         x1_ref, x1b_ref, lg_ref) = refs
    else:
        (o_ref, yc_ref, yp_ref, ys_ref, gain_ref, wo_ref, x_ref, g_ref, b_ref,
         x1_ref, x1b_ref) = refs
    hr = o_ref.shape[0] // OUT_PARTS
    for part in range(OUT_PARTS):
        rs = slice(part * hr, (part + 1) * hr)
        ym = (_rms_rows(o_ref[rs, :]) * gain_ref[:, 0:GROUP_W]).astype(BF16)
        mix = jnp.dot(ym, wo_ref[0:GROUP_W, :], preferred_element_type=F32)
        for gi, y_ref in enumerate((yc_ref, yp_ref, ys_ref), start=1):
            mix = mix + jnp.dot(y_ref[rs, :], wo_ref[gi * GROUP_W:(gi + 1) * GROUP_W, :], preferred_element_type=F32)
        x1 = _layer_norm_rows(ALPHA * x_ref[rs, :] + mix, g_ref[...], b_ref[...])
        x1_ref[rs, :] = x1
        x1_hi = x1.astype(BF16)
        x1b_ref[rs, :] = x1_hi
        if with_router:
            x1_lo = (x1 - x1_hi.astype(F32)).astype(BF16)
            hh_hl = jnp.dot(x1_hi, rw_ref[...], preferred_element_type=F32)
            lh = jnp.dot(x1_lo, rw_ref[:, 0:LANES], preferred_element_type=F32)
            lg_ref[rs, :] = hh_hl[:, 0:LANES] + (hh_hl[:, LANES:] + lh)


def _out_proj(o_mla, yc, yp, ys, gain, wo_all, l, x, ln_g, ln_b, router_hl=None):
    T, D = x.shape
    tm = min(TM_OUT, T)
    with_router = router_hl is not None

    def rows(w):
        return pl.BlockSpec((tm, w), lambda i: (i, 0))

    def full(a):
        return pl.BlockSpec(a.shape, lambda i: (0,) * a.ndim)

    wo_spec = pl.BlockSpec((None,) + wo_all.shape[1:], lambda i: (l, 0, 0))
    in_specs = [rows(GROUP_W)] * 4 + [full(gain), wo_spec, rows(D), full(ln_g), full(ln_b)]
    args = [o_mla, yc, yp, ys, gain, wo_all, x, ln_g, ln_b]
    out_specs = [rows(D), rows(D)]
    out_shape = [jax.ShapeDtypeStruct((T, D), F32), jax.ShapeDtypeStruct((T, D), BF16)]
    if with_router:
        in_specs.append(full(router_hl))
        args.append(router_hl)
        out_specs.append(rows(LANES))
        out_shape.append(jax.ShapeDtypeStruct((T, LANES), F32))
    return pl.pallas_call(
        functools.partial(_out_proj_kernel, with_router=with_router),
        grid=(T // tm,),
        in_specs=in_specs,
        out_specs=out_specs,
        out_shape=out_shape,
        compiler_params=_cparams(("parallel",)),
        name="out_proj",
    )(*args)


def _swiglu(xb, wg, wu):
    hg = jnp.dot(xb, wg, preferred_element_type=F32)
    hu = jnp.dot(xb, wu, preferred_element_type=F32)
    return (hg * jax.nn.sigmoid(hg) * hu).astype(BF16)


def _dense_up_kernel(*refs, n_cast):
    x_ref, wg_ref, wu_ref = refs[:3]
    cast_in = refs[3:3 + n_cast]
    h_ref = refs[3 + n_cast]
    cast_out = refs[4 + n_cast:4 + 2 * n_cast]
    for src_ref, dst_ref in zip(cast_in, cast_out):
        dst_ref[...] = src_ref[...].astype(dst_ref.dtype)
    h_ref[...] = _swiglu(x_ref[...], wg_ref[...], wu_ref[...])


def _dense_down_kernel(h_ref, wd_ref, xres_ref, g_ref, b_ref, o_ref, ob_ref):
    y = jnp.dot(h_ref[...], wd_ref[...], preferred_element_type=F32)
    x2 = _layer_norm_rows(ALPHA * xres_ref[...] + y, g_ref[...], b_ref[...])
    o_ref[...] = x2
    ob_ref[...] = x2.astype(BF16)


def _ffn_dense(xb, wg, wu, wd, e, xres, ln_g, ln_b, cast=()):
    T, D = xb.shape
    F = wg.shape[2]
    tm = min(TM_DENSE, T)
    nb = T // tm
    fh = F // MOE_F_SPLIT
    resident = dict(pipeline_mode=pl.Buffered(1))
    in_specs = [pl.BlockSpec((tm, D), lambda f, i: (i, 0)),
                pl.BlockSpec((None, D, fh), lambda f, i: (e, 0, f), **resident),
                pl.BlockSpec((None, D, fh), lambda f, i: (e, 0, f), **resident)]
    out_specs = [pl.BlockSpec((tm, fh), lambda f, i: (i, f))]
    out_shape = [jax.ShapeDtypeStruct((T, F), BF16)]
    steps = MOE_F_SPLIT * nb
    for arr, part, n_parts in cast:
        rows, cols = arr.shape[0] // n_parts, arr.shape[1]
        rb = rows // steps
        assert rb * steps == rows and rb % 16 == 0, (rows, steps)
        in_specs.append(pl.BlockSpec((rb, cols), lambda f, i, off=part * steps: (off + f * nb + i, 0)))
        out_specs.append(pl.BlockSpec((rb, cols), lambda f, i: (f * nb + i, 0)))
        out_shape.append(jax.ShapeDtypeStruct((rows, cols), BF16))
    h, *cast_out = pl.pallas_call(
        functools.partial(_dense_up_kernel, n_cast=len(cast)),
        grid=(MOE_F_SPLIT, nb),
        in_specs=in_specs,
        out_specs=out_specs,
        out_shape=out_shape,
        compiler_params=_cparams(("arbitrary", "arbitrary")),
        name="dense_up",
    )(xb, wg, wu, *[c[0] for c in cast])
    rows_spec = pl.BlockSpec((tm, D), lambda i: (i, 0))
    vec = pl.BlockSpec((1, D), lambda i: (0, 0))
    x2, x2b = pl.pallas_call(
        _dense_down_kernel,
        grid=(nb,),
        in_specs=[pl.BlockSpec((tm, F), lambda i: (i, 0)),
                  pl.BlockSpec((None, F, D), lambda i: (e, 0, 0), **resident), rows_spec, vec, vec],
        out_specs=[rows_spec, rows_spec],
        out_shape=[jax.ShapeDtypeStruct((T, D), F32), jax.ShapeDtypeStruct((T, D), BF16)],
        compiler_params=_cparams(("arbitrary",)),
        name="dense_down",
    )(h, wd, xres, ln_g, ln_b)
    return (x2, x2b, *cast_out)


def _moe_up_kernel(be_ref, nv_ref, x_ref, wg_ref, wu_ref, h_ref):
    i = pl.program_id(1)

    @pl.when(i < nv_ref[0])
    def _():
        h_ref[...] = _swiglu(x_ref[...], wg_ref[...], wu_ref[...])

    @pl.when(i >= nv_ref[0])
    def _():
        h_ref[...] = jnp.zeros(h_ref.shape, h_ref.dtype)


def _moe_down_kernel(be_ref, nv_ref, *refs):
    h_ref, wd_ref, o_ref = refs[0], refs[1], refs[-1]
    i = pl.program_id(0)

    @pl.when(i < nv_ref[0])
    def _():
        o_ref[...] = jnp.dot(h_ref[...], wd_ref[...], preferred_element_type=F32).astype(o_ref.dtype)

    @pl.when(i >= nv_ref[0])
    def _():
        o_ref[...] = jnp.zeros(o_ref.shape, o_ref.dtype)


def _ffn_moe(x_rows_b, wg, wu, wd, block_e, n_valid, y_buf, blk0, total_rows):
    Rc, D = x_rows_b.shape
    F = wg.shape[2]
    tm = min(TM_FFN, Rc)
    nb = Rc // tm
    fh = F // MOE_F_SPLIT
    resident = dict(pipeline_mode=pl.Buffered(1))
    h = pl.pallas_call(
        _moe_up_kernel,
        grid_spec=pltpu.PrefetchScalarGridSpec(
            num_scalar_prefetch=2,
            grid=(MOE_F_SPLIT, nb),
            in_specs=[pl.BlockSpec((tm, D), lambda f, i, be, nv: (i, 0)),
                      pl.BlockSpec((None, D, fh), lambda f, i, be, nv: (be[i], 0, f), **resident),
                      pl.BlockSpec((None, D, fh), lambda f, i, be, nv: (be[i], 0, f), **resident)],
            out_specs=pl.BlockSpec((tm, fh), lambda f, i, be, nv: (i, f))),
        out_shape=jax.ShapeDtypeStruct((Rc, F), BF16),
        compiler_params=_cparams(("arbitrary", "arbitrary")),
        name="moe_up",
    )(block_e, n_valid, x_rows_b, wg, wu)
    in_specs = [pl.BlockSpec((tm, F), lambda i, be, nv: (i, 0)),
                pl.BlockSpec((None, F, D), lambda i, be, nv: (be[i], 0, 0), **resident)]
    args = [block_e, n_valid, h, wd]
    aliases = {}
    if y_buf is not None:
        in_specs.append(pl.BlockSpec(memory_space=pl.ANY))
        args.append(y_buf)
        aliases = {len(args) - 1: 0}
    return pl.pallas_call(
        _moe_down_kernel,
        grid_spec=pltpu.PrefetchScalarGridSpec(
            num_scalar_prefetch=2,
            grid=(nb,),
            in_specs=in_specs,
            out_specs=pl.BlockSpec((tm, D), lambda i, be, nv: (blk0 + i, 0))),
        out_shape=jax.ShapeDtypeStruct((total_rows, D), BF16),
        input_output_aliases=aliases,
        compiler_params=_cparams(("arbitrary",)),
        name="moe_down",
    )(*args)


def _combine_kernel(x_ref, y0_ref, y1_ref, g0_ref, g1_ref, g_ref, b_ref, *refs):
    o_ref, ob_ref = refs[-2:]
    ffn = y0_ref[...].astype(F32) * g0_ref[...] + y1_ref[...].astype(F32) * g1_ref[...]
    x2 = _layer_norm_rows(ALPHA * x_ref[...] + ffn, g_ref[...], b_ref[...])
    o_ref[...] = x2
    ob_ref[...] = x2.astype(BF16)


def _combine(x, y0, y1, g0, g1, ln_g, ln_b, part, bufs):
    T, D = x.shape
    Tp = y0.shape[0]
    tm = min(TM_LN, Tp)
    nb = Tp // tm
    rows_x = pl.BlockSpec((tm, D), lambda i: (part * nb + i, 0))
    rows = pl.BlockSpec((tm, D), lambda i: (i, 0))
    gate = pl.BlockSpec((tm, 1), lambda i: (i, 0))
    vec = pl.BlockSpec((1, D), lambda i: (0, 0))
    in_specs = [rows_x, rows, rows, gate, gate, vec, vec]
    args = [x, y0, y1, g0, g1, ln_g, ln_b]
    aliases = {}
    if bufs is not None:
        in_specs += [pl.BlockSpec(memory_space=pl.ANY)] * 2
        args += list(bufs)
        aliases = {len(args) - 2: 0, len(args) - 1: 1}
    return pl.pallas_call(
        _combine_kernel,
        grid=(nb,),
        in_specs=in_specs,
        out_specs=[rows_x, rows_x],
        out_shape=[jax.ShapeDtypeStruct((T, D), F32), jax.ShapeDtypeStruct((T, D), BF16)],
        input_output_aliases=aliases,
        compiler_params=_cparams(("parallel",)),
        name="moe_combine",
    )(*args)


def _pad_w_in(w_in):
    L, D, _ = w_in.shape
    q_dim = MLA_HEADS * (QK_NOPE + QK_ROPE)
    wq = w_in[:, :, :q_dim].reshape(L, D, MLA_HEADS, QK_NOPE + QK_ROPE)
    kr0 = q_dim + KV_RANK
    w_kr = w_in[:, :, kr0:kr0 + QK_ROPE]
    pad = jnp.zeros((L, D, MLA_HEADS, HEAD_W - QK_NOPE - QK_ROPE), w_in.dtype).at[:, :, 0, :].set(w_kr)
    q_part = jnp.concatenate([wq, pad], axis=-1).reshape(L, D, Q_COLS)
    rest = jnp.concatenate([w_in[:, :, q_dim:kr0], w_in[:, :, kr0 + QK_ROPE:]], axis=-1)
    return jnp.concatenate([q_part, rest], axis=-1).astype(BF16)


def _rope_tables(positions):
    inv_freq = ROPE_THETA ** (-jnp.arange(0, QK_ROPE, 2, dtype=F32) / QK_ROPE)
    ang = positions.astype(F32).reshape(-1)[:, None] * inv_freq
    cos, sin = jnp.cos(ang), jnp.sin(ang)
    cos4 = jnp.concatenate([cos, cos, cos, cos], axis=-1)
    sin4 = jnp.concatenate([-sin, sin, -sin, sin], axis=-1)
    return cos4, sin4


def _route(logits, tm):
    T = logits.shape[0]
    A = T * TOP_K
    top_logit, top_e = lax.top_k(logits, TOP_K)
    gates = jax.nn.softmax(top_logit, axis=-1)
    flat_e = jnp.concatenate([top_e[:, k] for k in range(TOP_K)]).astype(jnp.int32)
    onehot = (flat_e[:, None] == jnp.arange(N_EXPERTS, dtype=jnp.int32)[None, :]).astype(jnp.int32)
    csum = jnp.cumsum(onehot, axis=0)
    counts = csum[-1]
    rank = jnp.sum(csum * onehot, axis=1) - 1
    padded = (counts + tm - 1) // tm * tm
    pad_ends = jnp.cumsum(padded)
    pad_starts = pad_ends - padded
    dest = (pad_starts[flat_e] + rank).astype(jnp.int32)
    n_blocks = -(-(A + N_EXPERTS * (tm - 1)) // tm)
    n_pad = n_blocks * tm
    block_start = jnp.arange(n_blocks, dtype=jnp.int32) * tm
    block_e = jnp.minimum(jnp.searchsorted(pad_ends, block_start, side='right'), N_EXPERTS - 1).astype(jnp.int32)
    arange_a = jnp.arange(A, dtype=jnp.int32)
    _, order = lax.sort_key_val(flat_e * A + arange_a, arange_a)
    starts = jnp.cumsum(counts) - counts
    e_s = jnp.repeat(block_e, tm)
    r = jnp.arange(n_pad, dtype=jnp.int32) - pad_starts[e_s].astype(jnp.int32)
    src = jnp.clip(starts[e_s].astype(jnp.int32) + r, 0, A - 1)
    tok_sorted = order % T
    slot_tok = jnp.where(r < counts[e_s], tok_sorted.at[src].get(mode="promise_in_bounds"), 0).astype(jnp.int32)
    n_valid = (pad_ends[-1] // tm).astype(jnp.int32).reshape(1)
    return dest.reshape(TOP_K, T), gates, slot_tok, block_e, n_valid


def kernel(x, positions, w_in, kv_norm_g, w_ukv, conv_w, pool_w, sgu_ln_g, sgu_ln_b, sgu_w, sgu_b, mix_gain, w_o, ln1_g, ln1_b, ffn_wg, ffn_wu, ffn_wd, router_w, exp_wg, exp_wu, exp_wd, ln2_g, ln2_b):
    B, S, D = x.shape
    T = B * S
    L = w_in.shape[0]
    cos4, sin4 = _rope_tables(positions)
    w_in_b = _pad_w_in(w_in)
    w_ukv_b = w_ukv.astype(BF16)
    w_o_b = w_o.astype(BF16)
    pool_w_b = pool_w.astype(BF16)
    sgu_bt = jnp.swapaxes(sgu_b, 1, 2)
    router_pad = jnp.pad(router_w, ((0, 0), (0, 0), (0, LANES - N_EXPERTS)))
    router_hi = router_pad.astype(BF16)
    router_hl = jnp.concatenate([router_hi, (router_pad - router_hi.astype(F32)).astype(BF16)], axis=2)
    ffn_w = (ffn_wg.astype(BF16), ffn_wu.astype(BF16), ffn_wd.astype(BF16))
    n_moe = exp_wg.shape[0]
    exp_f32 = tuple(w.reshape(-1, w.shape[-1]) for w in (exp_wg, exp_wu, exp_wd))
    exp_w = None

    xf = x.reshape(T, D)
    xb = xf
    tm_moe = min(TM_FFN, T)
    for l in range(L):
        proj = _in_proj(xb, w_in_b, l)
        q_t, k_cat, v_t = _mla_prep(proj, cos4, sin4, kv_norm_g[l][None, :], w_ukv_b, l)
        o_mla = _attention(q_t, k_cat, v_t, B, S)
        yc, yp, ys = _mixers(proj, conv_w[l], pool_w_b[l], sgu_ln_g[l][None, :], sgu_ln_b[l][None, :],
                             sgu_w[l], sgu_bt[l], mix_gain[l][None, :], S)
        ln1 = (ln1_g[l][None, :], ln1_b[l][None, :])
        ln2 = (ln2_g[l][None, :], ln2_b[l][None, :])
        e = l // 2
        if l % 2 == 0:
            x1, x1b = _out_proj(o_mla, yc, yp, ys, mix_gain[l][None, :], w_o_b, l, xf, *ln1)
            cast = [(w, e, n_moe) for w in exp_f32] if e < n_moe else []
            xf, xb, *cast_out = _ffn_dense(x1b, *ffn_w, e, x1, *ln2, cast=cast)
            if cast_out:
                exp_w = tuple(c.reshape((N_EXPERTS, -1, c.shape[-1])) for c in cast_out)
        else:
            x1, x1b, logits = _out_proj(o_mla, yc, yp, ys, mix_gain[l][None, :], w_o_b, l, xf, *ln1,
                                        router_hl=router_hl[e])
            dest, gates, slot_tok, block_e, n_valid = _route(logits[:, :N_EXPERTS], tm_moe)
            n_blocks = block_e.shape[0]
            chunks = max(c for c in (MOE_CHUNKS, 2, 1) if n_blocks % c == 0)
            cb = n_blocks // chunks
            y_rows = None
            for c in range(chunks):
                x_rows = x1b.at[slot_tok[c * cb * tm_moe:(c + 1) * cb * tm_moe]].get(mode="promise_in_bounds")
                y_rows = _ffn_moe(x_rows, *exp_w, block_e[c * cb:(c + 1) * cb],
                                  jnp.clip(n_valid - c * cb, 0, cb), y_rows, c * cb, n_blocks * tm_moe)
            parts = COMBINE_PARTS if T % (COMBINE_PARTS * TM_LN) == 0 else 1
            tp = T // parts
            bufs = None
            for p in range(parts):
                ts = slice(p * tp, (p + 1) * tp)
                y0 = y_rows.at[dest[0, ts]].get(mode="promise_in_bounds")
                y1 = y_rows.at[dest[1, ts]].get(mode="promise_in_bounds")
                bufs = _combine(x1, y0, y1, gates[ts, 0:1], gates[ts, 1:2], *ln2, p, bufs)
            xf, xb = bufs
    return xf.reshape(B, S, D)
```

```python
import functools
import math

import jax
import jax.numpy as jnp
from jax import lax
from jax.experimental import pallas as pl
from jax.experimental.pallas import tpu as pltpu

F32 = jnp.float32
BF16 = jnp.bfloat16

CHUNK = 64
GROUP_W = 512
MLA_HEADS = 4
QK_NOPE = 128
QK_ROPE = 64
V_HEAD = 128
KV_RANK = 512
ROPE_THETA = 10000.0
CONV_W = 3
POOL_WINDOWS = (2, 4, 8, 16)
POOL_CH = 128
SGU_LEN = 128
SGU_GROUPS = 4
SGU_CH = 128
N_EXPERTS = 8
TOP_K = 2
DEPTH = 4
ALPHA = (2.0 * DEPTH) ** 0.25
LN_EPS = 1e-5
RMS_EPS = 1e-6

LANES = 128
HEAD_W = 2 * LANES
V_ROWS = 128 + 16
VMEM_LIMIT = 56 * 1024 * 1024

Q_COLS = MLA_HEADS * HEAD_W
COL_CKV, COL_CB, COL_CC, COL_CH, COL_PH, COL_GU, COL_GV = 2, 3, 4, 5, 6, 7, 8

HALO = 32

TM_PROJ, TN_PROJ = 512, 4608
TB_ATT = 1024
ATT_HEADS = 2
ATT_BLOCKS_PER_TRIP = 2
TM_PREP = TB_ATT
MASK_BIG = 2.0 ** 100
TM_MIX = 512
TM_OUT, OUT_PARTS = 512, 2
TM_LN = 256
TM_FFN = 512
TM_DENSE = 256
MOE_F_SPLIT = 2
MOE_CHUNKS = 4
COMBINE_PARTS = 2


def _cparams(sem):
    return pltpu.CompilerParams(dimension_semantics=sem, vmem_limit_bytes=VMEM_LIMIT)


def _layer_norm_rows(z, g, b):
    mu = jnp.mean(z, axis=-1, keepdims=True)
    zc = z - mu
    var = jnp.mean(zc * zc, axis=-1, keepdims=True)
    return zc * lax.rsqrt(var + LN_EPS) * g + b


def _rms_rows(y):
    return y * lax.rsqrt(jnp.mean(y * y, axis=-1, keepdims=True) + RMS_EPS)


def _matmul_kernel(x_ref, w_ref, o_ref):
    o_ref[...] = jnp.dot(x_ref[...].astype(BF16), w_ref[...], preferred_element_type=F32).astype(o_ref.dtype)


def _in_proj(xb, w_all, l):
    T, K = xb.shape
    N = w_all.shape[2]
    tm = min(TM_PROJ, T)
    return pl.pallas_call(
        _matmul_kernel,
        grid=(T // tm, N // TN_PROJ),
        in_specs=[pl.BlockSpec((tm, K), lambda i, j: (i, 0)),
                  pl.BlockSpec((None, K, TN_PROJ), lambda i, j: (l, 0, j),
                               **(dict(pipeline_mode=pl.Buffered(1)) if N == TN_PROJ else {}))],
        out_specs=pl.BlockSpec((tm, TN_PROJ), lambda i, j: (i, j)),
        out_shape=jax.ShapeDtypeStruct((T, N), BF16),
        compiler_params=_cparams(("parallel", "arbitrary")),
        name="in_proj",
    )(xb, w_all)


def _mla_prep_kernel(q_ref, ckv_ref, cos_ref, sin_ref, g_ref, wukv_ref, qt_ref, ko_ref, vt_ref, *, qscale):
    cos = cos_ref[...]
    sin = sin_ref[...]
    lane = lax.broadcasted_iota(jnp.int32, cos.shape, 1)
    row_chunk = lax.broadcasted_iota(jnp.int32, cos.shape, 0) // CHUNK
    first_half = (lane & (QK_ROPE // 2)) == 0
    low = lane < QK_ROPE

    def rope(r):
        partner = jnp.where(first_half, pltpu.roll(r, LANES - QK_ROPE // 2, axis=1), pltpu.roll(r, QK_ROPE // 2, axis=1))
        return r * cos + partner * sin

    k_rope = None
    for h in range(MLA_HEADS):
        c0 = h * HEAD_W
        qt_ref[h, 0:LANES, :] = (q_ref[:, c0:c0 + LANES].astype(F32) * qscale).T.astype(BF16)
        rr = rope(q_ref[:, c0 + LANES:c0 + HEAD_W].astype(F32))
        qt_ref[h, LANES:HEAD_W, :] = jnp.where(low, rr * qscale, 0.0).T.astype(BF16)
        if h == 0:
            stair = jnp.where(row_chunk > lane - QK_ROPE, -MASK_BIG, 0.0)
            k_rope = jnp.where(low, pltpu.roll(rr, QK_ROPE, axis=1), stair).astype(BF16)

    c = ckv_ref[...].astype(F32)
    cn = (_rms_rows(c) * g_ref[...]).astype(BF16)
    kv = jnp.dot(cn, wukv_ref[...], preferred_element_type=F32)
    for h in range(MLA_HEADS):
        c0 = h * HEAD_W
        ko_ref[:, c0:c0 + LANES] = kv[:, c0:c0 + LANES].astype(BF16)
        ko_ref[:, c0 + LANES:c0 + HEAD_W] = k_rope
        vt_ref[h, 0:V_HEAD, :] = kv[:, c0 + LANES:c0 + HEAD_W].T.astype(BF16)
        vt_ref[h, V_HEAD:V_ROWS, :] = jnp.ones((V_ROWS - V_HEAD, kv.shape[0]), BF16)


def _mla_prep(proj, cos4, sin4, kv_g, w_ukv_all, l):
    T = proj.shape[0]
    tm = min(TM_PREP, T)
    qscale = math.log2(math.e) / math.sqrt(QK_NOPE + QK_ROPE)
    return pl.pallas_call(
        functools.partial(_mla_prep_kernel, qscale=qscale),
        grid=(T // tm,),
        in_specs=[pl.BlockSpec((tm, Q_COLS), lambda i: (i, 0)),
                  pl.BlockSpec((tm, GROUP_W), lambda i: (i, COL_CKV)),
                  pl.BlockSpec((tm, LANES), lambda i: (i, 0)),
                  pl.BlockSpec((tm, LANES), lambda i: (i, 0)),
                  pl.BlockSpec((1, KV_RANK), lambda i: (0, 0)),
                  pl.BlockSpec((None,) + w_ukv_all.shape[1:], lambda i: (l, 0, 0))],
        out_specs=[pl.BlockSpec((None, MLA_HEADS, HEAD_W, tm), lambda i: (i, 0, 0, 0)),
                   pl.BlockSpec((tm, Q_COLS), lambda i: (i, 0)),
                   pl.BlockSpec((None, MLA_HEADS, V_ROWS, tm), lambda i: (i, 0, 0, 0))],
        out_shape=[jax.ShapeDtypeStruct((T // tm, MLA_HEADS, HEAD_W, tm), BF16),
                   jax.ShapeDtypeStruct((T, Q_COLS), BF16),
                   jax.ShapeDtypeStruct((T // tm, MLA_HEADS, V_ROWS, tm), BF16)],
        compiler_params=_cparams(("parallel",)),
        name="mla_prep",
    )(proj, proj, cos4, sin4, kv_g, w_ukv_all)


def _attention_kernel(q_ref, k_ref, vt_ref, o_ref, q2_ref, s0_ref, s1_ref, m_ref, acc_ref, *, tb):
    i = pl.program_id(2)
    hp = ATT_HEADS
    q = q_ref[...]
    dim = lax.broadcasted_iota(jnp.int32, q.shape, 1)
    query_chunk = lax.broadcasted_iota(jnp.int32, q.shape, 2) // CHUNK
    q2_ref[0] = q
    q2_ref[1] = jnp.where(dim == QK_NOPE + QK_ROPE + query_chunk, jnp.ones_like(q), q)
    m_ref[...] = jnp.full(m_ref.shape, -jnp.inf, F32)
    acc_ref[...] = jnp.zeros(acc_ref.shape, F32)
    all_heads = tuple(range(hp))

    def scores(j, s_ref, heads=all_heads):
        r0 = pl.multiple_of(j * tb, tb)
        sel = (j == i).astype(jnp.int32)
        for h in heads:
            cs = slice(h * HEAD_W, (h + 1) * HEAD_W)
            s_ref[h] = jnp.dot(k_ref[pl.ds(r0, tb), cs], q2_ref[sel, h],
                               preferred_element_type=F32)

    def softmax_pv(j, s_ref, heads=all_heads):
        for h in heads:
            s = s_ref[h]
            m_old = m_ref[h]
            m_new = jnp.maximum(m_old, jnp.max(s, axis=0, keepdims=True))
            p = jnp.exp2(s - m_new)
            a = jnp.exp2(m_old - m_new)
            acc_ref[h] = a * acc_ref[h] + jnp.dot(vt_ref[j, h], p.astype(BF16), preferred_element_type=F32)
            m_ref[h] = m_new

    scores(0, s0_ref)
    bufs = (s0_ref, s1_ref)
    U = ATT_BLOCKS_PER_TRIP

    def trip(g, carry):
        j = U * g
        for u in range(U):
            for h in all_heads:
                scores(j + u + 1, bufs[(u + 1) % 2], (h,))
                softmax_pv(j + u, bufs[u % 2], (h,))
        return carry

    trips = i // U
    lax.fori_loop(0, trips, trip, 0)
    base = U * trips
    for u in range(U):
        j = base + u

        @pl.when(j < i)
        def _():
            for h in all_heads:
                scores(j + 1, bufs[(u + 1) % 2], (h,))
                softmax_pv(j, bufs[u % 2], (h,))

        @pl.when(j == i)
        def _():
            softmax_pv(j, bufs[u % 2])

    for h in range(hp):
        o_ref[:, h * V_HEAD:(h + 1) * V_HEAD] = (acc_ref[h, 0:V_HEAD] / acc_ref[h, V_HEAD:V_HEAD + 1]).T


def _attention(q_t, k_cat, v_t, B, S):
    T = B * S
    tb = v_t.shape[-1]
    nq = S // tb
    hp = ATT_HEADS
    resident = dict(pipeline_mode=pl.Buffered(1))
    return pl.pallas_call(
        functools.partial(_attention_kernel, tb=tb),
        grid=(B, MLA_HEADS // hp, nq),
        in_specs=[pl.BlockSpec((None, hp, HEAD_W, tb), lambda b, h, i: (b * nq + i, h, 0, 0)),
                  pl.BlockSpec((S, hp * HEAD_W), lambda b, h, i: (b, h), **resident),
                  pl.BlockSpec((nq, hp, V_ROWS, tb), lambda b, h, i: (b, h, 0, 0), **resident)],
        out_specs=pl.BlockSpec((tb, hp * V_HEAD), lambda b, h, i: (b * nq + i, h)),
        out_shape=jax.ShapeDtypeStruct((T, MLA_HEADS * V_HEAD), F32),
        scratch_shapes=[pltpu.VMEM((2, hp, HEAD_W, tb), BF16),
                        pltpu.VMEM((hp, tb, tb), F32), pltpu.VMEM((hp, tb, tb), F32),
                        pltpu.VMEM((hp, 1, tb), F32), pltpu.VMEM((hp, V_ROWS, tb), F32)],
        compiler_params=_cparams(("parallel", "parallel", "arbitrary")),
        name="attention",
    )(q_t, k_cat, v_t)


def _gelu(x):
    return 0.5 * x * (1.0 + lax.erf(x * (1.0 / math.sqrt(2.0))))


def _mixers_kernel(cb_ref, cc_ref, ch_ref, ph_ref, gu_ref, gv_ref, hcc_ref, hch_ref, hph_ref,
                   convw_ref, poolw_ref, lng_ref, lnb_ref, sguw_ref, sgub_ref, gain_ref,
                   yc_ref, yp_ref, ys_ref,
                   g_scr, e_scr, a_scr, b_scr, y_scr, *, tm, tiles_per_seq):
    i = pl.program_id(0)
    t0 = (i % tiles_per_seq) * tm
    keep = jnp.where(t0 == 0, 0.0, 1.0).astype(F32)

    g_scr[0:HALO, :] = hcc_ref[...].astype(F32) * hch_ref[...].astype(F32) * keep
    g_scr[HALO:, :] = cc_ref[...].astype(F32) * ch_ref[...].astype(F32)
    conv = convw_ref[CONV_W - 1:CONV_W, :] * g_scr[HALO:HALO + tm, :]
    for j in range(CONV_W - 1):
        off = HALO - (CONV_W - 1) + j
        conv = conv + convw_ref[j:j + 1, :] * g_scr[off:off + tm, :]
    yc = cb_ref[...].astype(F32) * conv
    yc_ref[...] = (_rms_rows(yc) * gain_ref[:, GROUP_W:2 * GROUP_W]).astype(BF16)

    e_scr[0:HALO, :] = hph_ref[...].astype(F32) * keep
    e_scr[HALO:, :] = ph_ref[...].astype(F32)
    n = tm + HALO
    pos = (t0 + 1 + lax.broadcasted_iota(jnp.int32, (tm, 1), 0)).astype(F32)
    for g, w in enumerate(POOL_WINDOWS):
        cs = slice(g * POOL_CH, (g + 1) * POOL_CH)
        levels = g + 1
        src = None
        bufs = (a_scr, b_scr)
        total = None
        for lev in range(1, levels + 1):
            shift = 1 << (lev - 1)
            lo = 8 * lev if lev < levels else HALO
            if src is None:
                cur = e_scr[lo:n, cs] + e_scr[lo - shift:n - shift, cs]
            else:
                cur = src[lo:n, :] + src[lo - shift:n - shift, :]
            if lev < levels:
                dst = bufs[(lev - 1) % 2]
                dst[lo:n, :] = cur
                src = dst
            else:
                total = cur
        inv = 1.0 / jnp.minimum(pos, float(w))
        pooled = total * inv - e_scr[HALO:, cs]
        y_scr[:, cs] = jnp.dot(pooled.astype(BF16), poolw_ref[g], preferred_element_type=F32)
    yp = y_scr[...]
    yp_ref[...] = (_rms_rows(yp) * gain_ref[:, 2 * GROUP_W:3 * GROUP_W]).astype(BF16)

    u = _gelu(gu_ref[...].astype(F32))
    vn = _layer_norm_rows(_gelu(gv_ref[...].astype(F32)), lng_ref[...], lnb_ref[...]).astype(BF16)
    row = lax.broadcasted_iota(jnp.int32, (SGU_LEN, SGU_LEN), 0)
    col = lax.broadcasted_iota(jnp.int32, (SGU_LEN, SGU_LEN), 1)
    for g in range(SGU_GROUPS):
        cs = slice(g * SGU_CH, (g + 1) * SGU_CH)
        wg = jnp.where(col <= row, sguw_ref[g], 0.0).astype(BF16)
        bias = sgub_ref[:, g:g + 1]
        for c in range(tm // SGU_LEN):
            rs = slice(c * SGU_LEN, (c + 1) * SGU_LEN)
            mixed = jnp.dot(wg, vn[rs, cs], preferred_element_type=F32) + bias
            y_scr[rs, cs] = u[rs, cs] * mixed
    ysg = y_scr[...]
    ys_ref[...] = (_rms_rows(ysg) * gain_ref[:, 3 * GROUP_W:4 * GROUP_W]).astype(BF16)


def _mixers(proj, conv_w, pool_w, ln_g, ln_b, sgu_w, sgu_bt, gain, S):
    T = proj.shape[0]
    tm = min(TM_MIX, S)
    tiles_per_seq = S // tm
    hb = tm // HALO

    def col(c):
        return pl.BlockSpec((tm, GROUP_W), lambda i: (i, c))

    def halo(c):
        return pl.BlockSpec((HALO, GROUP_W), lambda i: (jnp.maximum(i * hb - 1, 0), c))

    def full(a):
        return pl.BlockSpec(a.shape, lambda i: (0,) * a.ndim)

    out = jax.ShapeDtypeStruct((T, GROUP_W), BF16)
    return pl.pallas_call(
        functools.partial(_mixers_kernel, tm=tm, tiles_per_seq=tiles_per_seq),
        grid=(T // tm,),
        in_specs=[col(COL_CB), col(COL_CC), col(COL_CH), col(COL_PH), col(COL_GU), col(COL_GV),
                  halo(COL_CC), halo(COL_CH), halo(COL_PH),
                  full(conv_w), full(pool_w), full(ln_g), full(ln_b), full(sgu_w), full(sgu_bt), full(gain)],
        out_specs=[pl.BlockSpec((tm, GROUP_W), lambda i: (i, 0))] * 3,
        out_shape=[out, out, out],
        scratch_shapes=[pltpu.VMEM((tm + HALO, GROUP_W), F32), pltpu.VMEM((tm + HALO, GROUP_W), F32),
                        pltpu.VMEM((tm + HALO, POOL_CH), F32), pltpu.VMEM((tm + HALO, POOL_CH), F32),
                        pltpu.VMEM((tm, GROUP_W), F32)],
        compiler_params=_cparams(("parallel",)),
        name="mixers",
    )(proj, proj, proj, proj, proj, proj, proj, proj, proj,
      conv_w, pool_w, ln_g, ln_b, sgu_w, sgu_bt, gain)


def _out_proj_kernel(*refs, with_router):
    if with_router:
        (o_ref, yc_ref, yp_ref, ys_ref, gain_ref, wo_ref, x_ref, g_ref, b_ref, rw_ref,
         x1_ref, x1b_ref, lg_ref) = refs
    else:
        (o_ref, yc_ref, yp_ref, ys_ref, gain_ref, wo_ref, x_ref, g_ref, b_ref,
         x1_ref, x1b_ref) = refs
    hr = o_ref.shape[0] // OUT_PARTS
    for part in range(OUT_PARTS):
        rs = slice(part * hr, (part + 1) * hr)
        ym = (_rms_rows(o_ref[rs, :]) * gain_ref[:, 0:GROUP_W]).astype(BF16)
        mix = jnp.dot(ym, wo_ref[0:GROUP_W, :], preferred_element_type=F32)
        for gi, y_ref in enumerate((yc_ref, yp_ref, ys_ref), start=1):
            mix = mix + jnp.dot(y_ref[rs, :], wo_ref[gi * GROUP_W:(gi + 1) * GROUP_W, :], preferred_element_type=F32)
        x1 = _layer_norm_rows(ALPHA * x_ref[rs, :] + mix, g_ref[...], b_ref[...])
        x1_ref[rs, :] = x1
        x1_hi = x1.astype(BF16)
        x1b_ref[rs, :] = x1_hi
        if with_router:
            x1_lo = (x1 - x1_hi.astype(F32)).astype(BF16)
            hh_hl = jnp.dot(x1_hi, rw_ref[...], preferred_element_type=F32)
            lh = jnp.dot(x1_lo, rw_ref[:, 0:LANES], preferred_element_type=F32)
            lg_ref[rs, :] = hh_hl[:, 0:LANES] + (hh_hl[:, LANES:] + lh)


def _out_proj(o_mla, yc, yp, ys, gain, wo_all, l, x, ln_g, ln_b, router_hl=None):
    T, D = x.shape
    tm = min(TM_OUT, T)
    with_router = router_hl is not None

    def rows(w):
        return pl.BlockSpec((tm, w), lambda i: (i, 0))

    def full(a):
        return pl.BlockSpec(a.shape, lambda i: (0,) * a.ndim)

    wo_spec = pl.BlockSpec((None,) + wo_all.shape[1:], lambda i: (l, 0, 0))
    in_specs = [rows(GROUP_W)] * 4 + [full(gain), wo_spec, rows(D), full(ln_g), full(ln_b)]
    args = [o_mla, yc, yp, ys, gain, wo_all, x, ln_g, ln_b]
    out_specs = [rows(D), rows(D)]
    out_shape = [jax.ShapeDtypeStruct((T, D), F32), jax.ShapeDtypeStruct((T, D), BF16)]
    if with_router:
        in_specs.append(full(router_hl))
        args.append(router_hl)
        out_specs.append(rows(LANES))
        out_shape.append(jax.ShapeDtypeStruct((T, LANES), F32))
    return pl.pallas_call(
        functools.partial(_out_proj_kernel, with_router=with_router),
        grid=(T // tm,),
        in_specs=in_specs,
        out_specs=out_specs,
        out_shape=out_shape,
        compiler_params=_cparams(("parallel",)),
        name="out_proj",
    )(*args)


def _swiglu(xb, wg, wu):
    hg = jnp.dot(xb, wg, preferred_element_type=F32)
    hu = jnp.dot(xb, wu, preferred_element_type=F32)
    return (hg * jax.nn.sigmoid(hg) * hu).astype(BF16)


def _dense_up_kernel(*refs, n_cast):
    x_ref, wg_ref, wu_ref = refs[:3]
    cast_in = refs[3:3 + n_cast]
    h_ref = refs[3 + n_cast]
    cast_out = refs[4 + n_cast:4 + 2 * n_cast]
    for src_ref, dst_ref in zip(cast_in, cast_out):
        dst_ref[...] = src_ref[...].astype(dst_ref.dtype)
    h_ref[...] = _swiglu(x_ref[...], wg_ref[...], wu_ref[...])


def _dense_down_kernel(h_ref, wd_ref, xres_ref, g_ref, b_ref, o_ref, ob_ref):
    y = jnp.dot(h_ref[...], wd_ref[...], preferred_element_type=F32)
    x2 = _layer_norm_rows(ALPHA * xres_ref[...] + y, g_ref[...], b_ref[...])
    o_ref[...] = x2
    ob_ref[...] = x2.astype(BF16)


def _ffn_dense(xb, wg, wu, wd, e, xres, ln_g, ln_b, cast=()):
    T, D = xb.shape
    F = wg.shape[2]
    tm = min(TM_DENSE, T)
    nb = T // tm
    fh = F // MOE_F_SPLIT
    resident = dict(pipeline_mode=pl.Buffered(1))
    in_specs = [pl.BlockSpec((tm, D), lambda f, i: (i, 0)),
                pl.BlockSpec((None, D, fh), lambda f, i: (e, 0, f), **resident),
                pl.BlockSpec((None, D, fh), lambda f, i: (e, 0, f), **resident)]
    out_specs = [pl.BlockSpec((tm, fh), lambda f, i: (i, f))]
    out_shape = [jax.ShapeDtypeStruct((T, F), BF16)]
    steps = MOE_F_SPLIT * nb
    for arr, part, n_parts in cast:
        rows, cols = arr.shape[0] // n_parts, arr.shape[1]
        rb = rows // steps
        assert rb * steps == rows and rb % 16 == 0, (rows, steps)
        in_specs.append(pl.BlockSpec((rb, cols), lambda f, i, off=part * steps: (off + f * nb + i, 0)))
        out_specs.append(pl.BlockSpec((rb, cols), lambda f, i: (f * nb + i, 0)))
        out_shape.append(jax.ShapeDtypeStruct((rows, cols), BF16))
    h, *cast_out = pl.pallas_call(
        functools.partial(_dense_up_kernel, n_cast=len(cast)),
        grid=(MOE_F_SPLIT, nb),
        in_specs=in_specs,
        out_specs=out_specs,
        out_shape=out_shape,
        compiler_params=_cparams(("arbitrary", "arbitrary")),
        name="dense_up",
    )(xb, wg, wu, *[c[0] for c in cast])
    rows_spec = pl.BlockSpec((tm, D), lambda i: (i, 0))
    vec = pl.BlockSpec((1, D), lambda i: (0, 0))
    x2, x2b = pl.pallas_call(
        _dense_down_kernel,
        grid=(nb,),
        in_specs=[pl.BlockSpec((tm, F), lambda i: (i, 0)),
                  pl.BlockSpec((None, F, D), lambda i: (e, 0, 0), **resident), rows_spec, vec, vec],
        out_specs=[rows_spec, rows_spec],
        out_shape=[jax.ShapeDtypeStruct((T, D), F32), jax.ShapeDtypeStruct((T, D), BF16)],
        compiler_params=_cparams(("arbitrary",)),
        name="dense_down",
    )(h, wd, xres, ln_g, ln_b)
    return (x2, x2b, *cast_out)


def _moe_up_kernel(be_ref, nv_ref, x_ref, wg_ref, wu_ref, h_ref):
    i = pl.program_id(1)

    @pl.when(i < nv_ref[0])
    def _():
        h_ref[...] = _swiglu(x_ref[...], wg_ref[...], wu_ref[...])

    @pl.when(i >= nv_ref[0])
    def _():
        h_ref[...] = jnp.zeros(h_ref.shape, h_ref.dtype)


def _moe_down_kernel(be_ref, nv_ref, *refs):
    h_ref, wd_ref, o_ref = refs[0], refs[1], refs[-1]
    i = pl.program_id(0)

    @pl.when(i < nv_ref[0])
    def _():
        o_ref[...] = jnp.dot(h_ref[...], wd_ref[...], preferred_element_type=F32).astype(o_ref.dtype)

    @pl.when(i >= nv_ref[0])
    def _():
        o_ref[...] = jnp.zeros(o_ref.shape, o_ref.dtype)


def _ffn_moe(x_rows_b, wg, wu, wd, block_e, n_valid, y_buf, blk0, total_rows):
    Rc, D = x_rows_b.shape
    F = wg.shape[2]
    tm = min(TM_FFN, Rc)
    nb = Rc // tm
    fh = F // MOE_F_SPLIT
    resident = dict(pipeline_mode=pl.Buffered(1))
    h = pl.pallas_call(
        _moe_up_kernel,
        grid_spec=pltpu.PrefetchScalarGridSpec(
            num_scalar_prefetch=2,
            grid=(MOE_F_SPLIT, nb),
            in_specs=[pl.BlockSpec((tm, D), lambda f, i, be, nv: (i, 0)),
                      pl.BlockSpec((None, D, fh), lambda f, i, be, nv: (be[i], 0, f), **resident),
                      pl.BlockSpec((None, D, fh), lambda f, i, be, nv: (be[i], 0, f), **resident)],
            out_specs=pl.BlockSpec((tm, fh), lambda f, i, be, nv: (i, f))),
        out_shape=jax.ShapeDtypeStruct((Rc, F), BF16),
        compiler_params=_cparams(("arbitrary", "arbitrary")),
        name="moe_up",
    )(block_e, n_valid, x_rows_b, wg, wu)
    in_specs = [pl.BlockSpec((tm, F), lambda i, be, nv: (i, 0)),
                pl.BlockSpec((None, F, D), lambda i, be, nv: (be[i], 0, 0), **resident)]
    args = [block_e, n_valid, h, wd]
    aliases = {}
    if y_buf is not None:
        in_specs.append(pl.BlockSpec(memory_space=pl.ANY))
        args.append(y_buf)
        aliases = {len(args) - 1: 0}
    return pl.pallas_call(
        _moe_down_kernel,
        grid_spec=pltpu.PrefetchScalarGridSpec(
            num_scalar_prefetch=2,
            grid=(nb,),
            in_specs=in_specs,
            out_specs=pl.BlockSpec((tm, D), lambda i, be, nv: (blk0 + i, 0))),
        out_shape=jax.ShapeDtypeStruct((total_rows, D), BF16),
        input_output_aliases=aliases,
        compiler_params=_cparams(("arbitrary",)),
        name="moe_down",
    )(*args)


def _combine_kernel(x_ref, y0_ref, y1_ref, g0_ref, g1_ref, g_ref, b_ref, *refs):
    o_ref, ob_ref = refs[-2:]
    ffn = y0_ref[...].astype(F32) * g0_ref[...] + y1_ref[...].astype(F32) * g1_ref[...]
    x2 = _layer_norm_rows(ALPHA * x_ref[...] + ffn, g_ref[...], b_ref[...])
    o_ref[...] = x2
    ob_ref[...] = x2.astype(BF16)


def _combine(x, y0, y1, g0, g1, ln_g, ln_b, part, bufs):
    T, D = x.shape
    Tp = y0.shape[0]
    tm = min(TM_LN, Tp)
    nb = Tp // tm
    rows_x = pl.BlockSpec((tm, D), lambda i: (part * nb + i, 0))
    rows = pl.BlockSpec((tm, D), lambda i: (i, 0))
    gate = pl.BlockSpec((tm, 1), lambda i: (i, 0))
    vec = pl.BlockSpec((1, D), lambda i: (0, 0))
    in_specs = [rows_x, rows, rows, gate, gate, vec, vec]
    args = [x, y0, y1, g0, g1, ln_g, ln_b]
    aliases = {}
    if bufs is not None:
        in_specs += [pl.BlockSpec(memory_space=pl.ANY)] * 2
        args += list(bufs)
        aliases = {len(args) - 2: 0, len(args) - 1: 1}
    return pl.pallas_call(
        _combine_kernel,
        grid=(nb,),
        in_specs=in_specs,
        out_specs=[rows_x, rows_x],
        out_shape=[jax.ShapeDtypeStruct((T, D), F32), jax.ShapeDtypeStruct((T, D), BF16)],
        input_output_aliases=aliases,
        compiler_params=_cparams(("parallel",)),
        name="moe_combine",
    )(*args)


def _pad_w_in(w_in):
    L, D, _ = w_in.shape
    q_dim = MLA_HEADS * (QK_NOPE + QK_ROPE)
    wq = w_in[:, :, :q_dim].reshape(L, D, MLA_HEADS, QK_NOPE + QK_ROPE)
    kr0 = q_dim + KV_RANK
    w_kr = w_in[:, :, kr0:kr0 + QK_ROPE]
    pad = jnp.zeros((L, D, MLA_HEADS, HEAD_W - QK_NOPE - QK_ROPE), w_in.dtype).at[:, :, 0, :].set(w_kr)
    q_part = jnp.concatenate([wq, pad], axis=-1).reshape(L, D, Q_COLS)
    rest = jnp.concatenate([w_in[:, :, q_dim:kr0], w_in[:, :, kr0 + QK_ROPE:]], axis=-1)
    return jnp.concatenate([q_part, rest], axis=-1).astype(BF16)


def _rope_tables(positions):
    inv_freq = ROPE_THETA ** (-jnp.arange(0, QK_ROPE, 2, dtype=F32) / QK_ROPE)
    ang = positions.astype(F32).reshape(-1)[:, None] * inv_freq
    cos, sin = jnp.cos(ang), jnp.sin(ang)
    cos4 = jnp.concatenate([cos, cos, cos, cos], axis=-1)
    sin4 = jnp.concatenate([-sin, sin, -sin, sin], axis=-1)
    return cos4, sin4


def _route(logits, tm):
    T = logits.shape[0]
    A = T * TOP_K
    top_logit, top_e = lax.top_k(logits, TOP_K)
    gates = jax.nn.softmax(top_logit, axis=-1)
    flat_e = jnp.concatenate([top_e[:, k] for k in range(TOP_K)]).astype(jnp.int32)
    onehot = (flat_e[:, None] == jnp.arange(N_EXPERTS, dtype=jnp.int32)[None, :]).astype(jnp.int32)
    csum = jnp.cumsum(onehot, axis=0)
    counts = csum[-1]
    rank = jnp.sum(csum * onehot, axis=1) - 1
    padded = (counts + tm - 1) // tm * tm
    pad_ends = jnp.cumsum(padded)
    pad_starts = pad_ends - padded
    dest = (pad_starts[flat_e] + rank).astype(jnp.int32)
    n_blocks = -(-(A + N_EXPERTS * (tm - 1)) // tm)
    n_pad = n_blocks * tm
    block_start = jnp.arange(n_blocks, dtype=jnp.int32) * tm
    block_e = jnp.minimum(jnp.searchsorted(pad_ends, block_start, side='right'), N_EXPERTS - 1).astype(jnp.int32)
    arange_a = jnp.arange(A, dtype=jnp.int32)
    _, order = lax.sort_key_val(flat_e * A + arange_a, arange_a)
    starts = jnp.cumsum(counts) - counts
    e_s = jnp.repeat(block_e, tm)
    r = jnp.arange(n_pad, dtype=jnp.int32) - pad_starts[e_s].astype(jnp.int32)
    src = jnp.clip(starts[e_s].astype(jnp.int32) + r, 0, A - 1)
    tok_sorted = order % T
    slot_tok = jnp.where(r < counts[e_s], tok_sorted.at[src].get(mode="promise_in_bounds"), 0).astype(jnp.int32)
    n_valid = (pad_ends[-1] // tm).astype(jnp.int32).reshape(1)
    return dest.reshape(TOP_K, T), gates, slot_tok, block_e, n_valid


def kernel(x, positions, w_in, kv_norm_g, w_ukv, conv_w, pool_w, sgu_ln_g, sgu_ln_b, sgu_w, sgu_b, mix_gain, w_o, ln1_g, ln1_b, ffn_wg, ffn_wu, ffn_wd, router_w, exp_wg, exp_wu, exp_wd, ln2_g, ln2_b):
    B, S, D = x.shape
    T = B * S
    L = w_in.shape[0]
    cos4, sin4 = _rope_tables(positions)
    w_in_b = _pad_w_in(w_in)
    w_ukv_b = w_ukv.astype(BF16)
    w_o_b = w_o.astype(BF16)
    pool_w_b = pool_w.astype(BF16)
    sgu_bt = jnp.swapaxes(sgu_b, 1, 2)
    router_pad = jnp.pad(router_w, ((0, 0), (0, 0), (0, LANES - N_EXPERTS)))
    router_hi = router_pad.astype(BF16)
    router_hl = jnp.concatenate([router_hi, (router_pad - router_hi.astype(F32)).astype(BF16)], axis=2)
    ffn_w = (ffn_wg.astype(BF16), ffn_wu.astype(BF16), ffn_wd.astype(BF16))
    n_moe = exp_wg.shape[0]
    exp_f32 = tuple(w.reshape(-1, w.shape[-1]) for w in (exp_wg, exp_wu, exp_wd))
    exp_w = None

    xf = x.reshape(T, D)
    xb = xf
    tm_moe = min(TM_FFN, T)
    for l in range(L):
        proj = _in_proj(xb, w_in_b, l)
        q_t, k_cat, v_t = _mla_prep(proj, cos4, sin4, kv_norm_g[l][None, :], w_ukv_b, l)
        o_mla = _attention(q_t, k_cat, v_t, B, S)
        yc, yp, ys = _mixers(proj, conv_w[l], pool_w_b[l], sgu_ln_g[l][None, :], sgu_ln_b[l][None, :],
                             sgu_w[l], sgu_bt[l], mix_gain[l][None, :], S)
        ln1 = (ln1_g[l][None, :], ln1_b[l][None, :])
        ln2 = (ln2_g[l][None, :], ln2_b[l][None, :])
        e = l // 2
        if l % 2 == 0:
            x1, x1b = _out_proj(o_mla, yc, yp, ys, mix_gain[l][None, :], w_o_b, l, xf, *ln1)
            cast = [(w, e, n_moe) for w in exp_f32] if e < n_moe else []
            xf, xb, *cast_out = _ffn_dense(x1b, *ffn_w, e, x1, *ln2, cast=cast)
            if cast_out:
                exp_w = tuple(c.reshape((N_EXPERTS, -1, c.shape[-1])) for c in cast_out)
        else:
            x1, x1b, logits = _out_proj(o_mla, yc, yp, ys, mix_gain[l][None, :], w_o_b, l, xf, *ln1,
                                        router_hl=router_hl[e])
            dest, gates, slot_tok, block_e, n_valid = _route(logits[:, :N_EXPERTS], tm_moe)
            n_blocks = block_e.shape[0]
            chunks = max(c for c in (MOE_CHUNKS, 2, 1) if n_blocks % c == 0)
            cb = n_blocks // chunks
            y_rows = None
            for c in range(chunks):
                x_rows = x1b.at[slot_tok[c * cb * tm_moe:(c + 1) * cb * tm_moe]].get(mode="promise_in_bounds")
                y_rows = _ffn_moe(x_rows, *exp_w, block_e[c * cb:(c + 1) * cb],
                                  jnp.clip(n_valid - c * cb, 0, cb), y_rows, c * cb, n_blocks * tm_moe)
            parts = COMBINE_PARTS if T % (COMBINE_PARTS * TM_LN) == 0 else 1
            tp = T // parts
            bufs = None
            for p in range(parts):
                ts = slice(p * tp, (p + 1) * tp)
                y0 = y_rows.at[dest[0, ts]].get(mode="promise_in_bounds")
                y1 = y_rows.at[dest[1, ts]].get(mode="promise_in_bounds")
                bufs = _combine(x1, y0, y1, gates[ts, 0:1], gates[ts, 1:2], *ln2, p, bufs)
            xf, xb = bufs
    return xf.reshape(B, S, D)
```
